```python
import math
import jax, jax.numpy as jnp
from jax import lax
import numpy as np

D_MODEL = 2048
BATCH = 8
SEQ = 2048
DEPTH = 1

HEAD_DIM = 128
A_Q_HEADS = 8
A_KV_HEADS = 2
A_GROUP = A_Q_HEADS // A_KV_HEADS
A_HALF_WINDOW = 128
A_BLOCK = 128
B_PATTERNS = ((128, 1), (512, 4), (2048, 16))
B_N_GROUPS = len(B_PATTERNS)
B_HEADS_PER_GROUP = 4
B_HEADS = B_N_GROUPS * B_HEADS_PER_GROUP
B_BLOCK = 64
N_BUCKETS = 32
MAX_DISTANCE = 1024
N_BIAS_HEADS = A_Q_HEADS + B_HEADS
A_Q_W = A_Q_HEADS * HEAD_DIM
A_KV_W = A_KV_HEADS * HEAD_DIM
B_W = B_HEADS * HEAD_DIM
B_OUT_W = B_HEADS_PER_GROUP * HEAD_DIM
IN_PROJ_W = A_Q_W + 2 * A_KV_W + 3 * B_W + 2 * D_MODEL
D_FF = 11 * D_MODEL // 4
CONV_WIDTH = 3
PLE_DIM = 256
RMS_EPS = 1e-6
NEG_INF = -1e30

kernel_name = "hybrid_gated_window_dilated_encoder"


def rms_norm(x, g):
    xf = x.astype(jnp.float32)
    y = xf * lax.rsqrt(jnp.mean(xf * xf, axis=-1, keepdims=True) + RMS_EPS)
    return (y * g.astype(jnp.float32)).astype(x.dtype)


def t5_bucket(rel):
    half = N_BUCKETS // 2
    max_exact = half // 2
    n = jnp.abs(rel)
    side = jnp.where(rel > 0, half, 0)
    nf = jnp.maximum(n, 1).astype(jnp.float32)
    large = max_exact + (jnp.log(nf / max_exact) / math.log(MAX_DISTANCE / max_exact)
                         * (half - max_exact)).astype(jnp.int32)
    large = jnp.minimum(large, half - 1)
    return side + jnp.where(n < max_exact, n, large)


def band_bias(table, half_w, blk, dilation):
    rel = (jnp.arange(blk + 2 * half_w)[None, :] - half_w) - jnp.arange(blk)[:, None]
    b = table[t5_bucket(rel * dilation)]
    return jnp.transpose(b, (2, 0, 1))


def banded_attention(q, k, v, bias, half_w, blk, sink=None):
    n, hkv, g, L, hd = q.shape
    nb = -(-L // blk)
    lp = nb * blk
    kw = blk + 2 * half_w
    q = jnp.pad(q, ((0, 0), (0, 0), (0, 0), (0, lp - L), (0, 0)))
    pad_kv = ((0, 0), (0, 0), (half_w, lp - L + half_w), (0, 0))
    k = jnp.pad(k, pad_kv)
    v = jnp.pad(v, pad_kv)
    kidx = jnp.arange(nb)[:, None] * blk + jnp.arange(kw)[None, :]
    kb = jnp.take(k, kidx, axis=2)
    vb = jnp.take(v, kidx, axis=2)
    qb = q.reshape(n, hkv, g, nb, blk, hd)
    s = jnp.einsum('nhgbqd,nhbkd->nhgbqk', qb, kb).astype(jnp.float32) * (hd ** -0.5)
    s = s + bias[None, :, :, None].astype(jnp.float32)
    kpos = (kidx - half_w)[:, None, :]
    qpos = (jnp.arange(nb)[:, None] * blk + jnp.arange(blk)[None, :])[:, :, None]
    valid = (jnp.abs(kpos - qpos) <= half_w) & (kpos >= 0) & (kpos < L)
    s = jnp.where(valid, s, NEG_INF)
    m = jnp.max(s, axis=-1)
    if sink is not None:
        sk = sink.astype(jnp.float32)[None, :, :, None, None]
        m = jnp.maximum(m, sk)
    pexp = jnp.exp(s - m[..., None])
    denom = jnp.sum(pexp, axis=-1)
    if sink is not None:
        denom = denom + jnp.exp(sk - m)
    out = jnp.einsum('nhgbqk,nhbkd->nhgbqd', pexp.astype(v.dtype), vb).astype(jnp.float32)
    out = (out / denom[..., None]).astype(v.dtype)
    lse = m + jnp.log(denom)
    out = out.reshape(n, hkv, g, lp, hd)[:, :, :, :L]
    lse = lse.reshape(n, hkv, g, lp)[:, :, :, :L]
    return out, lse


def windowed_gqa(q, k, v, table, sink):
    b, s, _ = q.shape
    q = q.reshape(b, s, A_KV_HEADS, A_GROUP, HEAD_DIM).transpose(0, 2, 3, 1, 4)
    k = k.reshape(b, s, A_KV_HEADS, HEAD_DIM).transpose(0, 2, 1, 3)
    v = v.reshape(b, s, A_KV_HEADS, HEAD_DIM).transpose(0, 2, 1, 3)
    bias = band_bias(table[:, :A_Q_HEADS], A_HALF_WINDOW, A_BLOCK, 1).reshape(
        A_KV_HEADS, A_GROUP, A_BLOCK, A_BLOCK + 2 * A_HALF_WINDOW)
    out, _ = banded_attention(q, k, v, bias, A_HALF_WINDOW, A_BLOCK,
                              sink.reshape(A_KV_HEADS, A_GROUP))
    return out.transpose(0, 3, 1, 2, 4).reshape(b, s, A_Q_W)


def to_residue(t, dil):
    b, s, h, hd = t.shape
    return t.reshape(b, s // dil, dil, h, hd).transpose(0, 2, 3, 1, 4).reshape(b * dil, h, s // dil, hd)


def dilated_attention(q, k, v, table):
    b, s, _ = q.shape
    hg = B_HEADS_PER_GROUP
    qg = q.reshape(b, s, B_N_GROUPS, hg, HEAD_DIM)
    kg = k.reshape(b, s, B_N_GROUPS, hg, HEAD_DIM)
    vg = v.reshape(b, s, B_N_GROUPS, hg, HEAD_DIM)
    outs = []
    lses = []
    for gi, (window, dil) in enumerate(B_PATTERNS):
        half = window // (2 * dil)
        L = s // dil
        h0 = A_Q_HEADS + gi * hg
        bias = band_bias(table[:, h0:h0 + hg], half, B_BLOCK, dil)[:, None]
        o, lse = banded_attention(to_residue(qg[:, :, gi], dil)[:, :, None],
                                  to_residue(kg[:, :, gi], dil),
                                  to_residue(vg[:, :, gi], dil), bias, half, B_BLOCK)
        outs.append(o[:, :, 0].reshape(b, dil, hg, L, HEAD_DIM).transpose(0, 3, 1, 2, 4).reshape(b, s, hg, HEAD_DIM))
        lses.append(lse[:, :, 0].reshape(b, dil, hg, L).transpose(0, 3, 1, 2).reshape(b, s, hg))
    alpha = jax.nn.softmax(jnp.stack(lses, axis=0), axis=0)
    y = jnp.sum(alpha[..., None] * jnp.stack(outs, axis=0).astype(jnp.float32), axis=0)
    return y.astype(q.dtype).reshape(b, s, B_OUT_W)


def dwconv_centred(t, w, bias):
    pad = CONV_WIDTH // 2
    s = t.shape[1]
    tp = jnp.pad(t, ((0, 0), (pad, pad), (0, 0)))
    acc = tp[:, 0:s] * w[0]
    for j in range(1, CONV_WIDTH):
        acc = acc + tp[:, j:j + s] * w[j]
    return acc + bias


def setup_inputs(seed: int = 0) -> dict:
    key = jax.random.key(seed)
    ks = jax.random.split(key, 20)
    f32 = jnp.float32

    def nrm(k, shape, scale):
        return jax.random.normal(k, shape, f32) * scale

    def gain(k, shape):
        return 1.0 + 0.05 * jax.random.normal(k, shape, f32)

    return {
        "x": nrm(ks[0], (BATCH, SEQ, D_MODEL), 1.0),
        "p": nrm(ks[1], (DEPTH, BATCH, SEQ, PLE_DIM), 1.0),
        "rel_bias_table": nrm(ks[2], (N_BUCKETS, N_BIAS_HEADS), 0.5),
        "attn_norm": gain(ks[3], (DEPTH, D_MODEL)),
        "w_in": nrm(ks[4], (DEPTH, D_MODEL, IN_PROJ_W), D_MODEL ** -0.5),
        "sink_a": nrm(ks[5], (DEPTH, A_Q_HEADS), 0.5),
        "w_branch_a": nrm(ks[6], (DEPTH, A_Q_W, D_MODEL), A_Q_W ** -0.5),
        "w_branch_b": nrm(ks[7], (DEPTH, B_OUT_W, D_MODEL), B_OUT_W ** -0.5),
        "w_out": nrm(ks[8], (DEPTH, D_MODEL, D_MODEL), D_MODEL ** -0.5),
        "ffn_norm": gain(ks[9], (DEPTH, D_MODEL)),
        "w_ffn_gate": nrm(ks[10], (DEPTH, D_MODEL, D_FF), D_MODEL ** -0.5),
        "w_ffn_up": nrm(ks[11], (DEPTH, D_MODEL, D_FF), D_MODEL ** -0.5),
        "conv_w": nrm(ks[12], (DEPTH, CONV_WIDTH, D_FF), CONV_WIDTH ** -0.5),
        "conv_b": nrm(ks[13], (DEPTH, D_FF), 0.02),
        "w_ffn_down": nrm(ks[14], (DEPTH, D_FF, D_MODEL), D_FF ** -0.5),
        "ple_norm": gain(ks[15], (DEPTH, D_MODEL)),
        "w_ple_gate": nrm(ks[16], (DEPTH, D_MODEL, D_MODEL), D_MODEL ** -0.5),
        "w_ple_proj": nrm(ks[17], (DEPTH, PLE_DIM, D_MODEL), PLE_DIM ** -0.5),
        "final_norm": gain(ks[18], (D_MODEL,)),
    }


def reference(x, p, rel_bias_table, attn_norm, w_in, sink_a, w_branch_a, w_branch_b, w_out,
              ffn_norm, w_ffn_gate, w_ffn_up, conv_w, conv_b, w_ffn_down,
              ple_norm, w_ple_gate, w_ple_proj, final_norm):
    split_points = np.cumsum([A_Q_W, A_KV_W, A_KV_W, B_W, B_W, B_W, D_MODEL]).tolist()
    for i in range(DEPTH):
        h = rms_norm(x, attn_norm[i])
        proj = h @ w_in[i]
        qa, ka, va, qb, kb, vb, ga, gb = jnp.split(proj, split_points, axis=-1)
        ya = windowed_gqa(qa, ka, va, rel_bias_table, sink_a[i])
        yb = dilated_attention(qb, kb, vb, rel_bias_table)
        merged = jax.nn.sigmoid(ga) * (ya @ w_branch_a[i]) + jax.nn.sigmoid(gb) * (yb @ w_branch_b[i])
        x = x + merged @ w_out[i]
        hf = rms_norm(x, ffn_norm[i])
        g = dwconv_centred(hf @ w_ffn_gate[i], conv_w[i], conv_b[i])
        x = x + (jax.nn.gelu(g) * (hf @ w_ffn_up[i])) @ w_ffn_down[i]
        gate_p = jax.nn.sigmoid(rms_norm(x, ple_norm[i]) @ w_ple_gate[i])
        x = x + gate_p * (p[i] @ w_ple_proj[i])
    return rms_norm(x, final_norm)
```

```python
import functools
import math

import jax
import jax.numpy as jnp
import numpy as np
from jax import lax
from jax.experimental import pallas as pl
from jax.experimental.pallas import tpu as pltpu

D_MODEL = 2048
HEAD_DIM = 128
A_Q_HEADS = 8
A_KV_HEADS = 2
A_GROUP = A_Q_HEADS // A_KV_HEADS
A_HALF_WINDOW = 128
B_PATTERNS = ((128, 1), (512, 4), (2048, 16))
B_HEADS_PER_GROUP = 4
N_BUCKETS = 32
MAX_DISTANCE = 1024
A_Q_W = A_Q_HEADS * HEAD_DIM
A_KV_W = A_KV_HEADS * HEAD_DIM
B_GROUP_W = B_HEADS_PER_GROUP * HEAD_DIM
B_W = len(B_PATTERNS) * B_GROUP_W
IN_PROJ_W = A_Q_W + 2 * A_KV_W + 3 * B_W + 2 * D_MODEL
RMS_EPS = 1e-6
NEG_INF = -1e30

COL_QA = 0
COL_KA = COL_QA + A_Q_W
COL_VA = COL_KA + A_KV_W
COL_QB = COL_VA + A_KV_W
COL_KB = COL_QB + B_W
COL_VB = COL_KB + B_W
COL_GA = COL_VB + B_W
COL_GB = COL_GA + D_MODEL

Q_BLOCK = 128
HEADS_PER_STEP = 4

VMEM_LIMIT_BYTES = 56 * 1024 * 1024

BF16 = jnp.bfloat16
F32 = jnp.float32


def _params(*semantics):
    return pltpu.CompilerParams(dimension_semantics=semantics, vmem_limit_bytes=VMEM_LIMIT_BYTES)


def _rmsnorm(x, g):
    y = x * lax.rsqrt(jnp.mean(x * x, axis=-1, keepdims=True) + RMS_EPS)
    return y * g


def _inproj_kernel(x_ref, g_ref, w_ref, o_ref, h_ref, *, row_chunk):
    @pl.when(pl.program_id(1) == 0)
    def _():
        def body(c, _):
            rows = pl.ds(pl.multiple_of(c * row_chunk, row_chunk), row_chunk)
            h_ref[rows, :] = _rmsnorm(x_ref[rows, :], g_ref[...]).astype(BF16)
            return 0

        lax.fori_loop(0, x_ref.shape[0] // row_chunk, body, 0)

    o_ref[...] = jnp.dot(h_ref[...], w_ref[...], preferred_element_type=F32).astype(o_ref.dtype)


def _inproj(x2d, gain, w, *, tm=1024, tn=1024):
    m, k = x2d.shape
    n = w.shape[1]
    return pl.pallas_call(
        functools.partial(_inproj_kernel, row_chunk=128),
        grid=(m // tm, n // tn),
        in_specs=[
            pl.BlockSpec((tm, k), lambda i, j: (i, 0)),
            pl.BlockSpec((1, k), lambda i, j: (0, 0)),
            pl.BlockSpec((k, tn), lambda i, j: (0, j)),
        ],
        out_specs=pl.BlockSpec((tm, tn), lambda i, j: (i, j)),
        out_shape=jax.ShapeDtypeStruct((m, n), BF16),
        scratch_shapes=[pltpu.VMEM((tm, k), BF16)],
        compiler_params=_params("parallel", "arbitrary"),
        name="inproj",
    )(x2d, gain, w)


def _t5_bucket_static(rel):
    half = N_BUCKETS // 2
    max_exact = half // 2
    n = np.abs(rel)
    side = np.where(rel > 0, half, 0)
    nf = np.maximum(n, 1).astype(np.float32)
    large = max_exact + (np.log(nf / max_exact) / math.log(MAX_DISTANCE / max_exact)
                         * (half - max_exact)).astype(np.int32)
    large = np.minimum(large, half - 1)
    return side + np.where(n < max_exact, n, large)


def _window_geometry(seq_len):
    kw = min(3 * Q_BLOCK, seq_len)
    nb = seq_len // Q_BLOCK
    offsets = (0,) if nb == 1 else (0, -Q_BLOCK, Q_BLOCK - kw)
    return kw, nb, offsets


def _band_bias(table_heads, seq_len, half_w, dilation):
    kw, _, offsets = _window_geometry(seq_len)
    qi = np.arange(Q_BLOCK)[:, None]
    kj = np.arange(kw)[None, :]
    rel = np.stack([off + kj - qi for off in offsets])
    buckets = _t5_bucket_static(rel * dilation)
    valid = np.abs(rel) <= half_w
    b = jnp.take(table_heads, jnp.asarray(buckets, jnp.int32), axis=0)
    b = jnp.transpose(b, (0, 3, 1, 2))
    return jnp.where(jnp.asarray(valid)[:, None], b, NEG_INF)


def _attn_kernel(*refs, seq_len, shared_kv, has_sink, has_lse, head_base):
    it = iter(refs)
    q_ref, k_ref, v_ref, bias_ref = next(it), next(it), next(it), next(it)
    sink_ref = next(it) if has_sink else None
    o_ref = next(it)
    lse_ref = next(it) if has_lse else None

    kw, nb, offsets = _window_geometry(seq_len)
    scale = HEAD_DIM ** -0.5
    lane = lax.broadcasted_iota(jnp.int32, (Q_BLOCK, HEAD_DIM), 1)

    def block(i, _):
        qs = pl.multiple_of(i * Q_BLOCK, Q_BLOCK)
        if nb == 1:
            ks, var = 0, 0
        else:
            ks = pl.multiple_of(jnp.clip((i - 1) * Q_BLOCK, 0, seq_len - kw), Q_BLOCK)
            var = jnp.where(i == 0, 0, jnp.where(i == nb - 1, 2, 1))
        lse_tile = jnp.zeros((Q_BLOCK, HEAD_DIM), F32)
        for h in range(HEADS_PER_STEP):
            cols = slice(h * HEAD_DIM, (h + 1) * HEAD_DIM)
            kcols = slice(0, HEAD_DIM) if shared_kv else cols
            q = q_ref[0, pl.ds(qs, Q_BLOCK), cols]
            k = k_ref[0, pl.ds(ks, kw), kcols]
            v = v_ref[0, pl.ds(ks, kw), kcols]
            s = lax.dot_general(q, k, (((1,), (1,)), ((), ())), preferred_element_type=F32)
            s = s * scale + bias_ref[var, h]
            m = jnp.max(s, axis=-1, keepdims=True)
            if has_sink:
                sk = sink_ref[head_base(pl.program_id(1)) + h]
                m = jnp.maximum(m, sk)
            p = jnp.exp(s - m)
            denom = jnp.sum(p, axis=-1, keepdims=True)
            if has_sink:
                denom = denom + jnp.exp(sk - m)
            o = jnp.dot(p.astype(BF16), v, preferred_element_type=F32) / denom
            o_ref[0, pl.ds(qs, Q_BLOCK), cols] = o.astype(o_ref.dtype)
            if has_lse:
                lse_tile = jnp.where(lane == h, m + jnp.log(denom), lse_tile)
        if has_lse:
            lse_ref[0, pl.ds(qs, Q_BLOCK), :] = lse_tile
        return 0

    lax.fori_loop(0, nb, block, 0)


def _attn_a(proj, table, sink):
    b, s, _ = proj.shape
    gw = A_GROUP * HEAD_DIM
    kw, _, offsets = _window_geometry(s)
    bias = _band_bias(table[:, :A_Q_HEADS], s, A_HALF_WINDOW, 1)
    bias = bias.reshape(len(offsets), A_KV_HEADS, A_GROUP, Q_BLOCK, kw).transpose(1, 0, 2, 3, 4)
    kernel = functools.partial(_attn_kernel, seq_len=s, shared_kv=True, has_sink=True,
                               has_lse=False, head_base=lambda kv: kv * A_GROUP)
    return pl.pallas_call(
        kernel,
        grid=(b, A_KV_HEADS),
        in_specs=[
            pl.BlockSpec((1, s, gw), lambda i, j: (i, 0, COL_QA // gw + j)),
            pl.BlockSpec((1, s, HEAD_DIM), lambda i, j: (i, 0, COL_KA // HEAD_DIM + j)),
            pl.BlockSpec((1, s, HEAD_DIM), lambda i, j: (i, 0, COL_VA // HEAD_DIM + j)),
            pl.BlockSpec((None, len(offsets), A_GROUP, Q_BLOCK, kw), lambda i, j: (j, 0, 0, 0, 0)),
            pl.BlockSpec(memory_space=pltpu.SMEM),
        ],
        out_specs=pl.BlockSpec((1, s, gw), lambda i, j: (i, 0, j)),
        out_shape=jax.ShapeDtypeStruct((b, s, A_Q_W), BF16),
        compiler_params=_params("parallel", "parallel"),
        name="attn_a",
    )(proj, proj, proj, bias, sink)


def _attn_b_group(proj, table, gi):
    window, dil = B_PATTERNS[gi]
    b, s, width = proj.shape
    sub = s // dil
    half = window // (2 * dil)
    kw, _, offsets = _window_geometry(sub)
    h0 = A_Q_HEADS + gi * B_HEADS_PER_GROUP
    bias = _band_bias(table[:, h0:h0 + B_HEADS_PER_GROUP], sub, half, dil)
    view = proj.reshape(b, sub, dil * width)
    per_res = width // B_GROUP_W

    def col(base):
        return lambda i, r: (i, 0, r * per_res + (base + gi * B_GROUP_W) // B_GROUP_W)

    kernel = functools.partial(_attn_kernel, seq_len=sub, shared_kv=False, has_sink=False,
                               has_lse=True, head_base=None)
    out, lse = pl.pallas_call(
        kernel,
        grid=(b, dil),
        in_specs=[
            pl.BlockSpec((1, sub, B_GROUP_W), col(COL_QB)),
            pl.BlockSpec((1, sub, B_GROUP_W), col(COL_KB)),
            pl.BlockSpec((1, sub, B_GROUP_W), col(COL_VB)),
            pl.BlockSpec((len(offsets), B_HEADS_PER_GROUP, Q_BLOCK, kw), lambda i, r: (0, 0, 0, 0)),
        ],
        out_specs=[
            pl.BlockSpec((1, sub, B_GROUP_W), lambda i, r: (i, 0, r)),
            pl.BlockSpec((1, sub, HEAD_DIM), lambda i, r: (i, 0, r)),
        ],
        out_shape=[
            jax.ShapeDtypeStruct((b, sub, dil * B_GROUP_W), BF16),
            jax.ShapeDtypeStruct((b, sub, dil * HEAD_DIM), F32),
        ],
        compiler_params=_params("parallel", "parallel"),
        name=f"attn_b{gi}",
    )(view, view, view, bias)
    return out.reshape(b, s, B_GROUP_W), lse.reshape(b, s, HEAD_DIM)


def _mix_kernel(ya_ref, o0_ref, o1_ref, o2_ref, l0_ref, l1_ref, l2_ref, ga_ref, gb_ref, x_ref,
                wa_ref, wb_ref, wo_ref, g_ref, x1_ref, hf_ref):
    heads = []
    for h in range(B_HEADS_PER_GROUP):
        cols = slice(h * HEAD_DIM, (h + 1) * HEAD_DIM)
        l0, l1, l2 = (r[:, h:h + 1] for r in (l0_ref, l1_ref, l2_ref))
        mx = jnp.maximum(jnp.maximum(l0, l1), l2)
        e0, e1, e2 = jnp.exp(l0 - mx), jnp.exp(l1 - mx), jnp.exp(l2 - mx)
        tot = e0 + e1 + e2
        y = ((e0 / tot) * o0_ref[:, cols].astype(F32) + (e1 / tot) * o1_ref[:, cols].astype(F32)
             + (e2 / tot) * o2_ref[:, cols].astype(F32))
        heads.append(y.astype(BF16))
    yb = jnp.concatenate(heads, axis=-1)
    ta = jnp.dot(ya_ref[...], wa_ref[...], preferred_element_type=F32)
    tb = jnp.dot(yb, wb_ref[...], preferred_element_type=F32)
    merged = (jax.nn.sigmoid(ga_ref[...].astype(F32)) * ta
              + jax.nn.sigmoid(gb_ref[...].astype(F32)) * tb)
    x1 = x_ref[...] + jnp.dot(merged.astype(BF16), wo_ref[...], preferred_element_type=F32)
    x1_ref[...] = x1
    hf_ref[...] = _rmsnorm(x1, g_ref[...]).astype(BF16)


def _resident(shape):
    return pl.BlockSpec(shape, lambda i: (0,) * len(shape), pipeline_mode=pl.Buffered(1))


def _mix(ya, outs, lses, proj2d, x2d, wa, wb, wo, gain, *, tm=256):
    m, d = x2d.shape

    def rows(width, col_block=0):
        return pl.BlockSpec((tm, width), lambda i: (i, col_block))

    return pl.pallas_call(
        _mix_kernel,
        grid=(m // tm,),
        in_specs=[
            rows(A_Q_W),
            rows(B_GROUP_W), rows(B_GROUP_W), rows(B_GROUP_W),
            rows(HEAD_DIM), rows(HEAD_DIM), rows(HEAD_DIM),
            rows(d, COL_GA // d), rows(d, COL_GB // d),
            rows(d),
            _resident(wa.shape), _resident(wb.shape), _resident(wo.shape), _resident(gain.shape),
        ],
        out_specs=[rows(d), rows(d)],
        out_shape=[jax.ShapeDtypeStruct((m, d), F32), jax.ShapeDtypeStruct((m, d), BF16)],
        compiler_params=_params("parallel"),
        name="mix",
    )(ya, *outs, *lses, proj2d, proj2d, x2d, wa, wb, wo, gain)


def _ffn_up_kernel(hf_ref, wg_ref, wu_ref, cw_ref, cb_ref, act_ref):
    hf = hf_ref[...]
    g = jnp.dot(hf, wg_ref[...], preferred_element_type=F32)
    u = jnp.dot(hf, wu_ref[...], preferred_element_type=F32)
    s = g.shape[0]
    row = lax.broadcasted_iota(jnp.int32, g.shape, 0)
    prev = jnp.where(row == 0, 0.0, pltpu.roll(g, 1, 0))
    nxt = jnp.where(row == s - 1, 0.0, pltpu.roll(g, s - 1, 0))
    cw = cw_ref[...]
    conv = prev * cw[0:1] + g * cw[1:2] + nxt * cw[2:3] + cb_ref[...]
    act_ref[...] = (jax.nn.gelu(conv) * u).astype(act_ref.dtype)


def _ffn_up(hf, wg, wu, cw, cb, *, seq, tf=512):
    m, d = hf.shape
    f = wg.shape[1]
    return pl.pallas_call(
        _ffn_up_kernel,
        grid=(m // seq, f // tf),
        in_specs=[
            pl.BlockSpec((seq, d), lambda i, j: (i, 0)),
            pl.BlockSpec((d, tf), lambda i, j: (0, j)),
            pl.BlockSpec((d, tf), lambda i, j: (0, j)),
            pl.BlockSpec((cw.shape[0], tf), lambda i, j: (0, j)),
            pl.BlockSpec((1, tf), lambda i, j: (0, j)),
        ],
        out_specs=pl.BlockSpec((seq, tf), lambda i, j: (i, j)),
        out_shape=jax.ShapeDtypeStruct((m, f), BF16),
        compiler_params=_params("parallel", "arbitrary"),
        name="ffn_up",
    )(hf, wg, wu, cw, cb)


def _ffn_down_kernel(act_ref, w_ref, x_ref, o_ref):
    o_ref[...] = x_ref[...] + jnp.dot(act_ref[...], w_ref[...], preferred_element_type=F32)


def _ffn_down(act, wd, x1, *, tm=512, tn=512):
    m, f = act.shape
    n = wd.shape[1]
    return pl.pallas_call(
        _ffn_down_kernel,
        grid=(m // tm, n // tn),
        in_specs=[
            pl.BlockSpec((tm, f), lambda i, j: (i, 0)),
            pl.BlockSpec((f, tn), lambda i, j: (0, j)),
            pl.BlockSpec((tm, tn), lambda i, j: (i, j)),
        ],
        out_specs=pl.BlockSpec((tm, tn), lambda i, j: (i, j)),
        out_shape=jax.ShapeDtypeStruct((m, n), F32),
        compiler_params=_params("parallel", "arbitrary"),
        name="ffn_down",
    )(act, wd, x1)


def _ple_kernel(x_ref, p_ref, gp_ref, wg_ref, wp_ref, gf_ref, o_ref, *, final):
    x = x_ref[...]
    hp = _rmsnorm(x, gp_ref[...]).astype(BF16)
    gate = jax.nn.sigmoid(jnp.dot(hp, wg_ref[...], preferred_element_type=F32))
    emb = jnp.dot(p_ref[...].astype(BF16), wp_ref[...], preferred_element_type=F32)
    y = x + gate * emb
    o_ref[...] = _rmsnorm(y, gf_ref[...]) if final else y


def _ple(x2, p2d, gain_p, wpg, wpp, gain_f, *, final, tm=256):
    m, d = x2.shape
    pd = p2d.shape[1]
    return pl.pallas_call(
        functools.partial(_ple_kernel, final=final),
        grid=(m // tm,),
        in_specs=[
            pl.BlockSpec((tm, d), lambda i: (i, 0)),
            pl.BlockSpec((tm, pd), lambda i: (i, 0)),
            _resident(gain_p.shape), _resident(wpg.shape), _resident(wpp.shape),
            _resident(gain_f.shape),
        ],
        out_specs=pl.BlockSpec((tm, d), lambda i: (i, 0)),
        out_shape=jax.ShapeDtypeStruct((m, d), F32),
        compiler_params=_params("parallel"),
        name="ple",
    )(x2, p2d, gain_p, wpg, wpp, gain_f)


def kernel(x, p, rel_bias_table, attn_norm, w_in, sink_a, w_branch_a, w_branch_b, w_out,
           ffn_norm, w_ffn_gate, w_ffn_up, conv_w, conv_b, w_ffn_down,
           ple_norm, w_ple_gate, w_ple_proj, final_norm):
    b, s, d = x.shape
    depth = w_in.shape[0]
    assert d == D_MODEL and w_in.shape[2] == IN_PROJ_W and s % (Q_BLOCK * B_PATTERNS[-1][1]) == 0
    m = b * s
    x2d = x.reshape(m, d)
    for i in range(depth):
        proj2d = _inproj(x2d, attn_norm[i][None], w_in[i].astype(BF16))
        proj = proj2d.reshape(b, s, IN_PROJ_W)
        ya = _attn_a(proj, rel_bias_table, sink_a[i]).reshape(m, A_Q_W)
        groups = [_attn_b_group(proj, rel_bias_table, gi) for gi in range(len(B_PATTERNS))]
        outs = [o.reshape(m, B_GROUP_W) for o, _ in groups]
        lses = [l.reshape(m, HEAD_DIM) for _, l in groups]
        x1, hf = _mix(ya, outs, lses, proj2d, x2d, w_branch_a[i].astype(BF16),
                      w_branch_b[i].astype(BF16), w_out[i].astype(BF16), ffn_norm[i][None])
        act = _ffn_up(hf, w_ffn_gate[i].astype(BF16), w_ffn_up[i].astype(BF16), conv_w[i],
                      conv_b[i][None], seq=s)
        x2 = _ffn_down(act, w_ffn_down[i].astype(BF16), x1)
        x2d = _ple(x2, p[i].reshape(m, -1), ple_norm[i][None], w_ple_gate[i].astype(BF16),
                   w_ple_proj[i].astype(BF16), final_norm[None], final=i == depth - 1)
    return x2d.reshape(b, s, d)
```

```python
import functools
import math

import jax
import jax.numpy as jnp
import numpy as np
from jax import lax
from jax.experimental import pallas as pl
from jax.experimental.pallas import tpu as pltpu

D_MODEL = 2048
HEAD_DIM = 128
A_Q_HEADS = 8
A_KV_HEADS = 2
A_GROUP = A_Q_HEADS // A_KV_HEADS
A_HALF_WINDOW = 128
B_PATTERNS = ((128, 1), (512, 4), (2048, 16))
B_HEADS_PER_GROUP = 4
N_BUCKETS = 32
MAX_DISTANCE = 1024
N_BIAS_HEADS = A_Q_HEADS + len(B_PATTERNS) * B_HEADS_PER_GROUP
A_Q_W = A_Q_HEADS * HEAD_DIM
A_KV_W = A_KV_HEADS * HEAD_DIM
B_GROUP_W = B_HEADS_PER_GROUP * HEAD_DIM
B_W = len(B_PATTERNS) * B_GROUP_W
IN_PROJ_W = A_Q_W + 2 * A_KV_W + 3 * B_W + 2 * D_MODEL
RMS_EPS = 1e-6
NEG_INF = -1e30

W_COL_B = A_Q_W + 2 * A_KV_W
W_COL_G = W_COL_B + 3 * B_W
PA_W = 2 * D_MODEL + W_COL_B
PA_COL_GA = 0
PA_COL_GB = D_MODEL
PA_COL_QA = 2 * D_MODEL
PA_COL_KA = PA_COL_QA + A_Q_W
PA_COL_VA = PA_COL_KA + A_KV_W
PB_W = 3 * B_W

Q_BLOCK = 128
HEADS_PER_STEP = 4

VMEM_LIMIT_BYTES = 56 * 1024 * 1024

BF16 = jnp.bfloat16
F32 = jnp.float32


def _params(*semantics):
    return pltpu.CompilerParams(dimension_semantics=semantics, vmem_limit_bytes=VMEM_LIMIT_BYTES)


def _rmsnorm(x, g):
    y = x * lax.rsqrt(jnp.mean(x * x, axis=-1, keepdims=True) + RMS_EPS)
    return y * g


def _inproj_kernel(x_ref, g_ref, w_ref, pb_ref, pa_ref, h_ref, *, row_chunk, n_b_tiles):
    j = pl.program_id(1)

    @pl.when(j == 0)
    def _():
        def body(c, _):
            rows = pl.ds(pl.multiple_of(c * row_chunk, row_chunk), row_chunk)
            h_ref[rows, :] = _rmsnorm(x_ref[rows, :], g_ref[...]).astype(BF16)
            return 0

        lax.fori_loop(0, x_ref.shape[0] // row_chunk, body, 0)

    @pl.when(j < n_b_tiles)
    def _():
        pb_ref[...] = jnp.dot(h_ref[...], w_ref[...], preferred_element_type=F32)

    @pl.when(j >= n_b_tiles)
    def _():
        pa_ref[...] = jnp.dot(h_ref[...], w_ref[...], preferred_element_type=F32).astype(BF16)


def _inproj(x2d, gain, w, *, tm=1024, tn=512):
    m, k = x2d.shape
    nb_t = PB_W // tn
    head_t = W_COL_B // tn
    n_steps = w.shape[1] // tn

    def w_col(j):
        return jnp.where(j < n_steps - head_t, j + head_t, j - (n_steps - head_t))

    return pl.pallas_call(
        functools.partial(_inproj_kernel, row_chunk=128, n_b_tiles=nb_t),
        grid=(m // tm, n_steps),
        in_specs=[
            pl.BlockSpec((tm, k), lambda i, j: (i, 0)),
            pl.BlockSpec((1, k), lambda i, j: (0, 0)),
            pl.BlockSpec((k, tn), lambda i, j: (0, w_col(j))),
        ],
        out_specs=[
            pl.BlockSpec((tm, tn), lambda i, j: (i, jnp.minimum(j, nb_t - 1))),
            pl.BlockSpec((tm, tn), lambda i, j: (i, jnp.maximum(j - nb_t, 0))),
        ],
        out_shape=[jax.ShapeDtypeStruct((m, PB_W), F32), jax.ShapeDtypeStruct((m, PA_W), BF16)],
        scratch_shapes=[pltpu.VMEM((tm, k), BF16)],
        compiler_params=_params("arbitrary", "arbitrary"),
        name="inproj",
    )(x2d, gain, w)


def _t5_bucket_static(rel):
    half = N_BUCKETS // 2
    max_exact = half // 2
    n = np.abs(rel)
    side = np.where(rel > 0, half, 0)
    nf = np.maximum(n, 1).astype(np.float32)
    large = max_exact + (np.log(nf / max_exact) / math.log(MAX_DISTANCE / max_exact)
                         * (half - max_exact)).astype(np.int32)
    large = np.minimum(large, half - 1)
    return side + np.where(n < max_exact, n, large)


def _window_geometry(seq_len):
    kw = min(3 * Q_BLOCK, seq_len)
    nb = seq_len // Q_BLOCK
    offsets = (0,) if nb == 1 else (0, -Q_BLOCK, Q_BLOCK - kw)
    return kw, nb, offsets


def _band_buckets(seq_len, half_w, dilation):
    kw, _, offsets = _window_geometry(seq_len)
    qi = np.arange(Q_BLOCK)[:, None]
    kj = np.arange(kw)[None, :]
    rel = np.stack([off + kj - qi for off in offsets])
    buckets = _t5_bucket_static(rel * dilation)
    return np.where(np.abs(rel) <= half_w, buckets, -1).astype(np.int32)


def _attn_kernel(*refs, seq_len, dilation, n_q_refs, n_kv_refs, n_o_refs, has_sink, has_lse,
                 head0, head_sets, buckets_used):
    it = iter(refs)
    bkt_ref, table_ref = next(it), next(it)
    sink_ref = next(it) if has_sink else None
    q_refs = [next(it) for _ in range(n_q_refs)]
    k_refs = [next(it) for _ in range(n_kv_refs)]
    v_refs = [next(it) for _ in range(n_kv_refs)]
    o_refs = [next(it) for _ in range(n_o_refs)]
    lse_ref = next(it) if has_lse else None
    bias_ref = next(it)

    kw, nb, offsets = _window_geometry(seq_len)
    scale = HEAD_DIM ** -0.5
    step = pl.program_id(1)
    residue = step if dilation > 1 else 0
    first_head = head0(step)
    head_set = step if head_sets > 1 else 0

    @pl.when((pl.program_id(0) == 0) & (step < head_sets))
    def _():
        for var in range(len(offsets)):
            bkt = bkt_ref[var]
            for h in range(HEADS_PER_STEP):
                acc = jnp.full(bkt.shape, NEG_INF, F32)
                for bucket in buckets_used:
                    acc = jnp.where(bkt == bucket,
                                    table_ref[bucket * N_BIAS_HEADS + first_head + h], acc)
                bias_ref[head_set, var, h] = acc

    def rows(start, size):
        if dilation == 1:
            return pl.ds(start, size)
        return pl.ds(residue + start * dilation, size, stride=dilation)

    def head_view(ref_list, h):
        if len(ref_list) > 1:
            return ref_list[h], slice(None)
        if ref_list[0].shape[-1] == HEAD_DIM:
            return ref_list[0], slice(None)
        return ref_list[0], slice(h * HEAD_DIM, (h + 1) * HEAD_DIM)

    lane = lax.broadcasted_iota(jnp.int32, (Q_BLOCK, HEAD_DIM), 1)

    def block(i, _):
        qs = pl.multiple_of(i * Q_BLOCK, Q_BLOCK)
        if nb == 1:
            ks, var = 0, 0
        else:
            ks = pl.multiple_of(jnp.clip((i - 1) * Q_BLOCK, 0, seq_len - kw), Q_BLOCK)
            var = jnp.where(i == 0, 0, jnp.where(i == nb - 1, 2, 1))
        lse_tile = jnp.zeros((Q_BLOCK, HEAD_DIM), F32)
        for h in range(HEADS_PER_STEP):
            q_ref, q_cols = head_view(q_refs, h)
            k_ref, k_cols = head_view(k_refs, h)
            v_ref, _ = head_view(v_refs, h)
            o_ref, o_cols = head_view(o_refs, h)
            q = q_ref[0, rows(qs, Q_BLOCK), q_cols].astype(BF16)
            k = k_ref[0, rows(ks, kw), k_cols].astype(BF16)
            v = v_ref[0, rows(ks, kw), k_cols].astype(BF16)
            s = lax.dot_general(q, k, (((1,), (1,)), ((), ())), preferred_element_type=F32)
            s = s * scale + bias_ref[head_set, var, h]
            m = jnp.max(s, axis=-1, keepdims=True)
            if has_sink:
                sk = sink_ref[first_head + h]
                m = jnp.maximum(m, sk)
            p = jnp.exp(s - m)
            denom = jnp.sum(p, axis=-1, keepdims=True)
            if has_sink:
                denom = denom + jnp.exp(sk - m)
            o = jnp.dot(p.astype(BF16), v, preferred_element_type=F32) / denom
            o_ref[0, rows(qs, Q_BLOCK), o_cols] = o.astype(o_ref.dtype)
            if has_lse:
                lse_tile = jnp.where(lane == h, m + jnp.log(denom), lse_tile)
        if has_lse:
            lse_ref[0, rows(qs, Q_BLOCK), :] = lse_tile
        return 0

    lax.fori_loop(0, nb, block, 0)


def _buckets_used(bkt):
    return tuple(int(v) for v in np.unique(bkt) if v >= 0)


def _smem_spec():
    return pl.BlockSpec(memory_space=pltpu.SMEM)


def _attn_a(pa, table_flat, sink):
    b, s, _ = pa.shape
    gw = A_GROUP * HEAD_DIM
    kw, _, offsets = _window_geometry(s)
    bkt = _band_buckets(s, A_HALF_WINDOW, 1)
    kernel = functools.partial(
        _attn_kernel, seq_len=s, dilation=1, n_q_refs=1, n_kv_refs=1, n_o_refs=1, has_sink=True,
        has_lse=False, head0=lambda kv: kv * A_GROUP, head_sets=A_KV_HEADS,
        buckets_used=_buckets_used(bkt))
    return pl.pallas_call(
        kernel,
        grid=(b, A_KV_HEADS),
        in_specs=[
            pl.BlockSpec(bkt.shape, lambda i, j: (0, 0, 0)),
            _smem_spec(),
            _smem_spec(),
            pl.BlockSpec((1, s, gw), lambda i, j: (i, 0, PA_COL_QA // gw + j)),
            pl.BlockSpec((1, s, HEAD_DIM), lambda i, j: (i, 0, PA_COL_KA // HEAD_DIM + j)),
            pl.BlockSpec((1, s, HEAD_DIM), lambda i, j: (i, 0, PA_COL_VA // HEAD_DIM + j)),
        ],
        out_specs=pl.BlockSpec((1, s, gw), lambda i, j: (i, 0, j)),
        out_shape=jax.ShapeDtypeStruct((b, s, A_Q_W), BF16),
        scratch_shapes=[pltpu.VMEM((A_KV_HEADS, len(offsets), HEADS_PER_STEP, Q_BLOCK, kw), F32)],
        compiler_params=_params("arbitrary", "arbitrary"),
        name="attn_a",
    )(jnp.asarray(bkt), table_flat, sink, pa, pa, pa)


def _attn_b_group(pb, table_flat, gi):
    window, dil = B_PATTERNS[gi]
    b, s, _ = pb.shape
    sub = s // dil
    half = window // (2 * dil)
    kw, _, offsets = _window_geometry(sub)
    bkt = _band_buckets(sub, half, dil)
    h0 = A_Q_HEADS + gi * B_HEADS_PER_GROUP
    nh = B_HEADS_PER_GROUP

    def head_specs(part):
        base = (part * B_W + gi * B_GROUP_W) // HEAD_DIM
        return [pl.BlockSpec((1, s, HEAD_DIM), lambda i, r, c=base + h: (i, 0, c)) for h in range(nh)]

    kernel = functools.partial(
        _attn_kernel, seq_len=sub, dilation=dil, n_q_refs=nh, n_kv_refs=nh, n_o_refs=nh,
        has_sink=False, has_lse=True, head0=lambda r: h0, head_sets=1, buckets_used=_buckets_used(bkt))
    res = pl.pallas_call(
        kernel,
        grid=(b, dil),
        in_specs=[pl.BlockSpec(bkt.shape, lambda i, r: (0, 0, 0)), _smem_spec()]
        + head_specs(0) + head_specs(1) + head_specs(2),
        out_specs=[pl.BlockSpec((1, s, HEAD_DIM), lambda i, r: (i, 0, 0))] * (nh + 1),
        out_shape=[jax.ShapeDtypeStruct((b, s, HEAD_DIM), F32)] * (nh + 1),
        scratch_shapes=[pltpu.VMEM((1, len(offsets), HEADS_PER_STEP, Q_BLOCK, kw), F32)],
        compiler_params=_params("arbitrary", "arbitrary"),
        name=f"attn_b{gi}",
    )(jnp.asarray(bkt), table_flat, *([pb] * (3 * nh)))
    return res[:nh], res[nh]


def _mix_kernel(*refs):
    nh, ng = B_HEADS_PER_GROUP, len(B_PATTERNS)
    it = iter(refs)
    ya_ref = next(it)
    o_refs = [[next(it) for _ in range(nh)] for _ in range(ng)]
    l_refs = [next(it) for _ in range(ng)]
    ga_ref, gb_ref, x_ref, wa_ref, wb_ref, wo_ref, g_ref, x1_ref, hf_ref = it
    heads = []
    for h in range(nh):
        ls = [r[:, h:h + 1] for r in l_refs]
        mx = functools.reduce(jnp.maximum, ls)
        es = [jnp.exp(l - mx) for l in ls]
        tot = functools.reduce(lambda a, c: a + c, es)
        y = functools.reduce(lambda a, c: a + c,
                             [(e / tot) * o_refs[g][h][...] for g, e in enumerate(es)])
        heads.append(y.astype(BF16))
    yb = jnp.concatenate(heads, axis=-1)
    ta = jnp.dot(ya_ref[...], wa_ref[...], preferred_element_type=F32)
    tb = jnp.dot(yb, wb_ref[...], preferred_element_type=F32)
    merged = (jax.nn.sigmoid(ga_ref[...].astype(F32)) * ta
              + jax.nn.sigmoid(gb_ref[...].astype(F32)) * tb)
    x1 = x_ref[...] + jnp.dot(merged.astype(BF16), wo_ref[...], preferred_element_type=F32)
    x1_ref[...] = x1
    hf_ref[...] = _rmsnorm(x1, g_ref[...]).astype(BF16)


def _resident(shape):
    return pl.BlockSpec(shape, lambda i: (0,) * len(shape), pipeline_mode=pl.Buffered(1))


def _mix(ya, outs, lses, pa2d, x2d, wa, wb, wo, gain, *, tm=256):
    m, d = x2d.shape
    assert PA_COL_GA % d == 0 and PA_COL_GB % d == 0

    def rows(width, col_block=0):
        return pl.BlockSpec((tm, width), lambda i: (i, col_block))

    flat_outs = [o for group in outs for o in group]
    return pl.pallas_call(
        _mix_kernel,
        grid=(m // tm,),
        in_specs=[rows(A_Q_W)]
        + [rows(HEAD_DIM)] * (len(flat_outs) + len(lses))
        + [rows(d, PA_COL_GA // d), rows(d, PA_COL_GB // d), rows(d),
           _resident(wa.shape), _resident(wb.shape), _resident(wo.shape), _resident(gain.shape)],
        out_specs=[rows(d), rows(d)],
        out_shape=[jax.ShapeDtypeStruct((m, d), F32), jax.ShapeDtypeStruct((m, d), BF16)],
        compiler_params=_params("parallel"),
        name="mix",
    )(ya, *flat_outs, *lses, pa2d, pa2d, x2d, wa, wb, wo, gain)


def _ffn_up_kernel(hf_ref, wg_ref, wu_ref, cw_ref, cb_ref, act_ref):
    hf = hf_ref[...]
    g = jnp.dot(hf, wg_ref[...], preferred_element_type=F32)
    u = jnp.dot(hf, wu_ref[...], preferred_element_type=F32)
    s = g.shape[0]
    row = lax.broadcasted_iota(jnp.int32, g.shape, 0)
    prev = jnp.where(row == 0, 0.0, pltpu.roll(g, 1, 0))
    nxt = jnp.where(row == s - 1, 0.0, pltpu.roll(g, s - 1, 0))
    cw = cw_ref[...]
    conv = prev * cw[0:1] + g * cw[1:2] + nxt * cw[2:3] + cb_ref[...]
    act_ref[...] = (jax.nn.gelu(conv) * u).astype(act_ref.dtype)


def _ffn_up(hf, wg, wu, cw, cb, *, seq, tf=512):
    m, d = hf.shape
    f = wg.shape[1]
    return pl.pallas_call(
        _ffn_up_kernel,
        grid=(m // seq, f // tf),
        in_specs=[
            pl.BlockSpec((seq, d), lambda i, j: (i, 0)),
            pl.BlockSpec((d, tf), lambda i, j: (0, j)),
            pl.BlockSpec((d, tf), lambda i, j: (0, j)),
            pl.BlockSpec((cw.shape[0], tf), lambda i, j: (0, j)),
            pl.BlockSpec((1, tf), lambda i, j: (0, j)),
        ],
        out_specs=pl.BlockSpec((seq, tf), lambda i, j: (i, j)),
        out_shape=jax.ShapeDtypeStruct((m, f), BF16),
        compiler_params=_params("parallel", "arbitrary"),
        name="ffn_up",
    )(hf, wg, wu, cw, cb)


def _ffn_down_kernel(act_ref, w_ref, x_ref, o_ref):
    o_ref[...] = x_ref[...] + jnp.dot(act_ref[...], w_ref[...], preferred_element_type=F32)


def _ffn_down(act, wd, x1, *, tm=512, tn=512):
    m, f = act.shape
    n = wd.shape[1]
    return pl.pallas_call(
        _ffn_down_kernel,
        grid=(m // tm, n // tn),
        in_specs=[
            pl.BlockSpec((tm, f), lambda i, j: (i, 0)),
            pl.BlockSpec((f, tn), lambda i, j: (0, j)),
            pl.BlockSpec((tm, tn), lambda i, j: (i, j)),
        ],
        out_specs=pl.BlockSpec((tm, tn), lambda i, j: (i, j)),
        out_shape=jax.ShapeDtypeStruct((m, n), F32),
        compiler_params=_params("parallel", "arbitrary"),
        name="ffn_down",
    )(act, wd, x1)


def _ple_kernel(x_ref, p_ref, gp_ref, wg_ref, wp_ref, gf_ref, o_ref, *, final):
    x = x_ref[...]
    hp = _rmsnorm(x, gp_ref[...]).astype(BF16)
    gate = jax.nn.sigmoid(jnp.dot(hp, wg_ref[...], preferred_element_type=F32))
    emb = jnp.dot(p_ref[...].astype(BF16), wp_ref[...], preferred_element_type=F32)
    y = x + gate * emb
    o_ref[...] = _rmsnorm(y, gf_ref[...]) if final else y


def _ple(x2, p2d, gain_p, wpg, wpp, gain_f, *, final, tm=256):
    m, d = x2.shape
    pd = p2d.shape[1]
    return pl.pallas_call(
        functools.partial(_ple_kernel, final=final),
        grid=(m // tm,),
        in_specs=[
            pl.BlockSpec((tm, d), lambda i: (i, 0)),
            pl.BlockSpec((tm, pd), lambda i: (i, 0)),
            _resident(gain_p.shape), _resident(wpg.shape), _resident(wpp.shape),
            _resident(gain_f.shape),
        ],
        out_specs=pl.BlockSpec((tm, d), lambda i: (i, 0)),
        out_shape=jax.ShapeDtypeStruct((m, d), F32),
        compiler_params=_params("parallel"),
        name="ple",
    )(x2, p2d, gain_p, wpg, wpp, gain_f)


def kernel(x, p, rel_bias_table, attn_norm, w_in, sink_a, w_branch_a, w_branch_b, w_out,
           ffn_norm, w_ffn_gate, w_ffn_up, conv_w, conv_b, w_ffn_down,
           ple_norm, w_ple_gate, w_ple_proj, final_norm):
    b, s, d = x.shape
    depth = w_in.shape[0]
    assert d == D_MODEL and w_in.shape[2] == IN_PROJ_W and s % (Q_BLOCK * B_PATTERNS[-1][1]) == 0
    assert rel_bias_table.shape == (N_BUCKETS, N_BIAS_HEADS)
    m = b * s
    x2d = x.reshape(m, d)
    table_flat = rel_bias_table.reshape(-1)
    for i in range(depth):
        pb2d, pa2d = _inproj(x2d, attn_norm[i][None], w_in[i].astype(BF16))
        ya = _attn_a(pa2d.reshape(b, s, PA_W), table_flat, sink_a[i]).reshape(m, A_Q_W)
        pb = pb2d.reshape(b, s, PB_W)
        groups = [_attn_b_group(pb, table_flat, gi) for gi in range(len(B_PATTERNS))]
        outs = [[o.reshape(m, HEAD_DIM) for o in os] for os, _ in groups]
        lses = [l.reshape(m, HEAD_DIM) for _, l in groups]
        x1, hf = _mix(ya, outs, lses, pa2d, x2d, w_branch_a[i].astype(BF16),
                      w_branch_b[i].astype(BF16), w_out[i].astype(BF16), ffn_norm[i][None])
        act = _ffn_up(hf, w_ffn_gate[i].astype(BF16), w_ffn_up[i].astype(BF16), conv_w[i],
                      conv_b[i][None], seq=s)
        x2 = _ffn_down(act, w_ffn_down[i].astype(BF16), x1)
        x2d = _ple(x2, p[i].reshape(m, -1), ple_norm[i][None], w_ple_gate[i].astype(BF16),
                   w_ple_proj[i].astype(BF16), final_norm[None], final=i == depth - 1)
    return x2d.reshape(b, s, d)
```

```python
import functools
import math

import jax
import jax.numpy as jnp
import numpy as np
from jax import lax
from jax.experimental import pallas as pl
from jax.experimental.pallas import tpu as pltpu

D_MODEL = 2048
HEAD_DIM = 128
A_Q_HEADS = 8
A_KV_HEADS = 2
A_GROUP = A_Q_HEADS // A_KV_HEADS
A_HALF_WINDOW = 128
B_PATTERNS = ((128, 1), (512, 4), (2048, 16))
B_HEADS_PER_GROUP = 4
N_BUCKETS = 32
MAX_DISTANCE = 1024
N_BIAS_HEADS = A_Q_HEADS + len(B_PATTERNS) * B_HEADS_PER_GROUP
A_Q_W = A_Q_HEADS * HEAD_DIM
A_KV_W = A_KV_HEADS * HEAD_DIM
B_GROUP_W = B_HEADS_PER_GROUP * HEAD_DIM
B_W = len(B_PATTERNS) * B_GROUP_W
IN_PROJ_W = A_Q_W + 2 * A_KV_W + 3 * B_W + 2 * D_MODEL
RMS_EPS = 1e-6
NEG_INF = -1e30

W_COL_B = A_Q_W + 2 * A_KV_W
W_COL_G = W_COL_B + 3 * B_W
PA_W = 2 * D_MODEL + W_COL_B
PA_COL_GA = 0
PA_COL_GB = D_MODEL
PA_COL_QA = 2 * D_MODEL
PA_COL_KA = PA_COL_QA + A_Q_W
PA_COL_VA = PA_COL_KA + A_KV_W
PB_W = 3 * B_W

Q_BLOCK = 128
HEADS_PER_STEP = 4
ATTN_UNROLL = 2

VMEM_LIMIT_BYTES = 56 * 1024 * 1024

BF16 = jnp.bfloat16
F32 = jnp.float32


def _params(*semantics):
    return pltpu.CompilerParams(dimension_semantics=semantics, vmem_limit_bytes=VMEM_LIMIT_BYTES)


def _rmsnorm(x, g):
    y = x * lax.rsqrt(jnp.mean(x * x, axis=-1, keepdims=True) + RMS_EPS)
    return y * g


def _inproj_kernel(x_ref, g_ref, w_ref, pb_ref, pa_ref, h_ref, *, row_chunk, n_b_tiles):
    j = pl.program_id(1)

    @pl.when(j == 0)
    def _():
        def body(c, _):
            rows = pl.ds(pl.multiple_of(c * row_chunk, row_chunk), row_chunk)
            h_ref[rows, :] = _rmsnorm(x_ref[rows, :], g_ref[...]).astype(BF16)
            return 0

        lax.fori_loop(0, x_ref.shape[0] // row_chunk, body, 0)

    @pl.when(j < n_b_tiles)
    def _():
        pb_ref[...] = jnp.dot(h_ref[...], w_ref[...], preferred_element_type=F32)

    @pl.when(j >= n_b_tiles)
    def _():
        pa_ref[...] = jnp.dot(h_ref[...], w_ref[...], preferred_element_type=F32).astype(BF16)


def _inproj(x2d, gain, w, *, tm=1024, tn=512):
    m, k = x2d.shape
    nb_t = PB_W // tn
    head_t = W_COL_B // tn
    n_steps = w.shape[1] // tn

    def w_col(j):
        return jnp.where(j < n_steps - head_t, j + head_t, j - (n_steps - head_t))

    return pl.pallas_call(
        functools.partial(_inproj_kernel, row_chunk=128, n_b_tiles=nb_t),
        grid=(m // tm, n_steps),
        in_specs=[
            pl.BlockSpec((tm, k), lambda i, j: (i, 0)),
            pl.BlockSpec((1, k), lambda i, j: (0, 0)),
            pl.BlockSpec((k, tn), lambda i, j: (0, w_col(j))),
        ],
        out_specs=[
            pl.BlockSpec((tm, tn), lambda i, j: (i, jnp.minimum(j, nb_t - 1))),
            pl.BlockSpec((tm, tn), lambda i, j: (i, jnp.maximum(j - nb_t, 0))),
        ],
        out_shape=[jax.ShapeDtypeStruct((m, PB_W), F32), jax.ShapeDtypeStruct((m, PA_W), BF16)],
        scratch_shapes=[pltpu.VMEM((tm, k), BF16)],
        compiler_params=_params("arbitrary", "arbitrary"),
        name="inproj",
    )(x2d, gain, w)


def _t5_bucket_static(rel):
    half = N_BUCKETS // 2
    max_exact = half // 2
    n = np.abs(rel)
    side = np.where(rel > 0, half, 0)
    nf = np.maximum(n, 1).astype(np.float32)
    large = max_exact + (np.log(nf / max_exact) / math.log(MAX_DISTANCE / max_exact)
                         * (half - max_exact)).astype(np.int32)
    large = np.minimum(large, half - 1)
    return side + np.where(n < max_exact, n, large)


def _window_geometry(seq_len, half_w):
    nb = seq_len // Q_BLOCK
    if nb == 1:
        return seq_len, nb, (0,)
    kw = Q_BLOCK + 2 * half_w
    assert half_w <= Q_BLOCK and kw <= seq_len
    return kw, nb, (0, -half_w, Q_BLOCK - kw)


def _band_buckets(seq_len, half_w, dilation):
    kw, _, offsets = _window_geometry(seq_len, half_w)
    qi = np.arange(Q_BLOCK)[:, None]
    kj = np.arange(kw)[None, :]
    rel = np.stack([off + kj - qi for off in offsets])
    buckets = _t5_bucket_static(rel * dilation)
    return np.where(np.abs(rel) <= half_w, buckets, -1).astype(np.int32)


def _attn_kernel(*refs, seq_len, half_w, dilation, n_q_refs, n_kv_refs, n_o_refs, has_sink,
                 has_lse, head0, head_sets, buckets_used, unroll, res_per_step=1):
    it = iter(refs)
    bkt_ref, table_ref = next(it), next(it)
    sink_ref = next(it) if has_sink else None
    q_refs = [next(it) for _ in range(n_q_refs)]
    k_refs = [next(it) for _ in range(n_kv_refs)]
    v_refs = [next(it) for _ in range(n_kv_refs)]
    o_refs = [next(it) for _ in range(n_o_refs)]
    lse_ref = next(it) if has_lse else None
    bias_ref = next(it)

    kw, nb, offsets = _window_geometry(seq_len, half_w)
    scale = HEAD_DIM ** -0.5
    step = pl.program_id(1)
    first_head = head0(step)
    head_set = step if head_sets > 1 else 0
    chains = ([list(range(HEADS_PER_STEP))] if n_kv_refs == 1
              else [[h] for h in range(HEADS_PER_STEP)])

    @pl.when((pl.program_id(0) == 0) & (step < head_sets))
    def _():
        for var in range(len(offsets)):
            bkt = bkt_ref[var]
            for h in range(HEADS_PER_STEP):
                acc = jnp.full(bkt.shape, NEG_INF, F32)
                for bucket in buckets_used:
                    acc = jnp.where(bkt == bucket,
                                    table_ref[bucket * N_BIAS_HEADS + first_head + h], acc)
                bias_ref[head_set, var, h * Q_BLOCK:(h + 1) * Q_BLOCK, :] = acc

    def rows(residue, start, size):
        if dilation == 1:
            return pl.ds(start, size)
        return pl.ds(residue + start * dilation, size, stride=dilation)

    def head_view(ref_list, h):
        if len(ref_list) > 1:
            return ref_list[h], slice(None)
        if ref_list[0].shape[-1] == HEAD_DIM:
            return ref_list[0], slice(None)
        return ref_list[0], slice(h * HEAD_DIM, (h + 1) * HEAD_DIM)

    def load(ref_list, h, residue, start, size):
        ref, cols = head_view(ref_list, h)
        return ref[0, rows(residue, start, size), cols].astype(BF16)

    lane = lax.broadcasted_iota(jnp.int32, (Q_BLOCK, HEAD_DIM), 1)
    ones = jnp.ones((kw, HEAD_DIM), BF16)

    def block(i, _, residue):
        if isinstance(i, int):
            qs = i * Q_BLOCK
            ks = min(max(qs - half_w, 0), seq_len - kw)
            var = 0 if i == 0 else (2 if i == nb - 1 else 1)
        else:
            qs = pl.multiple_of(i * Q_BLOCK, Q_BLOCK)
            ks = pl.multiple_of(jnp.clip(qs - half_w, 0, seq_len - kw), half_w)
            var = jnp.where(i == 0, 0, jnp.where(i == nb - 1, 2, 1))
        lse_tile = jnp.zeros((Q_BLOCK, HEAD_DIM), F32)
        for heads in chains:
            q = jnp.concatenate([load(q_refs, h, residue, qs, Q_BLOCK) for h in heads], axis=0)
            k = load(k_refs, heads[0], residue, ks, kw)
            v_ext = jnp.concatenate([load(v_refs, heads[0], residue, ks, kw), ones], axis=1)
            bias = bias_ref[head_set, var, heads[0] * Q_BLOCK:(heads[-1] + 1) * Q_BLOCK, :]
            s = lax.dot_general(q, k, (((1,), (1,)), ((), ())), preferred_element_type=F32)
            s = s * scale + bias
            m = jnp.max(s, axis=-1, keepdims=True)
            if has_sink:
                sk = jnp.concatenate(
                    [jnp.full((Q_BLOCK, 1), sink_ref[first_head + h], F32) for h in heads], axis=0)
                m = jnp.maximum(m, sk)
            p = jnp.exp(s - m).astype(BF16)
            ov = jnp.dot(p, v_ext, preferred_element_type=F32)
            denom = ov[:, HEAD_DIM:]
            if has_sink:
                denom = denom + jnp.exp(sk - m)
            o = ov[:, :HEAD_DIM] / denom
            lse = m + jnp.log(denom) if has_lse else None
            for n, h in enumerate(heads):
                o_ref, o_cols = head_view(o_refs, h)
                o_ref[0, rows(residue, qs, Q_BLOCK), o_cols] = (
                    o[n * Q_BLOCK:(n + 1) * Q_BLOCK].astype(o_ref.dtype))
                if has_lse:
                    lse_tile = jnp.where(lane == h, lse[n * Q_BLOCK:(n + 1) * Q_BLOCK], lse_tile)
        if has_lse:
            lse_ref[0, rows(residue, qs, Q_BLOCK), :] = lse_tile
        return 0

    if res_per_step == dilation and dilation > 1:
        for r in range(dilation):
            for i in range(nb):
                block(i, 0, r)
    else:
        assert res_per_step == 1
        residue = step if dilation > 1 else 0
        lax.fori_loop(0, nb, functools.partial(block, residue=residue), 0, unroll=min(unroll, nb))


def _buckets_used(bkt):
    return tuple(int(v) for v in np.unique(bkt) if v >= 0)


def _smem_spec():
    return pl.BlockSpec(memory_space=pltpu.SMEM)


def _attn_a(pa, table_flat, sink):
    b, s, _ = pa.shape
    gw = A_GROUP * HEAD_DIM
    kw, _, offsets = _window_geometry(s, A_HALF_WINDOW)
    bkt = _band_buckets(s, A_HALF_WINDOW, 1)
    kernel = functools.partial(
        _attn_kernel, seq_len=s, half_w=A_HALF_WINDOW, dilation=1, unroll=ATTN_UNROLL,
        n_q_refs=1, n_kv_refs=1, n_o_refs=1, has_sink=True, has_lse=False, head0=lambda kv: kv * A_GROUP, head_sets=A_KV_HEADS,
        buckets_used=_buckets_used(bkt))
    return pl.pallas_call(
        kernel,
        grid=(b, A_KV_HEADS),
        in_specs=[
            pl.BlockSpec(bkt.shape, lambda i, j: (0, 0, 0)),
            _smem_spec(),
            _smem_spec(),
            pl.BlockSpec((1, s, gw), lambda i, j: (i, 0, PA_COL_QA // gw + j)),
            pl.BlockSpec((1, s, HEAD_DIM), lambda i, j: (i, 0, PA_COL_KA // HEAD_DIM + j)),
            pl.BlockSpec((1, s, HEAD_DIM), lambda i, j: (i, 0, PA_COL_VA // HEAD_DIM + j)),
        ],
        out_specs=pl.BlockSpec((1, s, gw), lambda i, j: (i, 0, j)),
        out_shape=jax.ShapeDtypeStruct((b, s, A_Q_W), BF16),
        scratch_shapes=[pltpu.VMEM((A_KV_HEADS, len(offsets), HEADS_PER_STEP * Q_BLOCK, kw), F32)],
        compiler_params=_params("arbitrary", "arbitrary"),
        name="attn_a",
    )(jnp.asarray(bkt), table_flat, sink, pa, pa, pa)


def _attn_b_group(pb, table_flat, gi):
    window, dil = B_PATTERNS[gi]
    b, s, _ = pb.shape
    sub = s // dil
    half = window // (2 * dil)
    kw, _, offsets = _window_geometry(sub, half)
    rps = dil
    bkt = _band_buckets(sub, half, dil)
    h0 = A_Q_HEADS + gi * B_HEADS_PER_GROUP
    nh = B_HEADS_PER_GROUP

    def head_specs(part):
        base = (part * B_W + gi * B_GROUP_W) // HEAD_DIM
        return [pl.BlockSpec((1, s, HEAD_DIM), lambda i, r, c=base + h: (i, 0, c)) for h in range(nh)]

    kernel = functools.partial(
        _attn_kernel, seq_len=sub, half_w=half, dilation=dil, unroll=ATTN_UNROLL,
        res_per_step=rps, n_q_refs=nh, n_kv_refs=nh, n_o_refs=nh,
        has_sink=False, has_lse=True, head0=lambda r: h0, head_sets=1, buckets_used=_buckets_used(bkt))
    res = pl.pallas_call(
        kernel,
        grid=(b, dil // rps),
        in_specs=[pl.BlockSpec(bkt.shape, lambda i, r: (0, 0, 0)), _smem_spec()]
        + head_specs(0) + head_specs(1) + head_specs(2),
        out_specs=[pl.BlockSpec((1, s, HEAD_DIM), lambda i, r: (i, 0, 0))] * (nh + 1),
        out_shape=[jax.ShapeDtypeStruct((b, s, HEAD_DIM), F32)] * (nh + 1),
        scratch_shapes=[pltpu.VMEM((1, len(offsets), HEADS_PER_STEP * Q_BLOCK, kw), F32)],
        compiler_params=_params("arbitrary", "arbitrary"),
        name=f"attn_b{gi}",
    )(jnp.asarray(bkt), table_flat, *([pb] * (3 * nh)))
    return res[:nh], res[nh]


def _mix_kernel(*refs):
    nh, ng = B_HEADS_PER_GROUP, len(B_PATTERNS)
    it = iter(refs)
    ya_ref = next(it)
    o_refs = [[next(it) for _ in range(nh)] for _ in range(ng)]
    l_refs = [next(it) for _ in range(ng)]
    ga_ref, gb_ref, x_ref, wa_ref, wb_ref, wo_ref, g_ref, x1_ref, hf_ref = it
    heads = []
    for h in range(nh):
        ls = [r[:, h:h + 1] for r in l_refs]
        mx = functools.reduce(jnp.maximum, ls)
        es = [jnp.exp(l - mx) for l in ls]
        tot = functools.reduce(lambda a, c: a + c, es)
        y = functools.reduce(lambda a, c: a + c,
                             [(e / tot) * o_refs[g][h][...] for g, e in enumerate(es)])
        heads.append(y.astype(BF16))
    yb = jnp.concatenate(heads, axis=-1)
    ta = jnp.dot(ya_ref[...], wa_ref[...], preferred_element_type=F32)
    tb = jnp.dot(yb, wb_ref[...], preferred_element_type=F32)
    merged = (jax.nn.sigmoid(ga_ref[...].astype(F32)) * ta
              + jax.nn.sigmoid(gb_ref[...].astype(F32)) * tb)
    x1 = x_ref[...] + jnp.dot(merged.astype(BF16), wo_ref[...], preferred_element_type=F32)
    x1_ref[...] = x1
    hf_ref[...] = _rmsnorm(x1, g_ref[...]).astype(BF16)


def _resident(shape):
    return pl.BlockSpec(shape, lambda i: (0,) * len(shape), pipeline_mode=pl.Buffered(1))


def _mix(ya, outs, lses, pa2d, x2d, wa, wb, wo, gain, *, tm=256):
    m, d = x2d.shape
    assert PA_COL_GA % d == 0 and PA_COL_GB % d == 0

    def rows(width, col_block=0):
        return pl.BlockSpec((tm, width), lambda i: (i, col_block))

    flat_outs = [o for group in outs for o in group]
    return pl.pallas_call(
        _mix_kernel,
        grid=(m // tm,),
        in_specs=[rows(A_Q_W)]
        + [rows(HEAD_DIM)] * (len(flat_outs) + len(lses))
        + [rows(d, PA_COL_GA // d), rows(d, PA_COL_GB // d), rows(d),
           _resident(wa.shape), _resident(wb.shape), _resident(wo.shape), _resident(gain.shape)],
        out_specs=[rows(d), rows(d)],
        out_shape=[jax.ShapeDtypeStruct((m, d), F32), jax.ShapeDtypeStruct((m, d), BF16)],
        compiler_params=_params("parallel"),
        name="mix",
    )(ya, *flat_outs, *lses, pa2d, pa2d, x2d, wa, wb, wo, gain)


def _ffn_up_kernel(hf_ref, wg_ref, wu_ref, cw_ref, cb_ref, act_ref):
    hf = hf_ref[...]
    g = jnp.dot(hf, wg_ref[...], preferred_element_type=F32)
    u = jnp.dot(hf, wu_ref[...], preferred_element_type=F32)
    s = g.shape[0]
    row = lax.broadcasted_iota(jnp.int32, g.shape, 0)
    prev = jnp.where(row == 0, 0.0, pltpu.roll(g, 1, 0))
    nxt = jnp.where(row == s - 1, 0.0, pltpu.roll(g, s - 1, 0))
    cw = cw_ref[...]
    conv = prev * cw[0:1] + g * cw[1:2] + nxt * cw[2:3] + cb_ref[...]
    act_ref[...] = (jax.nn.gelu(conv) * u).astype(act_ref.dtype)


def _ffn_up(hf, wg, wu, cw, cb, *, seq, tf=512):
    m, d = hf.shape
    f = wg.shape[1]
    return pl.pallas_call(
        _ffn_up_kernel,
        grid=(m // seq, f // tf),
        in_specs=[
            pl.BlockSpec((seq, d), lambda i, j: (i, 0)),
            pl.BlockSpec((d, tf), lambda i, j: (0, j)),
            pl.BlockSpec((d, tf), lambda i, j: (0, j)),
            pl.BlockSpec((cw.shape[0], tf), lambda i, j: (0, j)),
            pl.BlockSpec((1, tf), lambda i, j: (0, j)),
        ],
        out_specs=pl.BlockSpec((seq, tf), lambda i, j: (i, j)),
        out_shape=jax.ShapeDtypeStruct((m, f), BF16),
        compiler_params=_params("parallel", "arbitrary"),
        name="ffn_up",
    )(hf, wg, wu, cw, cb)


def _ffn_down_kernel(act_ref, w_ref, x_ref, o_ref):
    o_ref[...] = x_ref[...] + jnp.dot(act_ref[...], w_ref[...], preferred_element_type=F32)


def _ffn_down(act, wd, x1, *, tm=512, tn=512):
    m, f = act.shape
    n = wd.shape[1]
    return pl.pallas_call(
        _ffn_down_kernel,
        grid=(m // tm, n // tn),
        in_specs=[
            pl.BlockSpec((tm, f), lambda i, j: (i, 0)),
            pl.BlockSpec((f, tn), lambda i, j: (0, j)),
            pl.BlockSpec((tm, tn), lambda i, j: (i, j)),
        ],
        out_specs=pl.BlockSpec((tm, tn), lambda i, j: (i, j)),
        out_shape=jax.ShapeDtypeStruct((m, n), F32),
        compiler_params=_params("parallel", "arbitrary"),
        name="ffn_down",
    )(act, wd, x1)


def _ple_kernel(x_ref, p_ref, gp_ref, wg_ref, wp_ref, gf_ref, o_ref, *, final):
    x = x_ref[...]
    hp = _rmsnorm(x, gp_ref[...]).astype(BF16)
    gate = jax.nn.sigmoid(jnp.dot(hp, wg_ref[...], preferred_element_type=F32))
    emb = jnp.dot(p_ref[...].astype(BF16), wp_ref[...], preferred_element_type=F32)
    y = x + gate * emb
    o_ref[...] = _rmsnorm(y, gf_ref[...]) if final else y


def _ple(x2, p2d, gain_p, wpg, wpp, gain_f, *, final, tm=256):
    m, d = x2.shape
    pd = p2d.shape[1]
    return pl.pallas_call(
        functools.partial(_ple_kernel, final=final),
        grid=(m // tm,),
        in_specs=[
            pl.BlockSpec((tm, d), lambda i: (i, 0)),
            pl.BlockSpec((tm, pd), lambda i: (i, 0)),
            _resident(gain_p.shape), _resident(wpg.shape), _resident(wpp.shape),
            _resident(gain_f.shape),
        ],
        out_specs=pl.BlockSpec((tm, d), lambda i: (i, 0)),
        out_shape=jax.ShapeDtypeStruct((m, d), F32),
        compiler_params=_params("parallel"),
        name="ple",
    )(x2, p2d, gain_p, wpg, wpp, gain_f)


def kernel(x, p, rel_bias_table, attn_norm, w_in, sink_a, w_branch_a, w_branch_b, w_out,
           ffn_norm, w_ffn_gate, w_ffn_up, conv_w, conv_b, w_ffn_down,
           ple_norm, w_ple_gate, w_ple_proj, final_norm):
    b, s, d = x.shape
    depth = w_in.shape[0]
    assert d == D_MODEL and w_in.shape[2] == IN_PROJ_W and s % (Q_BLOCK * B_PATTERNS[-1][1]) == 0
    assert rel_bias_table.shape == (N_BUCKETS, N_BIAS_HEADS)
    m = b * s
    x2d = x.reshape(m, d)
    table_flat = rel_bias_table.reshape(-1)
    for i in range(depth):
        pb2d, pa2d = _inproj(x2d, attn_norm[i][None], w_in[i].astype(BF16))
        ya = _attn_a(pa2d.reshape(b, s, PA_W), table_flat, sink_a[i]).reshape(m, A_Q_W)
        pb = pb2d.reshape(b, s, PB_W)
        groups = [_attn_b_group(pb, table_flat, gi) for gi in range(len(B_PATTERNS))]
        outs = [[o.reshape(m, HEAD_DIM) for o in os] for os, _ in groups]
        lses = [l.reshape(m, HEAD_DIM) for _, l in groups]
        x1, hf = _mix(ya, outs, lses, pa2d, x2d, w_branch_a[i].astype(BF16),
                      w_branch_b[i].astype(BF16), w_out[i].astype(BF16), ffn_norm[i][None])
        act = _ffn_up(hf, w_ffn_gate[i].astype(BF16), w_ffn_up[i].astype(BF16), conv_w[i],
                      conv_b[i][None], seq=s)
        x2 = _ffn_down(act, w_ffn_down[i].astype(BF16), x1)
        x2d = _ple(x2, p[i].reshape(m, -1), ple_norm[i][None], w_ple_gate[i].astype(BF16),
                   w_ple_proj[i].astype(BF16), final_norm[None], final=i == depth - 1)
    return x2d.reshape(b, s, d)
```

```python
import functools
import math

import jax
import jax.numpy as jnp
import numpy as np
from jax import lax
from jax.experimental import pallas as pl
from jax.experimental.pallas import tpu as pltpu

D_MODEL = 2048
HEAD_DIM = 128
A_Q_HEADS = 8
A_KV_HEADS = 2
A_GROUP = A_Q_HEADS // A_KV_HEADS
A_HALF_WINDOW = 128
B_PATTERNS = ((128, 1), (512, 4), (2048, 16))
B_HEADS_PER_GROUP = 4
N_BUCKETS = 32
MAX_DISTANCE = 1024
N_BIAS_HEADS = A_Q_HEADS + len(B_PATTERNS) * B_HEADS_PER_GROUP
A_Q_W = A_Q_HEADS * HEAD_DIM
A_KV_W = A_KV_HEADS * HEAD_DIM
B_GROUP_W = B_HEADS_PER_GROUP * HEAD_DIM
B_W = len(B_PATTERNS) * B_GROUP_W
IN_PROJ_W = A_Q_W + 2 * A_KV_W + 3 * B_W + 2 * D_MODEL
RMS_EPS = 1e-6
NEG_INF = -1e30

PB_W = A_Q_W + 2 * A_KV_W + 3 * B_W
PB_COL_KA = A_Q_W
PB_COL_VA = PB_COL_KA + A_KV_W
PB_COL_QB = PB_COL_VA + A_KV_W
PA_W = 2 * D_MODEL
PA_COL_GA = 0
PA_COL_GB = D_MODEL

Q_BLOCK = 128
HEADS_PER_STEP = 4
ATTN_UNROLL = 2

VMEM_LIMIT_BYTES = 56 * 1024 * 1024

BF16 = jnp.bfloat16
F32 = jnp.float32


def _params(*semantics):
    return pltpu.CompilerParams(dimension_semantics=semantics, vmem_limit_bytes=VMEM_LIMIT_BYTES)


def _rmsnorm(x, g):
    y = x * lax.rsqrt(jnp.mean(x * x, axis=-1, keepdims=True) + RMS_EPS)
    return y * g


def _inproj_kernel(x_ref, g_ref, w_ref, pb_ref, pa_ref, h_ref, *, row_chunk, n_b_tiles):
    j = pl.program_id(1)

    @pl.when(j == 0)
    def _():
        def body(c, _):
            rows = pl.ds(pl.multiple_of(c * row_chunk, row_chunk), row_chunk)
            h_ref[rows, :] = _rmsnorm(x_ref[rows, :], g_ref[...]).astype(BF16)
            return 0

        lax.fori_loop(0, x_ref.shape[0] // row_chunk, body, 0)

    @pl.when(j < n_b_tiles)
    def _():
        pb_ref[...] = jnp.dot(h_ref[...], w_ref[...], preferred_element_type=F32)

    @pl.when(j >= n_b_tiles)
    def _():
        pa_ref[...] = jnp.dot(h_ref[...], w_ref[...], preferred_element_type=F32).astype(BF16)


def _inproj(x2d, gain, w, *, tm=1024, tn=1024):
    m, k = x2d.shape
    assert PB_W % tn == 0 and PA_W % tn == 0 and w.shape[1] == PB_W + PA_W
    nb_t = PB_W // tn
    n_steps = w.shape[1] // tn

    return pl.pallas_call(
        functools.partial(_inproj_kernel, row_chunk=128, n_b_tiles=nb_t),
        grid=(m // tm, n_steps),
        in_specs=[
            pl.BlockSpec((tm, k), lambda i, j: (i, 0)),
            pl.BlockSpec((1, k), lambda i, j: (0, 0)),
            pl.BlockSpec((k, tn), lambda i, j: (0, j)),
        ],
        out_specs=[
            pl.BlockSpec((tm, tn), lambda i, j: (i, jnp.minimum(j, nb_t - 1))),
            pl.BlockSpec((tm, tn), lambda i, j: (i, jnp.maximum(j - nb_t, 0))),
        ],
        out_shape=[jax.ShapeDtypeStruct((m, PB_W), F32), jax.ShapeDtypeStruct((m, PA_W), BF16)],
        scratch_shapes=[pltpu.VMEM((tm, k), BF16)],
        compiler_params=_params("arbitrary", "arbitrary"),
        name="inproj",
    )(x2d, gain, w)


def _t5_bucket_static(rel):
    half = N_BUCKETS // 2
    max_exact = half // 2
    n = np.abs(rel)
    side = np.where(rel > 0, half, 0)
    nf = np.maximum(n, 1).astype(np.float32)
    large = max_exact + (np.log(nf / max_exact) / math.log(MAX_DISTANCE / max_exact)
                         * (half - max_exact)).astype(np.int32)
    large = np.minimum(large, half - 1)
    return side + np.where(n < max_exact, n, large)


def _window_geometry(seq_len, half_w):
    nb = seq_len // Q_BLOCK
    if nb == 1:
        return seq_len, nb, (0,)
    kw = Q_BLOCK + 2 * half_w
    assert half_w <= Q_BLOCK and kw <= seq_len
    return kw, nb, (0, -half_w, Q_BLOCK - kw)


def _band_buckets(seq_len, half_w, dilation):
    kw, _, offsets = _window_geometry(seq_len, half_w)
    qi = np.arange(Q_BLOCK)[:, None]
    kj = np.arange(kw)[None, :]
    rel = np.stack([off + kj - qi for off in offsets])
    buckets = _t5_bucket_static(rel * dilation)
    return np.where(np.abs(rel) <= half_w, buckets, -1).astype(np.int32)


def _attn_kernel(*refs, seq_len, half_w, dilation, n_q_refs, n_kv_refs, n_o_refs, has_sink,
                 has_lse, head0, head_sets, buckets_used, unroll, res_per_step=1):
    it = iter(refs)
    bkt_ref, table_ref = next(it), next(it)
    sink_ref = next(it) if has_sink else None
    q_refs = [next(it) for _ in range(n_q_refs)]
    k_refs = [next(it) for _ in range(n_kv_refs)]
    v_refs = [next(it) for _ in range(n_kv_refs)]
    o_refs = [next(it) for _ in range(n_o_refs)]
    lse_ref = next(it) if has_lse else None
    bias_ref = next(it)

    kw, nb, offsets = _window_geometry(seq_len, half_w)
    scale = HEAD_DIM ** -0.5
    step = pl.program_id(1)
    first_head = head0(step)
    head_set = step if head_sets > 1 else 0
    chains = ([list(range(HEADS_PER_STEP))] if n_kv_refs == 1
              else [[h] for h in range(HEADS_PER_STEP)])

    @pl.when((pl.program_id(0) == 0) & (step < head_sets))
    def _():
        for var in range(len(offsets)):
            bkt = bkt_ref[var]
            for h in range(HEADS_PER_STEP):
                acc = jnp.full(bkt.shape, NEG_INF, F32)
                for bucket in buckets_used:
                    acc = jnp.where(bkt == bucket,
                                    table_ref[bucket * N_BIAS_HEADS + first_head + h], acc)
                bias_ref[head_set, var, h * Q_BLOCK:(h + 1) * Q_BLOCK, :] = acc

    def rows(residue, start, size):
        if dilation == 1:
            return pl.ds(start, size)
        return pl.ds(residue + start * dilation, size, stride=dilation)

    def head_view(ref_list, h):
        if len(ref_list) > 1:
            return ref_list[h], slice(None)
        if ref_list[0].shape[-1] == HEAD_DIM:
            return ref_list[0], slice(None)
        return ref_list[0], slice(h * HEAD_DIM, (h + 1) * HEAD_DIM)

    def load(ref_list, h, residue, start, size):
        ref, cols = head_view(ref_list, h)
        return ref[0, rows(residue, start, size), cols].astype(BF16)

    lane = lax.broadcasted_iota(jnp.int32, (Q_BLOCK, HEAD_DIM), 1)
    ones = jnp.ones((kw, HEAD_DIM), BF16)

    def block(i, _, residue):
        if isinstance(i, int):
            qs = i * Q_BLOCK
            ks = min(max(qs - half_w, 0), seq_len - kw)
            var = 0 if i == 0 else (2 if i == nb - 1 else 1)
        else:
            qs = pl.multiple_of(i * Q_BLOCK, Q_BLOCK)
            ks = pl.multiple_of(jnp.clip(qs - half_w, 0, seq_len - kw), half_w)
            var = jnp.where(i == 0, 0, jnp.where(i == nb - 1, 2, 1))
        lse_tile = jnp.zeros((Q_BLOCK, HEAD_DIM), F32)
        for heads in chains:
            q = jnp.concatenate([load(q_refs, h, residue, qs, Q_BLOCK) for h in heads], axis=0)
            k = load(k_refs, heads[0], residue, ks, kw)
            v_ext = jnp.concatenate([load(v_refs, heads[0], residue, ks, kw), ones], axis=1)
            bias = bias_ref[head_set, var, heads[0] * Q_BLOCK:(heads[-1] + 1) * Q_BLOCK, :]
            s = lax.dot_general(q, k, (((1,), (1,)), ((), ())), preferred_element_type=F32)
            s = s * scale + bias
            m = jnp.max(s, axis=-1, keepdims=True)
            if has_sink:
                sk = jnp.concatenate(
                    [jnp.full((Q_BLOCK, 1), sink_ref[first_head + h], F32) for h in heads], axis=0)
                m = jnp.maximum(m, sk)
            p = jnp.exp(s - m).astype(BF16)
            ov = jnp.dot(p, v_ext, preferred_element_type=F32)
            denom = ov[:, HEAD_DIM:]
            if has_sink:
                denom = denom + jnp.exp(sk - m)
            o = ov[:, :HEAD_DIM] / denom
            lse = m + jnp.log(denom) if has_lse else None
            for n, h in enumerate(heads):
                o_ref, o_cols = head_view(o_refs, h)
                o_ref[0, rows(residue, qs, Q_BLOCK), o_cols] = (
                    o[n * Q_BLOCK:(n + 1) * Q_BLOCK].astype(o_ref.dtype))
                if has_lse:
                    lse_tile = jnp.where(lane == h, lse[n * Q_BLOCK:(n + 1) * Q_BLOCK], lse_tile)
        if has_lse:
            lse_ref[0, rows(residue, qs, Q_BLOCK), :] = lse_tile
        return 0

    if res_per_step == dilation and dilation > 1:
        for r in range(dilation):
            for i in range(nb):
                block(i, 0, r)
    else:
        assert res_per_step == 1
        residue = step if dilation > 1 else 0
        lax.fori_loop(0, nb, functools.partial(block, residue=residue), 0, unroll=min(unroll, nb))


def _buckets_used(bkt):
    return tuple(int(v) for v in np.unique(bkt) if v >= 0)


def _smem_spec():
    return pl.BlockSpec(memory_space=pltpu.SMEM)


def _attn_a(pb, table_flat, sink):
    b, s, _ = pb.shape
    gw = A_GROUP * HEAD_DIM
    kw, _, offsets = _window_geometry(s, A_HALF_WINDOW)
    bkt = _band_buckets(s, A_HALF_WINDOW, 1)
    kernel = functools.partial(
        _attn_kernel, seq_len=s, half_w=A_HALF_WINDOW, dilation=1, unroll=ATTN_UNROLL,
        n_q_refs=1, n_kv_refs=1, n_o_refs=1, has_sink=True, has_lse=False,
        head0=lambda kv: kv * A_GROUP, head_sets=A_KV_HEADS, buckets_used=_buckets_used(bkt))
    return pl.pallas_call(
        kernel,
        grid=(b, A_KV_HEADS),
        in_specs=[
            pl.BlockSpec(bkt.shape, lambda i, j: (0, 0, 0)),
            _smem_spec(),
            _smem_spec(),
            pl.BlockSpec((1, s, gw), lambda i, j: (i, 0, j)),
            pl.BlockSpec((1, s, HEAD_DIM), lambda i, j: (i, 0, PB_COL_KA // HEAD_DIM + j)),
            pl.BlockSpec((1, s, HEAD_DIM), lambda i, j: (i, 0, PB_COL_VA // HEAD_DIM + j)),
        ],
        out_specs=pl.BlockSpec((1, s, gw), lambda i, j: (i, 0, j)),
        out_shape=jax.ShapeDtypeStruct((b, s, A_Q_W), BF16),
        scratch_shapes=[pltpu.VMEM((A_KV_HEADS, len(offsets), HEADS_PER_STEP * Q_BLOCK, kw), F32)],
        compiler_params=_params("arbitrary", "arbitrary"),
        name="attn_a",
    )(jnp.asarray(bkt), table_flat, sink, pb, pb, pb)


def _attn_b_group(pb, table_flat, gi):
    window, dil = B_PATTERNS[gi]
    b, s, _ = pb.shape
    sub = s // dil
    half = window // (2 * dil)
    kw, _, offsets = _window_geometry(sub, half)
    rps = dil
    bkt = _band_buckets(sub, half, dil)
    h0 = A_Q_HEADS + gi * B_HEADS_PER_GROUP
    nh = B_HEADS_PER_GROUP

    def head_specs(part):
        base = (PB_COL_QB + part * B_W + gi * B_GROUP_W) // HEAD_DIM
        return [pl.BlockSpec((1, s, HEAD_DIM), lambda i, r, c=base + h: (i, 0, c)) for h in range(nh)]

    kernel = functools.partial(
        _attn_kernel, seq_len=sub, half_w=half, dilation=dil, unroll=ATTN_UNROLL,
        res_per_step=rps, n_q_refs=nh, n_kv_refs=nh, n_o_refs=nh,
        has_sink=False, has_lse=True, head0=lambda r: h0, head_sets=1, buckets_used=_buckets_used(bkt))
    res = pl.pallas_call(
        kernel,
        grid=(b, dil // rps),
        in_specs=[pl.BlockSpec(bkt.shape, lambda i, r: (0, 0, 0)), _smem_spec()]
        + head_specs(0) + head_specs(1) + head_specs(2),
        out_specs=[pl.BlockSpec((1, s, HEAD_DIM), lambda i, r: (i, 0, 0))] * (nh + 1),
        out_shape=[jax.ShapeDtypeStruct((b, s, HEAD_DIM), F32)] * (nh + 1),
        scratch_shapes=[pltpu.VMEM((1, len(offsets), HEADS_PER_STEP * Q_BLOCK, kw), F32)],
        compiler_params=_params("arbitrary", "arbitrary"),
        name=f"attn_b{gi}",
    )(jnp.asarray(bkt), table_flat, *([pb] * (3 * nh)))
    return res[:nh], res[nh]


def _mix_kernel(*refs):
    nh, ng = B_HEADS_PER_GROUP, len(B_PATTERNS)
    it = iter(refs)
    ya_ref = next(it)
    o_refs = [[next(it) for _ in range(nh)] for _ in range(ng)]
    l_refs = [next(it) for _ in range(ng)]
    ga_ref, gb_ref, x_ref, wa_ref, wb_ref, wo_ref, g_ref, x1_ref, hf_ref = it
    heads = []
    for h in range(nh):
        ls = [r[:, h:h + 1] for r in l_refs]
        mx = functools.reduce(jnp.maximum, ls)
        es = [jnp.exp(l - mx) for l in ls]
        tot = functools.reduce(lambda a, c: a + c, es)
        y = functools.reduce(lambda a, c: a + c,
                             [(e / tot) * o_refs[g][h][...] for g, e in enumerate(es)])
        heads.append(y.astype(BF16))
    yb = jnp.concatenate(heads, axis=-1)
    ta = jnp.dot(ya_ref[...], wa_ref[...], preferred_element_type=F32)
    tb = jnp.dot(yb, wb_ref[...], preferred_element_type=F32)
    merged = (jax.nn.sigmoid(ga_ref[...].astype(F32)) * ta
              + jax.nn.sigmoid(gb_ref[...].astype(F32)) * tb)
    x1 = x_ref[...] + jnp.dot(merged.astype(BF16), wo_ref[...], preferred_element_type=F32)
    x1_ref[...] = x1
    hf_ref[...] = _rmsnorm(x1, g_ref[...]).astype(BF16)


def _resident(shape):
    return pl.BlockSpec(shape, lambda i: (0,) * len(shape), pipeline_mode=pl.Buffered(1))


def _mix(ya, outs, lses, pa2d, x2d, wa, wb, wo, gain, *, tm=256):
    m, d = x2d.shape
    assert PA_COL_GA % d == 0 and PA_COL_GB % d == 0

    def rows(width, col_block=0):
        return pl.BlockSpec((tm, width), lambda i: (i, col_block))

    flat_outs = [o for group in outs for o in group]
    return pl.pallas_call(
        _mix_kernel,
        grid=(m // tm,),
        in_specs=[rows(A_Q_W)]
        + [rows(HEAD_DIM)] * (len(flat_outs) + len(lses))
        + [rows(d, PA_COL_GA // d), rows(d, PA_COL_GB // d), rows(d),
           _resident(wa.shape), _resident(wb.shape), _resident(wo.shape), _resident(gain.shape)],
        out_specs=[rows(d), rows(d)],
        out_shape=[jax.ShapeDtypeStruct((m, d), F32), jax.ShapeDtypeStruct((m, d), BF16)],
        compiler_params=_params("parallel"),
        name="mix",
    )(ya, *flat_outs, *lses, pa2d, pa2d, x2d, wa, wb, wo, gain)


def _ffn_up_kernel(hf_ref, wg_ref, wu_ref, cw_ref, cb_ref, act_ref):
    hf = hf_ref[...]
    g = jnp.dot(hf, wg_ref[...], preferred_element_type=F32)
    u = jnp.dot(hf, wu_ref[...], preferred_element_type=F32)
    s = g.shape[0]
    row = lax.broadcasted_iota(jnp.int32, g.shape, 0)
    prev = jnp.where(row == 0, 0.0, pltpu.roll(g, 1, 0))
    nxt = jnp.where(row == s - 1, 0.0, pltpu.roll(g, s - 1, 0))
    cw = cw_ref[...]
    conv = prev * cw[0:1] + g * cw[1:2] + nxt * cw[2:3] + cb_ref[...]
    act_ref[...] = (jax.nn.gelu(conv) * u).astype(act_ref.dtype)


def _ffn_up(hf, wg, wu, cw, cb, *, seq, tf=512):
    m, d = hf.shape
    f = wg.shape[1]
    return pl.pallas_call(
        _ffn_up_kernel,
        grid=(m // seq, f // tf),
        in_specs=[
            pl.BlockSpec((seq, d), lambda i, j: (i, 0)),
            pl.BlockSpec((d, tf), lambda i, j: (0, j)),
            pl.BlockSpec((d, tf), lambda i, j: (0, j)),
            pl.BlockSpec((cw.shape[0], tf), lambda i, j: (0, j)),
            pl.BlockSpec((1, tf), lambda i, j: (0, j)),
        ],
        out_specs=pl.BlockSpec((seq, tf), lambda i, j: (i, j)),
        out_shape=jax.ShapeDtypeStruct((m, f), BF16),
        compiler_params=_params("parallel", "arbitrary"),
        name="ffn_up",
    )(hf, wg, wu, cw, cb)


def _ffn_down_kernel(act_ref, w_ref, x_ref, o_ref):
    o_ref[...] = x_ref[...] + jnp.dot(act_ref[...], w_ref[...], preferred_element_type=F32)


def _ffn_down(act, wd, x1, *, tm=512, tn=1024):
    m, f = act.shape
    n = wd.shape[1]
    return pl.pallas_call(
        _ffn_down_kernel,
        grid=(n // tn, m // tm),
        in_specs=[
            pl.BlockSpec((tm, f), lambda j, i: (i, 0)),
            pl.BlockSpec((f, tn), lambda j, i: (0, j)),
            pl.BlockSpec((tm, tn), lambda j, i: (i, j)),
        ],
        out_specs=pl.BlockSpec((tm, tn), lambda j, i: (i, j)),
        out_shape=jax.ShapeDtypeStruct((m, n), F32),
        compiler_params=_params("parallel", "parallel"),
        name="ffn_down",
    )(act, wd, x1)


def _ple_kernel(x_ref, p_ref, gp_ref, wg_ref, wp_ref, gf_ref, o_ref, *, final):
    x = x_ref[...]
    hp = _rmsnorm(x, gp_ref[...]).astype(BF16)
    gate = jax.nn.sigmoid(jnp.dot(hp, wg_ref[...], preferred_element_type=F32))
    emb = jnp.dot(p_ref[...].astype(BF16), wp_ref[...], preferred_element_type=F32)
    y = x + gate * emb
    o_ref[...] = _rmsnorm(y, gf_ref[...]) if final else y


def _ple(x2, p2d, gain_p, wpg, wpp, gain_f, *, final, tm=256):
    m, d = x2.shape
    pd = p2d.shape[1]
    return pl.pallas_call(
        functools.partial(_ple_kernel, final=final),
        grid=(m // tm,),
        in_specs=[
            pl.BlockSpec((tm, d), lambda i: (i, 0)),
            pl.BlockSpec((tm, pd), lambda i: (i, 0)),
            _resident(gain_p.shape), _resident(wpg.shape), _resident(wpp.shape),
            _resident(gain_f.shape),
        ],
        out_specs=pl.BlockSpec((tm, d), lambda i: (i, 0)),
        out_shape=jax.ShapeDtypeStruct((m, d), F32),
        compiler_params=_params("parallel"),
        name="ple",
    )(x2, p2d, gain_p, wpg, wpp, gain_f)


def kernel(x, p, rel_bias_table, attn_norm, w_in, sink_a, w_branch_a, w_branch_b, w_out,
           ffn_norm, w_ffn_gate, w_ffn_up, conv_w, conv_b, w_ffn_down,
           ple_norm, w_ple_gate, w_ple_proj, final_norm):
    b, s, d = x.shape
    depth = w_in.shape[0]
    assert d == D_MODEL and w_in.shape[2] == IN_PROJ_W and s % (Q_BLOCK * B_PATTERNS[-1][1]) == 0
    assert rel_bias_table.shape == (N_BUCKETS, N_BIAS_HEADS)
    m = b * s
    x2d = x.reshape(m, d)
    table_flat = rel_bias_table.reshape(-1)
    for i in range(depth):
        pb2d, pa2d = _inproj(x2d, attn_norm[i][None], w_in[i].astype(BF16))
        pb = pb2d.reshape(b, s, PB_W)
        ya = _attn_a(pb, table_flat, sink_a[i]).reshape(m, A_Q_W)
        groups = [_attn_b_group(pb, table_flat, gi) for gi in range(len(B_PATTERNS))]
        outs = [[o.reshape(m, HEAD_DIM) for o in os] for os, _ in groups]
        lses = [l.reshape(m, HEAD_DIM) for _, l in groups]
        x1, hf = _mix(ya, outs, lses, pa2d, x2d, w_branch_a[i].astype(BF16),
                      w_branch_b[i].astype(BF16), w_out[i].astype(BF16), ffn_norm[i][None])
        act = _ffn_up(hf, w_ffn_gate[i].astype(BF16), w_ffn_up[i].astype(BF16), conv_w[i],
                      conv_b[i][None], seq=s)
        x2 = _ffn_down(act, w_ffn_down[i].astype(BF16), x1)
        x2d = _ple(x2, p[i].reshape(m, -1), ple_norm[i][None], w_ple_gate[i].astype(BF16),
                   w_ple_proj[i].astype(BF16), final_norm[None], final=i == depth - 1)
    return x2d.reshape(b, s, d)
```

```python
import functools
import math

import jax
import jax.numpy as jnp
import numpy as np
from jax import lax
from jax.experimental import pallas as pl
from jax.experimental.pallas import tpu as pltpu

D_MODEL = 2048
HEAD_DIM = 128
A_Q_HEADS = 8
A_KV_HEADS = 2
A_GROUP = A_Q_HEADS // A_KV_HEADS
A_HALF_WINDOW = 128
B_PATTERNS = ((128, 1), (512, 4), (2048, 16))
B_HEADS_PER_GROUP = 4
N_BUCKETS = 32
MAX_DISTANCE = 1024
N_BIAS_HEADS = A_Q_HEADS + len(B_PATTERNS) * B_HEADS_PER_GROUP
A_Q_W = A_Q_HEADS * HEAD_DIM
A_KV_W = A_KV_HEADS * HEAD_DIM
B_GROUP_W = B_HEADS_PER_GROUP * HEAD_DIM
B_W = len(B_PATTERNS) * B_GROUP_W
IN_PROJ_W = A_Q_W + 2 * A_KV_W + 3 * B_W + 2 * D_MODEL
RMS_EPS = 1e-6
NEG_INF = -1e30

PB_W = A_Q_W + 2 * A_KV_W + 3 * B_W
PB_COL_KA = A_Q_W
PB_COL_VA = PB_COL_KA + A_KV_W
PB_COL_QB = PB_COL_VA + A_KV_W
PA_W = 2 * D_MODEL
PA_COL_GA = 0
PA_COL_GB = D_MODEL

Q_BLOCK = 128
HEADS_PER_STEP = 4
ATTN_UNROLL = 2
STAGE_STRIDE = 4
CHAIN_GROUP = (16, 1, 8)
A_HEADS_PER_CHAIN = 1
A_CHAIN_GROUP = 8

VMEM_LIMIT_BYTES = 56 * 1024 * 1024

BF16 = jnp.bfloat16
F32 = jnp.float32


def _params(*semantics):
    return pltpu.CompilerParams(dimension_semantics=semantics, vmem_limit_bytes=VMEM_LIMIT_BYTES)


def _rmsnorm(x, g):
    y = x * lax.rsqrt(jnp.mean(x * x, axis=-1, keepdims=True) + RMS_EPS)
    return y * g


def _inproj_kernel(x_ref, g_ref, w_ref, pb_ref, pa_ref, h_ref, *, row_chunk, n_b_tiles):
    j = pl.program_id(1)

    @pl.when(j == 0)
    def _():
        def body(c, _):
            rows = pl.ds(pl.multiple_of(c * row_chunk, row_chunk), row_chunk)
            h_ref[rows, :] = _rmsnorm(x_ref[rows, :], g_ref[...]).astype(BF16)
            return 0

        lax.fori_loop(0, x_ref.shape[0] // row_chunk, body, 0)

    @pl.when(j < n_b_tiles)
    def _():
        pb_ref[...] = jnp.dot(h_ref[...], w_ref[...], preferred_element_type=F32)

    @pl.when(j >= n_b_tiles)
    def _():
        pa_ref[...] = jnp.dot(h_ref[...], w_ref[...], preferred_element_type=F32).astype(BF16)


def _inproj(x2d, gain, w, *, tm=1024, tn=1024):
    m, k = x2d.shape
    assert PB_W % tn == 0 and PA_W % tn == 0 and w.shape[1] == PB_W + PA_W
    nb_t = PB_W // tn
    n_steps = w.shape[1] // tn

    return pl.pallas_call(
        functools.partial(_inproj_kernel, row_chunk=128, n_b_tiles=nb_t),
        grid=(m // tm, n_steps),
        in_specs=[
            pl.BlockSpec((tm, k), lambda i, j: (i, 0)),
            pl.BlockSpec((1, k), lambda i, j: (0, 0)),
            pl.BlockSpec((k, tn), lambda i, j: (0, j)),
        ],
        out_specs=[
            pl.BlockSpec((tm, tn), lambda i, j: (i, jnp.minimum(j, nb_t - 1))),
            pl.BlockSpec((tm, tn), lambda i, j: (i, jnp.maximum(j - nb_t, 0))),
        ],
        out_shape=[jax.ShapeDtypeStruct((m, PB_W), F32), jax.ShapeDtypeStruct((m, PA_W), BF16)],
        scratch_shapes=[pltpu.VMEM((tm, k), BF16)],
        compiler_params=_params("arbitrary", "arbitrary"),
        name="inproj",
    )(x2d, gain, w)


def _t5_bucket_static(rel):
    half = N_BUCKETS // 2
    max_exact = half // 2
    n = np.abs(rel)
    side = np.where(rel > 0, half, 0)
    nf = np.maximum(n, 1).astype(np.float32)
    large = max_exact + (np.log(nf / max_exact) / math.log(MAX_DISTANCE / max_exact)
                         * (half - max_exact)).astype(np.int32)
    large = np.minimum(large, half - 1)
    return side + np.where(n < max_exact, n, large)


def _window_geometry(seq_len, half_w):
    nb = seq_len // Q_BLOCK
    if nb == 1:
        return seq_len, nb, (0,)
    kw = Q_BLOCK + 2 * half_w
    assert half_w <= Q_BLOCK and kw <= seq_len
    return kw, nb, (0, -half_w, Q_BLOCK - kw)


def _band_buckets(seq_len, half_w, dilation):
    kw, _, offsets = _window_geometry(seq_len, half_w)
    qi = np.arange(Q_BLOCK)[:, None]
    kj = np.arange(kw)[None, :]
    rel = np.stack([off + kj - qi for off in offsets])
    buckets = _t5_bucket_static(rel * dilation)
    return np.where(np.abs(rel) <= half_w, buckets, -1).astype(np.int32)


def _attn_kernel(*refs, seq_len, half_w, dilation, stage, n_q_refs, n_kv_refs, has_sink, has_lse,
                 head0, head_sets, buckets_used, chain_group, unroll=1,
                 heads_per_chain=1):
    it = iter(refs)
    bkt_ref, table_ref = next(it), next(it)
    sink_ref = next(it) if has_sink else None
    qkv_refs = [[next(it) for _ in range(n)] for n in (n_q_refs, n_kv_refs, n_kv_refs)]
    out_refs = [[next(it) for _ in range(n_q_refs)]]
    if has_lse:
        out_refs.append([next(it) for _ in range(n_q_refs)])
    bias_ref = next(it)
    stage_in, stage_out = (next(it), next(it)) if stage > 1 else (None, None)

    kw, nb, offsets = _window_geometry(seq_len, half_w)
    scale = HEAD_DIM ** -0.5
    step = pl.program_id(1)
    first_head = head0(step)
    head_set = step if head_sets > 1 else 0
    hop2 = dilation // stage
    staged_rows = seq_len * hop2
    assert heads_per_chain == 1 or n_kv_refs == 1
    all_chains = [list(range(c, c + heads_per_chain))
                  for c in range(0, HEADS_PER_STEP, heads_per_chain)]

    @pl.when((pl.program_id(0) == 0) & (step < head_sets))
    def _():
        for var in range(len(offsets)):
            bkt = bkt_ref[var]
            for h in range(HEADS_PER_STEP):
                acc = jnp.full(bkt.shape, NEG_INF, F32)
                for bucket in buckets_used:
                    acc = jnp.where(bkt == bucket,
                                    table_ref[bucket * N_BIAS_HEADS + first_head + h], acc)
                bias_ref[head_set, var, h * Q_BLOCK:(h + 1) * Q_BLOCK, :] = acc

    def rows(residue, start, size):
        if dilation == 1:
            return pl.ds(start, size)
        if stage > 1:
            return pl.ds(residue // stage + start * hop2, size, stride=hop2)
        return pl.ds(residue + start * dilation, size, stride=dilation)

    def head_view(ref_list, h):
        if len(ref_list) > 1:
            return ref_list[h], slice(None)
        if ref_list[0].shape[-1] == HEAD_DIM:
            return ref_list[0], slice(None)
        return ref_list[0], slice(h * HEAD_DIM, (h + 1) * HEAD_DIM)

    def load(t, h, residue, start, size):
        if stage > 1:
            return stage_in[h % 2, t, residue % stage, rows(residue, start, size), :].astype(BF16)
        ref, cols = head_view(qkv_refs[t], h)
        return ref[0, rows(residue, start, size), cols].astype(BF16)

    def store(t, h, residue, start, value):
        if stage > 1:
            stage_out[t * 2 + h % 2, residue % stage, rows(residue, start, Q_BLOCK), :] = value
        else:
            ref, cols = head_view(out_refs[t], h)
            ref[0, rows(residue, start, Q_BLOCK), cols] = value.astype(ref.dtype)

    ones = jnp.ones((kw, HEAD_DIM), BF16)

    def geometry(i):
        if isinstance(i, int):
            qs = i * Q_BLOCK
            return qs, min(max(qs - half_w, 0), seq_len - kw), 0 if i == 0 else (2 if i == nb - 1 else 1)
        qs = pl.multiple_of(i * Q_BLOCK, Q_BLOCK)
        ks = pl.multiple_of(jnp.clip(qs - half_w, 0, seq_len - kw), half_w)
        return qs, ks, jnp.where(i == 0, 0, jnp.where(i == nb - 1, 2, 1))

    def group(i0, _, items):
        scores = []
        for heads, residue, di in items:
            qs, ks, var = geometry(i0 + di)
            q = jnp.concatenate([load(0, h, residue, qs, Q_BLOCK) for h in heads], axis=0)
            k = load(1, heads[0], residue, ks, kw)
            bias = bias_ref[head_set, var, heads[0] * Q_BLOCK:(heads[-1] + 1) * Q_BLOCK, :]
            s = lax.dot_general(q, k, (((1,), (1,)), ((), ())), preferred_element_type=F32)
            scores.append(s * scale + bias)
        for (heads, residue, di), s in zip(items, scores):
            qs, ks, _ = geometry(i0 + di)
            v_ext = jnp.concatenate([load(2, heads[0], residue, ks, kw), ones], axis=1)
            m = jnp.max(s, axis=-1, keepdims=True)
            if has_sink:
                sk = jnp.concatenate(
                    [jnp.full((Q_BLOCK, 1), sink_ref[first_head + h], F32) for h in heads], axis=0)
                m = jnp.maximum(m, sk)
            p = jnp.exp(s - m).astype(BF16)
            ov = jnp.dot(p, v_ext, preferred_element_type=F32)
            denom = ov[:, HEAD_DIM:]
            if has_sink:
                denom = denom + jnp.exp(sk - m)
            o = ov[:, :HEAD_DIM] / denom
            lse = m + jnp.log(denom) if has_lse else None
            for n, h in enumerate(heads):
                store(0, h, residue, qs, o[n * Q_BLOCK:(n + 1) * Q_BLOCK])
                if has_lse:
                    store(1, h, residue, qs, lse[n * Q_BLOCK:(n + 1) * Q_BLOCK])
        return 0

    if stage > 1:
        for h in range(HEADS_PER_STEP):
            for t in range(3):
                for r1 in range(stage):
                    stage_in[h % 2, t, r1] = qkv_refs[t][h][0, pl.ds(r1, staged_rows, stride=stage), :]
            work = [([h], r, i) for r in range(dilation) for i in range(nb)]
            for c in range(0, len(work), chain_group):
                group(0, 0, work[c:c + chain_group])
            for t, ref_list in enumerate(out_refs):
                for r1 in range(stage):
                    ref_list[h][0, pl.ds(r1, staged_rows, stride=stage), :] = stage_out[t * 2 + h % 2, r1]
    elif dilation > 1:
        work = [(heads, r, i) for r in range(dilation) for i in range(nb) for heads in all_chains]
        for c in range(0, len(work), chain_group):
            group(0, 0, work[c:c + chain_group])
    else:
        per_iter = max(1, min(chain_group // len(all_chains), nb))
        assert nb % per_iter == 0
        items = [(heads, 0, di) for di in range(per_iter) for heads in all_chains]
        lax.fori_loop(0, nb // per_iter, lambda it, c: group(it * per_iter, c, items), 0,
                      unroll=unroll)


def _buckets_used(bkt):
    return tuple(int(v) for v in np.unique(bkt) if v >= 0)


def _smem_spec():
    return pl.BlockSpec(memory_space=pltpu.SMEM)


def _attn_a(pb, table_flat, sink):
    b, s, _ = pb.shape
    gw = A_GROUP * HEAD_DIM
    kw, _, offsets = _window_geometry(s, A_HALF_WINDOW)
    bkt = _band_buckets(s, A_HALF_WINDOW, 1)
    kernel = functools.partial(
        _attn_kernel, seq_len=s, half_w=A_HALF_WINDOW, dilation=1, stage=1,
        chain_group=A_CHAIN_GROUP, heads_per_chain=A_HEADS_PER_CHAIN, unroll=ATTN_UNROLL,
        n_q_refs=1, n_kv_refs=1, has_sink=True, has_lse=False,
        head0=lambda kv: kv * A_GROUP, head_sets=A_KV_HEADS, buckets_used=_buckets_used(bkt))
    return pl.pallas_call(
        kernel,
        grid=(b, A_KV_HEADS),
        in_specs=[
            pl.BlockSpec(bkt.shape, lambda i, j: (0, 0, 0)),
            _smem_spec(),
            _smem_spec(),
            pl.BlockSpec((1, s, gw), lambda i, j: (i, 0, j)),
            pl.BlockSpec((1, s, HEAD_DIM), lambda i, j: (i, 0, PB_COL_KA // HEAD_DIM + j)),
            pl.BlockSpec((1, s, HEAD_DIM), lambda i, j: (i, 0, PB_COL_VA // HEAD_DIM + j)),
        ],
        out_specs=pl.BlockSpec((1, s, gw), lambda i, j: (i, 0, j)),
        out_shape=jax.ShapeDtypeStruct((b, s, A_Q_W), BF16),
        scratch_shapes=[pltpu.VMEM((A_KV_HEADS, len(offsets), HEADS_PER_STEP * Q_BLOCK, kw), F32)],
        compiler_params=_params("arbitrary", "arbitrary"),
        name="attn_a",
    )(jnp.asarray(bkt), table_flat, sink, pb, pb, pb)


def _attn_b_group(pb, table_flat, gi):
    window, dil = B_PATTERNS[gi]
    b, s, _ = pb.shape
    sub = s // dil
    half = window // (2 * dil)
    kw, _, offsets = _window_geometry(sub, half)
    stage = STAGE_STRIDE if dil > STAGE_STRIDE else 1
    bkt = _band_buckets(sub, half, dil)
    h0 = A_Q_HEADS + gi * B_HEADS_PER_GROUP
    nh = B_HEADS_PER_GROUP

    def head_specs(part):
        base = (PB_COL_QB + part * B_W + gi * B_GROUP_W) // HEAD_DIM
        return [pl.BlockSpec((1, s, HEAD_DIM), lambda i, r, c=base + h: (i, 0, c)) for h in range(nh)]

    scratch = [pltpu.VMEM((1, len(offsets), HEADS_PER_STEP * Q_BLOCK, kw), F32)]
    if stage > 1:
        scratch += [pltpu.VMEM((2, 3, stage, s // stage, HEAD_DIM), F32),
                    pltpu.VMEM((2 * 2, stage, s // stage, HEAD_DIM), F32)]
    kernel = functools.partial(
        _attn_kernel, seq_len=sub, half_w=half, dilation=dil, stage=stage,
        chain_group=CHAIN_GROUP[gi],
        n_q_refs=nh, n_kv_refs=nh, has_sink=False, has_lse=True, head0=lambda r: h0, head_sets=1,
        buckets_used=_buckets_used(bkt))
    res = pl.pallas_call(
        kernel,
        grid=(b, 1),
        in_specs=[pl.BlockSpec(bkt.shape, lambda i, r: (0, 0, 0)), _smem_spec()]
        + head_specs(0) + head_specs(1) + head_specs(2),
        out_specs=[pl.BlockSpec((1, s, HEAD_DIM), lambda i, r: (i, 0, 0))] * (2 * nh),
        out_shape=[jax.ShapeDtypeStruct((b, s, HEAD_DIM), F32)] * (2 * nh),
        scratch_shapes=scratch,
        compiler_params=_params("arbitrary", "arbitrary"),
        name=f"attn_b{gi}",
    )(jnp.asarray(bkt), table_flat, *([pb] * (3 * nh)))
    return res[:nh], res[nh:]


def _mix_kernel(*refs):
    nh, ng = B_HEADS_PER_GROUP, len(B_PATTERNS)
    it = iter(refs)
    ya_ref = next(it)
    o_refs = [[next(it) for _ in range(nh)] for _ in range(ng)]
    l_refs = [[next(it) for _ in range(nh)] for _ in range(ng)]
    ga_ref, gb_ref, x_ref, wa_ref, wb_ref, wo_ref, g_ref, x1_ref, hf_ref = it
    heads = []
    for h in range(nh):
        ls = [l_refs[g][h][...] for g in range(ng)]
        mx = functools.reduce(jnp.maximum, ls)
        es = [jnp.exp(l - mx) for l in ls]
        tot = functools.reduce(lambda a, c: a + c, es)
        y = functools.reduce(lambda a, c: a + c,
                             [(e / tot) * o_refs[g][h][...] for g, e in enumerate(es)])
        heads.append(y.astype(BF16))
    yb = jnp.concatenate(heads, axis=-1)
    ta = jnp.dot(ya_ref[...], wa_ref[...], preferred_element_type=F32)
    tb = jnp.dot(yb, wb_ref[...], preferred_element_type=F32)
    merged = (jax.nn.sigmoid(ga_ref[...].astype(F32)) * ta
              + jax.nn.sigmoid(gb_ref[...].astype(F32)) * tb)
    x1 = x_ref[...] + jnp.dot(merged.astype(BF16), wo_ref[...], preferred_element_type=F32)
    x1_ref[...] = x1
    hf_ref[...] = _rmsnorm(x1, g_ref[...]).astype(BF16)


def _resident(shape):
    return pl.BlockSpec(shape, lambda i: (0,) * len(shape), pipeline_mode=pl.Buffered(1))


def _mix(ya, outs, lses, pa2d, x2d, wa, wb, wo, gain, *, tm=256):
    m, d = x2d.shape
    assert PA_COL_GA % d == 0 and PA_COL_GB % d == 0

    def rows(width, col_block=0):
        return pl.BlockSpec((tm, width), lambda i: (i, col_block))

    flat_outs = [o for group in outs for o in group]
    flat_lses = [l for group in lses for l in group]
    return pl.pallas_call(
        _mix_kernel,
        grid=(m // tm,),
        in_specs=[rows(A_Q_W)]
        + [rows(HEAD_DIM)] * (len(flat_outs) + len(flat_lses))
        + [rows(d, PA_COL_GA // d), rows(d, PA_COL_GB // d), rows(d),
           _resident(wa.shape), _resident(wb.shape), _resident(wo.shape), _resident(gain.shape)],
        out_specs=[rows(d), rows(d)],
        out_shape=[jax.ShapeDtypeStruct((m, d), F32), jax.ShapeDtypeStruct((m, d), BF16)],
        compiler_params=_params("parallel"),
        name="mix",
    )(ya, *flat_outs, *flat_lses, pa2d, pa2d, x2d, wa, wb, wo, gain)


def _ffn_up_kernel(hf_ref, wg_ref, wu_ref, cw_ref, cb_ref, act_ref):
    hf = hf_ref[...]
    g = jnp.dot(hf, wg_ref[...], preferred_element_type=F32)
    u = jnp.dot(hf, wu_ref[...], preferred_element_type=F32)
    s = g.shape[0]
    row = lax.broadcasted_iota(jnp.int32, g.shape, 0)
    prev = jnp.where(row == 0, 0.0, pltpu.roll(g, 1, 0))
    nxt = jnp.where(row == s - 1, 0.0, pltpu.roll(g, s - 1, 0))
    cw = cw_ref[...]
    conv = prev * cw[0:1] + g * cw[1:2] + nxt * cw[2:3] + cb_ref[...]
    act_ref[...] = (jax.nn.gelu(conv) * u).astype(act_ref.dtype)


def _ffn_up(hf, wg, wu, cw, cb, *, seq, tf=512):
    m, d = hf.shape
    f = wg.shape[1]
    return pl.pallas_call(
        _ffn_up_kernel,
        grid=(m // seq, f // tf),
        in_specs=[
            pl.BlockSpec((seq, d), lambda i, j: (i, 0)),
            pl.BlockSpec((d, tf), lambda i, j: (0, j)),
            pl.BlockSpec((d, tf), lambda i, j: (0, j)),
            pl.BlockSpec((cw.shape[0], tf), lambda i, j: (0, j)),
            pl.BlockSpec((1, tf), lambda i, j: (0, j)),
        ],
        out_specs=pl.BlockSpec((seq, tf), lambda i, j: (i, j)),
        out_shape=jax.ShapeDtypeStruct((m, f), BF16),
        compiler_params=_params("parallel", "arbitrary"),
        name="ffn_up",
    )(hf, wg, wu, cw, cb)


def _ffn_down_kernel(act_ref, w_ref, x_ref, o_ref):
    o_ref[...] = x_ref[...] + jnp.dot(act_ref[...], w_ref[...], preferred_element_type=F32)


def _ffn_down(act, wd, x1, *, tm=512, tn=1024):
    m, f = act.shape
    n = wd.shape[1]
    return pl.pallas_call(
        _ffn_down_kernel,
        grid=(n // tn, m // tm),
        in_specs=[
            pl.BlockSpec((tm, f), lambda j, i: (i, 0)),
            pl.BlockSpec((f, tn), lambda j, i: (0, j)),
            pl.BlockSpec((tm, tn), lambda j, i: (i, j)),
        ],
        out_specs=pl.BlockSpec((tm, tn), lambda j, i: (i, j)),
        out_shape=jax.ShapeDtypeStruct((m, n), F32),
        compiler_params=_params("parallel", "parallel"),
        name="ffn_down",
    )(act, wd, x1)


def _ple_kernel(x_ref, p_ref, gp_ref, wg_ref, wp_ref, gf_ref, o_ref, *, final):
    x = x_ref[...]
    hp = _rmsnorm(x, gp_ref[...]).astype(BF16)
    gate = jax.nn.sigmoid(jnp.dot(hp, wg_ref[...], preferred_element_type=F32))
    emb = jnp.dot(p_ref[...].astype(BF16), wp_ref[...], preferred_element_type=F32)
    y = x + gate * emb
    o_ref[...] = _rmsnorm(y, gf_ref[...]) if final else y


def _ple(x2, p2d, gain_p, wpg, wpp, gain_f, *, final, tm=256):
    m, d = x2.shape
    pd = p2d.shape[1]
    return pl.pallas_call(
        functools.partial(_ple_kernel, final=final),
        grid=(m // tm,),
        in_specs=[
            pl.BlockSpec((tm, d), lambda i: (i, 0)),
            pl.BlockSpec((tm, pd), lambda i: (i, 0)),
            _resident(gain_p.shape), _resident(wpg.shape), _resident(wpp.shape),
            _resident(gain_f.shape),
        ],
        out_specs=pl.BlockSpec((tm, d), lambda i: (i, 0)),
        out_shape=jax.ShapeDtypeStruct((m, d), F32),
        compiler_params=_params("parallel"),
        name="ple",
    )(x2, p2d, gain_p, wpg, wpp, gain_f)


def kernel(x, p, rel_bias_table, attn_norm, w_in, sink_a, w_branch_a, w_branch_b, w_out,
           ffn_norm, w_ffn_gate, w_ffn_up, conv_w, conv_b, w_ffn_down,
           ple_norm, w_ple_gate, w_ple_proj, final_norm):
    b, s, d = x.shape
    depth = w_in.shape[0]
    assert d == D_MODEL and w_in.shape[2] == IN_PROJ_W and s % (Q_BLOCK * B_PATTERNS[-1][1]) == 0
    assert rel_bias_table.shape == (N_BUCKETS, N_BIAS_HEADS)
    m = b * s
    x2d = x.reshape(m, d)
    table_flat = rel_bias_table.reshape(-1)
    for i in range(depth):
        pb2d, pa2d = _inproj(x2d, attn_norm[i][None], w_in[i].astype(BF16))
        pb = pb2d.reshape(b, s, PB_W)
        ya = _attn_a(pb, table_flat, sink_a[i]).reshape(m, A_Q_W)
        groups = [_attn_b_group(pb, table_flat, gi) for gi in range(len(B_PATTERNS))]
        outs = [[o.reshape(m, HEAD_DIM) for o in os] for os, _ in groups]
        lses = [[l.reshape(m, HEAD_DIM) for l in ls] for _, ls in groups]
        x1, hf = _mix(ya, outs, lses, pa2d, x2d, w_branch_a[i].astype(BF16),
                      w_branch_b[i].astype(BF16), w_out[i].astype(BF16), ffn_norm[i][None])
        act = _ffn_up(hf, w_ffn_gate[i].astype(BF16), w_ffn_up[i].astype(BF16), conv_w[i],
                      conv_b[i][None], seq=s)
        x2 = _ffn_down(act, w_ffn_down[i].astype(BF16), x1)
        x2d = _ple(x2, p[i].reshape(m, -1), ple_norm[i][None], w_ple_gate[i].astype(BF16),
                   w_ple_proj[i].astype(BF16), final_norm[None], final=i == depth - 1)
    return x2d.reshape(b, s, d)
```

```python
import functools
import math

import jax
import jax.numpy as jnp
import numpy as np
from jax import lax
from jax.experimental import pallas as pl
from jax.experimental.pallas import tpu as pltpu

D_MODEL = 2048
HEAD_DIM = 128
A_Q_HEADS = 8
A_KV_HEADS = 2
A_GROUP = A_Q_HEADS // A_KV_HEADS
A_HALF_WINDOW = 128
B_PATTERNS = ((128, 1), (512, 4), (2048, 16))
B_HEADS_PER_GROUP = 4
N_BUCKETS = 32
MAX_DISTANCE = 1024
N_BIAS_HEADS = A_Q_HEADS + len(B_PATTERNS) * B_HEADS_PER_GROUP
A_Q_W = A_Q_HEADS * HEAD_DIM
A_KV_W = A_KV_HEADS * HEAD_DIM
B_GROUP_W = B_HEADS_PER_GROUP * HEAD_DIM
B_W = len(B_PATTERNS) * B_GROUP_W
IN_PROJ_W = A_Q_W + 2 * A_KV_W + 3 * B_W + 2 * D_MODEL
RMS_EPS = 1e-6
NEG_INF = -1e30

PB_W = A_Q_W + 2 * A_KV_W + 3 * B_W
PB_COL_KA = A_Q_W
PB_COL_VA = PB_COL_KA + A_KV_W
PB_COL_QB = PB_COL_VA + A_KV_W
PA_W = 2 * D_MODEL
PA_COL_GA = 0
PA_COL_GB = D_MODEL

Q_BLOCK = 128
ATTN_UNROLL = 2
STAGE_STRIDE = 4
CHAIN_GROUP = (16, 1, 8)
A_BLOCKS_PER_GROUP = 2
MERGE_ROWS = 256

VMEM_LIMIT_BYTES = 56 * 1024 * 1024

BF16 = jnp.bfloat16
F32 = jnp.float32


def _params(*semantics):
    return pltpu.CompilerParams(dimension_semantics=semantics, vmem_limit_bytes=VMEM_LIMIT_BYTES)


def _rmsnorm(x, g):
    y = x * lax.rsqrt(jnp.mean(x * x, axis=-1, keepdims=True) + RMS_EPS)
    return y * g


def _inproj_kernel(x_ref, g_ref, w_ref, pb_ref, pa_ref, h_ref, *, row_chunk, n_b_tiles):
    j = pl.program_id(1)

    @pl.when(j == 0)
    def _():
        def body(c, _):
            rows = pl.ds(pl.multiple_of(c * row_chunk, row_chunk), row_chunk)
            h_ref[rows, :] = _rmsnorm(x_ref[rows, :], g_ref[...]).astype(BF16)
            return 0

        lax.fori_loop(0, x_ref.shape[0] // row_chunk, body, 0)

    @pl.when(j < n_b_tiles)
    def _():
        pb_ref[...] = jnp.dot(h_ref[...], w_ref[...], preferred_element_type=F32)

    @pl.when(j >= n_b_tiles)
    def _():
        pa_ref[...] = jnp.dot(h_ref[...], w_ref[...], preferred_element_type=F32).astype(BF16)


def _inproj(x2d, gain, w, *, tm=1024, tn=1024):
    m, k = x2d.shape
    assert PB_W % tn == 0 and PA_W % tn == 0 and w.shape[1] == PB_W + PA_W
    nb_t = PB_W // tn
    n_steps = w.shape[1] // tn

    return pl.pallas_call(
        functools.partial(_inproj_kernel, row_chunk=128, n_b_tiles=nb_t),
        grid=(m // tm, n_steps),
        in_specs=[
            pl.BlockSpec((tm, k), lambda i, j: (i, 0)),
            pl.BlockSpec((1, k), lambda i, j: (0, 0)),
            pl.BlockSpec((k, tn), lambda i, j: (0, j)),
        ],
        out_specs=[
            pl.BlockSpec((tm, tn), lambda i, j: (i, jnp.minimum(j, nb_t - 1))),
            pl.BlockSpec((tm, tn), lambda i, j: (i, jnp.maximum(j - nb_t, 0))),
        ],
        out_shape=[jax.ShapeDtypeStruct((m, PB_W), F32), jax.ShapeDtypeStruct((m, PA_W), BF16)],
        scratch_shapes=[pltpu.VMEM((tm, k), BF16)],
        compiler_params=_params("arbitrary", "arbitrary"),
        name="inproj",
    )(x2d, gain, w)


def _t5_bucket_static(rel):
    half = N_BUCKETS // 2
    max_exact = half // 2
    n = np.abs(rel)
    side = np.where(rel > 0, half, 0)
    nf = np.maximum(n, 1).astype(np.float32)
    large = max_exact + (np.log(nf / max_exact) / math.log(MAX_DISTANCE / max_exact)
                         * (half - max_exact)).astype(np.int32)
    large = np.minimum(large, half - 1)
    return side + np.where(n < max_exact, n, large)


def _window_geometry(seq_len, half_w):
    nb = seq_len // Q_BLOCK
    if nb == 1:
        return seq_len, nb, (0,)
    kw = Q_BLOCK + 2 * half_w
    assert half_w <= Q_BLOCK and kw <= seq_len
    return kw, nb, (0, -half_w, Q_BLOCK - kw)


def _band_buckets(seq_len, half_w, dilation):
    kw, _, offsets = _window_geometry(seq_len, half_w)
    qi = np.arange(Q_BLOCK)[:, None]
    kj = np.arange(kw)[None, :]
    rel = np.stack([off + kj - qi for off in offsets])
    buckets = _t5_bucket_static(rel * dilation)
    return np.where(np.abs(rel) <= half_w, buckets, -1).astype(np.int32)


def _block_geometry(i, seq_len, half_w):
    kw, nb, _ = _window_geometry(seq_len, half_w)
    if isinstance(i, int):
        qs = i * Q_BLOCK
        return qs, min(max(qs - half_w, 0), seq_len - kw), 0 if i == 0 else (2 if i == nb - 1 else 1)
    qs = pl.multiple_of(i * Q_BLOCK, Q_BLOCK)
    ks = pl.multiple_of(jnp.clip(qs - half_w, 0, seq_len - kw), half_w)
    return qs, ks, jnp.where(i == 0, 0, jnp.where(i == nb - 1, 2, 1))


def _build_bias(bkt_ref, table_ref, bias_ref, slot, first_head, n_heads, buckets_used):
    for kind in range(bkt_ref.shape[0]):
        bkt = bkt_ref[kind]
        for h in range(n_heads):
            acc = jnp.full(bkt.shape, NEG_INF, F32)
            for bucket in buckets_used:
                acc = jnp.where(bkt == bucket, table_ref[bucket * N_BIAS_HEADS + first_head + h], acc)
            bias_ref[slot, kind, h * Q_BLOCK:(h + 1) * Q_BLOCK, :] = acc


def _softmax_chains(items, *, load, store, bias, kw, sink=None, want_lse):
    scale = HEAD_DIM ** -0.5
    ones = jnp.ones((kw, HEAD_DIM), BF16)
    scores = []
    for tag, qs, ks, kind in items:
        s = lax.dot_general(load(0, tag, qs, Q_BLOCK), load(1, tag, ks, kw),
                            (((1,), (1,)), ((), ())), preferred_element_type=F32)
        scores.append(s * scale + bias(tag, kind))
    for (tag, qs, ks, _), s in zip(items, scores):
        v_ext = jnp.concatenate([load(2, tag, ks, kw), ones], axis=1)
        m = jnp.max(s, axis=-1, keepdims=True)
        if sink is not None:
            sk = sink(tag)
            m = jnp.maximum(m, sk)
        p = jnp.exp(s - m).astype(BF16)
        ov = jnp.dot(p, v_ext, preferred_element_type=F32)
        denom = ov[:, HEAD_DIM:]
        if sink is not None:
            denom = denom + jnp.exp(sk - m)
        store(0, tag, qs, ov[:, :HEAD_DIM] / denom)
        if want_lse:
            store(1, tag, qs, m + jnp.log(denom))


def _attn_a_kernel(bkt_ref, table_ref, sink_ref, q_ref, k_ref, v_ref, o_ref, bias_ref, *,
                   seq_len, buckets_used):
    kv = pl.program_id(1)
    kw, nb, _ = _window_geometry(seq_len, A_HALF_WINDOW)

    @pl.when(pl.program_id(0) == 0)
    def _():
        _build_bias(bkt_ref, table_ref, bias_ref, kv, kv * A_GROUP, A_GROUP, buckets_used)

    def load(t, h, start, size):
        if t == 0:
            return q_ref[0, pl.ds(start, size), h * HEAD_DIM:(h + 1) * HEAD_DIM].astype(BF16)
        return (k_ref, v_ref)[t - 1][0, pl.ds(start, size), :].astype(BF16)

    def store(t, h, start, value):
        o_ref[0, pl.ds(start, Q_BLOCK), h * HEAD_DIM:(h + 1) * HEAD_DIM] = value.astype(o_ref.dtype)

    def body(it, carry):
        items = [(h,) + _block_geometry(it * A_BLOCKS_PER_GROUP + di, seq_len, A_HALF_WINDOW)
                 for di in range(A_BLOCKS_PER_GROUP) for h in range(A_GROUP)]
        _softmax_chains(
            items, load=load, store=store, kw=kw, want_lse=False,
            bias=lambda h, kind: bias_ref[kv, kind, h * Q_BLOCK:(h + 1) * Q_BLOCK, :],
            sink=lambda h: jnp.full((Q_BLOCK, 1), sink_ref[kv * A_GROUP + h], F32))
        return carry

    assert nb % A_BLOCKS_PER_GROUP == 0
    lax.fori_loop(0, nb // A_BLOCKS_PER_GROUP, body, 0, unroll=ATTN_UNROLL)


def _buckets_used(bkt):
    return tuple(int(v) for v in np.unique(bkt) if v >= 0)


def _smem_spec():
    return pl.BlockSpec(memory_space=pltpu.SMEM)


def _attn_a(pb, table_flat, sink):
    b, s, _ = pb.shape
    gw = A_GROUP * HEAD_DIM
    kw, _, offsets = _window_geometry(s, A_HALF_WINDOW)
    bkt = _band_buckets(s, A_HALF_WINDOW, 1)
    kernel = functools.partial(_attn_a_kernel, seq_len=s, buckets_used=_buckets_used(bkt))
    return pl.pallas_call(
        kernel,
        grid=(b, A_KV_HEADS),
        in_specs=[
            pl.BlockSpec(bkt.shape, lambda i, j: (0, 0, 0)),
            _smem_spec(),
            _smem_spec(),
            pl.BlockSpec((1, s, gw), lambda i, j: (i, 0, j)),
            pl.BlockSpec((1, s, HEAD_DIM), lambda i, j: (i, 0, PB_COL_KA // HEAD_DIM + j)),
            pl.BlockSpec((1, s, HEAD_DIM), lambda i, j: (i, 0, PB_COL_VA // HEAD_DIM + j)),
        ],
        out_specs=pl.BlockSpec((1, s, gw), lambda i, j: (i, 0, j)),
        out_shape=jax.ShapeDtypeStruct((b, s, A_Q_W), BF16),
        scratch_shapes=[pltpu.VMEM((A_KV_HEADS, len(offsets), A_GROUP * Q_BLOCK, kw), F32)],
        compiler_params=_params("arbitrary", "arbitrary"),
        name="attn_a",
    )(jnp.asarray(bkt), table_flat, sink, pb, pb, pb)


def _attn_b_kernel(*refs, seq, groups):
    ng = len(groups)
    it = iter(refs)
    bkt_refs = [next(it) for _ in range(ng)]
    table_ref = next(it)
    qkv_refs = [[next(it) for _ in range(3)] for _ in range(ng)]
    yb_ref = next(it)
    bias_refs = [next(it) for _ in range(ng)]
    o_acc, l_acc, stage_in, stage_out = next(it), next(it), next(it), next(it)
    head = pl.program_id(1)

    @pl.when(pl.program_id(0) == 0)
    def _():
        for g, grp in enumerate(groups):
            _build_bias(bkt_refs[g], table_ref, bias_refs[g], head, grp["head0"] + head, 1,
                        grp["buckets_used"])

    for g, grp in enumerate(groups):
        dil, sub, half, chain_group = grp["dil"], grp["sub"], grp["half"], grp["chain_group"]
        kw, nb, _ = _window_geometry(sub, half)
        stage = STAGE_STRIDE if dil > STAGE_STRIDE else 1
        hop2 = dil // stage
        accs = (o_acc, l_acc)

        def rows(residue, start, size, dil=dil, stage=stage, hop2=hop2):
            if dil == 1:
                return pl.ds(start, size)
            if stage > 1:
                return pl.ds(residue // stage + start * hop2, size, stride=hop2)
            return pl.ds(residue + start * dil, size, stride=dil)

        def load(t, residue, start, size, g=g, stage=stage, rows=rows):
            if stage > 1:
                return stage_in[t, residue % stage, rows(residue, start, size), :].astype(BF16)
            return qkv_refs[g][t][0, rows(residue, start, size), :].astype(BF16)

        def store(t, residue, start, value, g=g, stage=stage, rows=rows, accs=accs):
            if stage > 1:
                stage_out[t, residue % stage, rows(residue, start, Q_BLOCK), :] = value
            else:
                accs[t][g, rows(residue, start, Q_BLOCK), :] = value

        def bias(residue, kind, g=g):
            return bias_refs[g][head, kind]

        run = functools.partial(_softmax_chains, load=load, store=store, bias=bias, kw=kw,
                                want_lse=True)
        if dil == 1:
            per_iter = min(chain_group, nb)
            assert nb % per_iter == 0

            def body(it_, carry, run=run, per_iter=per_iter, sub=sub, half=half):
                run([(0,) + _block_geometry(it_ * per_iter + di, sub, half) for di in range(per_iter)])
                return carry

            lax.fori_loop(0, nb // per_iter, body, 0)
            continue
        if stage > 1:
            for t in range(3):
                for r1 in range(stage):
                    stage_in[t, r1] = qkv_refs[g][t][0, pl.ds(r1, seq // stage, stride=stage), :]
        work = [(r,) + _block_geometry(i, sub, half) for r in range(dil) for i in range(nb)]
        for c in range(0, len(work), chain_group):
            run(work[c:c + chain_group])
        if stage > 1:
            for t in range(2):
                for r1 in range(stage):
                    accs[t][g, pl.ds(r1, seq // stage, stride=stage), :] = stage_out[t, r1]

    def merge(c, carry):
        rws = pl.ds(pl.multiple_of(c * MERGE_ROWS, MERGE_ROWS), MERGE_ROWS)
        ls = [l_acc[g, rws, :] for g in range(ng)]
        mx = functools.reduce(jnp.maximum, ls)
        es = [jnp.exp(l - mx) for l in ls]
        tot = functools.reduce(lambda a, e: a + e, es)
        y = functools.reduce(lambda a, e: a + e, [e * o_acc[g, rws, :] for g, e in enumerate(es)])
        yb_ref[0, rws, :] = (y / tot).astype(yb_ref.dtype)
        return carry

    lax.fori_loop(0, seq // MERGE_ROWS, merge, 0)


def _attn_b(pb, table_flat):
    b, s, _ = pb.shape
    nh = B_HEADS_PER_GROUP
    groups, bkts, bias_scratch = [], [], []
    for gi, (window, dil) in enumerate(B_PATTERNS):
        sub, half = s // dil, window // (2 * dil)
        kw, _, offsets = _window_geometry(sub, half)
        bkt = _band_buckets(sub, half, dil)
        bkts.append(bkt)
        bias_scratch.append(pltpu.VMEM((nh, len(offsets), Q_BLOCK, kw), F32))
        groups.append(dict(dil=dil, sub=sub, half=half, chain_group=CHAIN_GROUP[gi],
                           head0=A_Q_HEADS + gi * nh, buckets_used=_buckets_used(bkt)))

    def qkv_spec(gi, part):
        base = (PB_COL_QB + part * B_W + gi * B_GROUP_W) // HEAD_DIM
        return pl.BlockSpec((1, s, HEAD_DIM), lambda i, h: (i, 0, base + h))

    ng = len(groups)
    return pl.pallas_call(
        functools.partial(_attn_b_kernel, seq=s, groups=groups),
        grid=(b, nh),
        in_specs=[pl.BlockSpec(bkt.shape, lambda i, h: (0, 0, 0)) for bkt in bkts] + [_smem_spec()]
        + [qkv_spec(gi, part) for gi in range(ng) for part in range(3)],
        out_specs=pl.BlockSpec((1, s, HEAD_DIM), lambda i, h: (i, 0, h)),
        out_shape=jax.ShapeDtypeStruct((b, s, B_GROUP_W), BF16),
        scratch_shapes=bias_scratch + [
            pltpu.VMEM((ng, s, HEAD_DIM), F32), pltpu.VMEM((ng, s, HEAD_DIM), F32),
            pltpu.VMEM((3, STAGE_STRIDE, s // STAGE_STRIDE, HEAD_DIM), F32),
            pltpu.VMEM((2, STAGE_STRIDE, s // STAGE_STRIDE, HEAD_DIM), F32)],
        compiler_params=_params("arbitrary", "arbitrary"),
        name="attn_b",
    )(*[jnp.asarray(bkt) for bkt in bkts], table_flat, *([pb] * (3 * ng)))


def _mix_kernel(ya_ref, yb_ref, ga_ref, gb_ref, x_ref, wa_ref, wb_ref, wo_ref, g_ref, x1_ref, hf_ref):
    ta = jnp.dot(ya_ref[...], wa_ref[...], preferred_element_type=F32)
    tb = jnp.dot(yb_ref[...], wb_ref[...], preferred_element_type=F32)
    merged = (jax.nn.sigmoid(ga_ref[...].astype(F32)) * ta
              + jax.nn.sigmoid(gb_ref[...].astype(F32)) * tb)
    x1 = x_ref[...] + jnp.dot(merged.astype(BF16), wo_ref[...], preferred_element_type=F32)
    x1_ref[...] = x1
    hf_ref[...] = _rmsnorm(x1, g_ref[...]).astype(BF16)


def _resident(shape):
    return pl.BlockSpec(shape, lambda i: (0,) * len(shape), pipeline_mode=pl.Buffered(1))


def _mix(ya, yb, pa2d, x2d, wa, wb, wo, gain, *, tm=256):
    m, d = x2d.shape
    assert PA_COL_GA % d == 0 and PA_COL_GB % d == 0

    def rows(width, col_block=0):
        return pl.BlockSpec((tm, width), lambda i: (i, col_block))

    return pl.pallas_call(
        _mix_kernel,
        grid=(m // tm,),
        in_specs=[rows(A_Q_W), rows(B_GROUP_W),
                  rows(d, PA_COL_GA // d), rows(d, PA_COL_GB // d), rows(d),
                  _resident(wa.shape), _resident(wb.shape), _resident(wo.shape), _resident(gain.shape)],
        out_specs=[rows(d), rows(d)],
        out_shape=[jax.ShapeDtypeStruct((m, d), F32), jax.ShapeDtypeStruct((m, d), BF16)],
        compiler_params=_params("parallel"),
        name="mix",
    )(ya, yb, pa2d, pa2d, x2d, wa, wb, wo, gain)


def _ffn_up_kernel(hf_ref, wg_ref, wu_ref, cw_ref, cb_ref, act_ref):
    hf = hf_ref[...]
    g = jnp.dot(hf, wg_ref[...], preferred_element_type=F32)
    u = jnp.dot(hf, wu_ref[...], preferred_element_type=F32)
    s = g.shape[0]
    row = lax.broadcasted_iota(jnp.int32, g.shape, 0)
    prev = jnp.where(row == 0, 0.0, pltpu.roll(g, 1, 0))
    nxt = jnp.where(row == s - 1, 0.0, pltpu.roll(g, s - 1, 0))
    cw = cw_ref[...]
    conv = prev * cw[0:1] + g * cw[1:2] + nxt * cw[2:3] + cb_ref[...]
    act_ref[...] = (jax.nn.gelu(conv) * u).astype(act_ref.dtype)


def _ffn_up(hf, wg, wu, cw, cb, *, seq, tf=512):
    m, d = hf.shape
    f = wg.shape[1]
    return pl.pallas_call(
        _ffn_up_kernel,
        grid=(m // seq, f // tf),
        in_specs=[
            pl.BlockSpec((seq, d), lambda i, j: (i, 0)),
            pl.BlockSpec((d, tf), lambda i, j: (0, j)),
            pl.BlockSpec((d, tf), lambda i, j: (0, j)),
            pl.BlockSpec((cw.shape[0], tf), lambda i, j: (0, j)),
            pl.BlockSpec((1, tf), lambda i, j: (0, j)),
        ],
        out_specs=pl.BlockSpec((seq, tf), lambda i, j: (i, j)),
        out_shape=jax.ShapeDtypeStruct((m, f), BF16),
        compiler_params=_params("parallel", "arbitrary"),
        name="ffn_up",
    )(hf, wg, wu, cw, cb)


def _ffn_down_kernel(act_ref, w_ref, x_ref, o_ref):
    o_ref[...] = x_ref[...] + jnp.dot(act_ref[...], w_ref[...], preferred_element_type=F32)


def _ffn_down(act, wd, x1, *, tm=512, tn=1024):
    m, f = act.shape
    n = wd.shape[1]
    return pl.pallas_call(
        _ffn_down_kernel,
        grid=(n // tn, m // tm),
        in_specs=[
            pl.BlockSpec((tm, f), lambda j, i: (i, 0)),
            pl.BlockSpec((f, tn), lambda j, i: (0, j)),
            pl.BlockSpec((tm, tn), lambda j, i: (i, j)),
        ],
        out_specs=pl.BlockSpec((tm, tn), lambda j, i: (i, j)),
        out_shape=jax.ShapeDtypeStruct((m, n), F32),
        compiler_params=_params("parallel", "parallel"),
        name="ffn_down",
    )(act, wd, x1)


def _ple_kernel(x_ref, p_ref, gp_ref, wg_ref, wp_ref, gf_ref, o_ref, *, final):
    x = x_ref[...]
    hp = _rmsnorm(x, gp_ref[...]).astype(BF16)
    gate = jax.nn.sigmoid(jnp.dot(hp, wg_ref[...], preferred_element_type=F32))
    emb = jnp.dot(p_ref[...].astype(BF16), wp_ref[...], preferred_element_type=F32)
    y = x + gate * emb
    o_ref[...] = _rmsnorm(y, gf_ref[...]) if final else y


def _ple(x2, p2d, gain_p, wpg, wpp, gain_f, *, final, tm=256):
    m, d = x2.shape
    pd = p2d.shape[1]
    return pl.pallas_call(
        functools.partial(_ple_kernel, final=final),
        grid=(m // tm,),
        in_specs=[
            pl.BlockSpec((tm, d), lambda i: (i, 0)),
            pl.BlockSpec((tm, pd), lambda i: (i, 0)),
            _resident(gain_p.shape), _resident(wpg.shape), _resident(wpp.shape),
            _resident(gain_f.shape),
        ],
        out_specs=pl.BlockSpec((tm, d), lambda i: (i, 0)),
        out_shape=jax.ShapeDtypeStruct((m, d), F32),
        compiler_params=_params("parallel"),
        name="ple",
    )(x2, p2d, gain_p, wpg, wpp, gain_f)


def kernel(x, p, rel_bias_table, attn_norm, w_in, sink_a, w_branch_a, w_branch_b, w_out,
           ffn_norm, w_ffn_gate, w_ffn_up, conv_w, conv_b, w_ffn_down,
           ple_norm, w_ple_gate, w_ple_proj, final_norm):
    b, s, d = x.shape
    depth = w_in.shape[0]
    assert d == D_MODEL and w_in.shape[2] == IN_PROJ_W and s % (Q_BLOCK * B_PATTERNS[-1][1]) == 0
    assert rel_bias_table.shape == (N_BUCKETS, N_BIAS_HEADS)
    m = b * s
    x2d = x.reshape(m, d)
    table_flat = rel_bias_table.reshape(-1)
    for i in range(depth):
        pb2d, pa2d = _inproj(x2d, attn_norm[i][None], w_in[i].astype(BF16))
        pb = pb2d.reshape(b, s, PB_W)
        ya = _attn_a(pb, table_flat, sink_a[i]).reshape(m, A_Q_W)
        yb = _attn_b(pb, table_flat).reshape(m, B_GROUP_W)
        x1, hf = _mix(ya, yb, pa2d, x2d, w_branch_a[i].astype(BF16), w_branch_b[i].astype(BF16),
                      w_out[i].astype(BF16), ffn_norm[i][None])
        act = _ffn_up(hf, w_ffn_gate[i].astype(BF16), w_ffn_up[i].astype(BF16), conv_w[i],
                      conv_b[i][None], seq=s)
        x2 = _ffn_down(act, w_ffn_down[i].astype(BF16), x1)
        x2d = _ple(x2, p[i].reshape(m, -1), ple_norm[i][None], w_ple_gate[i].astype(BF16),
                   w_ple_proj[i].astype(BF16), final_norm[None], final=i == depth - 1)
    return x2d.reshape(b, s, d)
```

```python
import functools
import math

import jax
import jax.numpy as jnp
import numpy as np
from jax import lax
from jax.experimental import pallas as pl
from jax.experimental.pallas import tpu as pltpu

D_MODEL = 2048
HEAD_DIM = 128
A_Q_HEADS = 8
A_KV_HEADS = 2
A_GROUP = A_Q_HEADS // A_KV_HEADS
A_HALF_WINDOW = 128
B_PATTERNS = ((128, 1), (512, 4), (2048, 16))
B_HEADS_PER_GROUP = 4
N_BUCKETS = 32
MAX_DISTANCE = 1024
N_BIAS_HEADS = A_Q_HEADS + len(B_PATTERNS) * B_HEADS_PER_GROUP
A_Q_W = A_Q_HEADS * HEAD_DIM
A_KV_W = A_KV_HEADS * HEAD_DIM
B_GROUP_W = B_HEADS_PER_GROUP * HEAD_DIM
B_W = len(B_PATTERNS) * B_GROUP_W
IN_PROJ_W = A_Q_W + 2 * A_KV_W + 3 * B_W + 2 * D_MODEL
RMS_EPS = 1e-6
NEG_INF = -1e30

PB_W = A_Q_W + 2 * A_KV_W + 3 * B_W
PB_COL_KA = A_Q_W
PB_COL_VA = PB_COL_KA + A_KV_W
PB_COL_QB = PB_COL_VA + A_KV_W
PA_W = 2 * D_MODEL
PA_COL_GA = 0
PA_COL_GB = D_MODEL

Q_BLOCK = 128
ATTN_UNROLL = 2
STAGE_STRIDE = 4
CHAIN_GROUP = (16, 1, 8)
A_BLOCKS_PER_GROUP = 2
MERGE_ROWS = 256

VMEM_LIMIT_BYTES = 56 * 1024 * 1024

BF16 = jnp.bfloat16
F32 = jnp.float32


def _params(*semantics):
    return pltpu.CompilerParams(dimension_semantics=semantics, vmem_limit_bytes=VMEM_LIMIT_BYTES)


def _rmsnorm(x, g):
    y = x * lax.rsqrt(jnp.mean(x * x, axis=-1, keepdims=True) + RMS_EPS)
    return y * g


def _inproj_kernel(x_ref, g_ref, w_ref, pb_ref, pa_ref, h_ref, *, row_chunk, n_b_tiles):
    j = pl.program_id(1)

    @pl.when(j == 0)
    def _():
        def body(c, _):
            rows = pl.ds(pl.multiple_of(c * row_chunk, row_chunk), row_chunk)
            h_ref[rows, :] = _rmsnorm(x_ref[rows, :], g_ref[...]).astype(BF16)
            return 0

        lax.fori_loop(0, x_ref.shape[0] // row_chunk, body, 0)

    @pl.when(j < n_b_tiles)
    def _():
        pb_ref[...] = jnp.dot(h_ref[...], w_ref[...], preferred_element_type=F32)

    @pl.when(j >= n_b_tiles)
    def _():
        pa_ref[...] = jnp.dot(h_ref[...], w_ref[...], preferred_element_type=F32).astype(BF16)


def _inproj(x2d, gain, w, *, tm=1024, tn=1024):
    m, k = x2d.shape
    assert PB_W % tn == 0 and PA_W % tn == 0 and w.shape[1] == PB_W + PA_W
    nb_t = PB_W // tn
    n_steps = w.shape[1] // tn

    return pl.pallas_call(
        functools.partial(_inproj_kernel, row_chunk=128, n_b_tiles=nb_t),
        grid=(m // tm, n_steps),
        in_specs=[
            pl.BlockSpec((tm, k), lambda i, j: (i, 0)),
            pl.BlockSpec((1, k), lambda i, j: (0, 0)),
            pl.BlockSpec((k, tn), lambda i, j: (0, j)),
        ],
        out_specs=[
            pl.BlockSpec((tm, tn), lambda i, j: (i, jnp.minimum(j, nb_t - 1))),
            pl.BlockSpec((tm, tn), lambda i, j: (i, jnp.maximum(j - nb_t, 0))),
        ],
        out_shape=[jax.ShapeDtypeStruct((m, PB_W), F32), jax.ShapeDtypeStruct((m, PA_W), BF16)],
        scratch_shapes=[pltpu.VMEM((tm, k), BF16)],
        compiler_params=_params("arbitrary", "arbitrary"),
        name="inproj",
    )(x2d, gain, w)


def _t5_bucket_static(rel):
    half = N_BUCKETS // 2
    max_exact = half // 2
    n = np.abs(rel)
    side = np.where(rel > 0, half, 0)
    nf = np.maximum(n, 1).astype(np.float32)
    large = max_exact + (np.log(nf / max_exact) / math.log(MAX_DISTANCE / max_exact)
                         * (half - max_exact)).astype(np.int32)
    large = np.minimum(large, half - 1)
    return side + np.where(n < max_exact, n, large)


def _window_geometry(seq_len, half_w):
    nb = seq_len // Q_BLOCK
    if nb == 1:
        return seq_len, nb, (0,)
    kw = Q_BLOCK + 2 * half_w
    assert half_w <= Q_BLOCK and kw <= seq_len
    return kw, nb, (0, -half_w, Q_BLOCK - kw)


def _band_buckets(seq_len, half_w, dilation):
    kw, _, offsets = _window_geometry(seq_len, half_w)
    qi = np.arange(Q_BLOCK)[:, None]
    kj = np.arange(kw)[None, :]
    rel = np.stack([off + kj - qi for off in offsets])
    buckets = _t5_bucket_static(rel * dilation)
    return np.where(np.abs(rel) <= half_w, buckets, -1).astype(np.int32)


def _block_geometry(i, seq_len, half_w):
    kw, nb, _ = _window_geometry(seq_len, half_w)
    if isinstance(i, int):
        qs = i * Q_BLOCK
        return qs, min(max(qs - half_w, 0), seq_len - kw), 0 if i == 0 else (2 if i == nb - 1 else 1)
    qs = pl.multiple_of(i * Q_BLOCK, Q_BLOCK)
    ks = pl.multiple_of(jnp.clip(qs - half_w, 0, seq_len - kw), half_w)
    return qs, ks, jnp.where(i == 0, 0, jnp.where(i == nb - 1, 2, 1))


def _build_bias(bkt_ref, table_ref, bias_ref, slot, first_head, n_heads, buckets_used):
    for kind in range(bkt_ref.shape[0]):
        bkt = bkt_ref[kind]
        for h in range(n_heads):
            acc = jnp.full(bkt.shape, NEG_INF, F32)
            for bucket in buckets_used:
                acc = jnp.where(bkt == bucket, table_ref[bucket * N_BIAS_HEADS + first_head + h], acc)
            bias_ref[slot, kind, h * Q_BLOCK:(h + 1) * Q_BLOCK, :] = acc


def _softmax_chains(items, *, load, store, bias, kw, sink=None, want_lse):
    scale = HEAD_DIM ** -0.5
    ones = jnp.ones((kw, HEAD_DIM), BF16)
    scores = []
    for tag, qs, ks, kind in items:
        s = lax.dot_general(load(0, tag, qs, Q_BLOCK), load(1, tag, ks, kw),
                            (((1,), (1,)), ((), ())), preferred_element_type=F32)
        scores.append(s * scale + bias(tag, kind))
    for (tag, qs, ks, _), s in zip(items, scores):
        v_ext = jnp.concatenate([load(2, tag, ks, kw), ones], axis=1)
        m = jnp.max(s, axis=-1, keepdims=True)
        if sink is not None:
            sk = sink(tag)
            m = jnp.maximum(m, sk)
        p = jnp.exp(s - m).astype(BF16)
        ov = jnp.dot(p, v_ext, preferred_element_type=F32)
        denom = ov[:, HEAD_DIM:]
        if sink is not None:
            denom = denom + jnp.exp(sk - m)
        store(0, tag, qs, ov[:, :HEAD_DIM] / denom)
        if want_lse:
            store(1, tag, qs, m + jnp.log(denom))


def _attn_a_kernel(bkt_ref, table_ref, sink_ref, q_ref, k_ref, v_ref, o_ref, bias_ref, *,
                   seq_len, buckets_used):
    kv = pl.program_id(1)
    kw, nb, _ = _window_geometry(seq_len, A_HALF_WINDOW)

    @pl.when(pl.program_id(0) == 0)
    def _():
        _build_bias(bkt_ref, table_ref, bias_ref, kv, kv * A_GROUP, A_GROUP, buckets_used)

    def load(t, h, start, size):
        if t == 0:
            return q_ref[0, pl.ds(start, size), h * HEAD_DIM:(h + 1) * HEAD_DIM].astype(BF16)
        return (k_ref, v_ref)[t - 1][0, pl.ds(start, size), :].astype(BF16)

    def store(t, h, start, value):
        o_ref[0, pl.ds(start, Q_BLOCK), h * HEAD_DIM:(h + 1) * HEAD_DIM] = value.astype(o_ref.dtype)

    def body(it, carry):
        items = [(h,) + _block_geometry(it * A_BLOCKS_PER_GROUP + di, seq_len, A_HALF_WINDOW)
                 for di in range(A_BLOCKS_PER_GROUP) for h in range(A_GROUP)]
        _softmax_chains(
            items, load=load, store=store, kw=kw, want_lse=False,
            bias=lambda h, kind: bias_ref[kv, kind, h * Q_BLOCK:(h + 1) * Q_BLOCK, :],
            sink=lambda h: jnp.full((Q_BLOCK, 1), sink_ref[kv * A_GROUP + h], F32))
        return carry

    assert nb % A_BLOCKS_PER_GROUP == 0
    lax.fori_loop(0, nb // A_BLOCKS_PER_GROUP, body, 0, unroll=ATTN_UNROLL)


def _buckets_used(bkt):
    return tuple(int(v) for v in np.unique(bkt) if v >= 0)


def _smem_spec():
    return pl.BlockSpec(memory_space=pltpu.SMEM)


def _attn_a(pb, table_flat, sink):
    b, s, _ = pb.shape
    gw = A_GROUP * HEAD_DIM
    kw, _, offsets = _window_geometry(s, A_HALF_WINDOW)
    bkt = _band_buckets(s, A_HALF_WINDOW, 1)
    kernel = functools.partial(_attn_a_kernel, seq_len=s, buckets_used=_buckets_used(bkt))
    return pl.pallas_call(
        kernel,
        grid=(b, A_KV_HEADS),
        in_specs=[
            pl.BlockSpec(bkt.shape, lambda i, j: (0, 0, 0)),
            _smem_spec(),
            _smem_spec(),
            pl.BlockSpec((1, s, gw), lambda i, j: (i, 0, j)),
            pl.BlockSpec((1, s, HEAD_DIM), lambda i, j: (i, 0, PB_COL_KA // HEAD_DIM + j)),
            pl.BlockSpec((1, s, HEAD_DIM), lambda i, j: (i, 0, PB_COL_VA // HEAD_DIM + j)),
        ],
        out_specs=pl.BlockSpec((1, s, gw), lambda i, j: (i, 0, j)),
        out_shape=jax.ShapeDtypeStruct((b, s, A_Q_W), BF16),
        scratch_shapes=[pltpu.VMEM((A_KV_HEADS, len(offsets), A_GROUP * Q_BLOCK, kw), F32)],
        compiler_params=_params("arbitrary", "arbitrary"),
        name="attn_a",
    )(jnp.asarray(bkt), table_flat, sink, pb, pb, pb)


def _attn_b_kernel(*refs, seq, groups):
    ng = len(groups)
    it = iter(refs)
    bkt_refs = [next(it) for _ in range(ng)]
    table_ref = next(it)
    qkv_refs = [[next(it) for _ in range(3)] for _ in range(ng)]
    yb_ref = next(it)
    bias_refs = [next(it) for _ in range(ng)]
    o_acc, l_acc, stage_in, stage_out = next(it), next(it), next(it), next(it)
    head = pl.program_id(1)

    @pl.when(pl.program_id(0) == 0)
    def _():
        for g, grp in enumerate(groups):
            _build_bias(bkt_refs[g], table_ref, bias_refs[g], head, grp["head0"] + head, 1,
                        grp["buckets_used"])

    for g, grp in enumerate(groups):
        dil, sub, half, chain_group = grp["dil"], grp["sub"], grp["half"], grp["chain_group"]
        kw, nb, _ = _window_geometry(sub, half)
        stage = STAGE_STRIDE if dil > STAGE_STRIDE else 1
        hop2 = dil // stage
        accs = (o_acc, l_acc)

        def rows(residue, start, size, dil=dil, stage=stage, hop2=hop2):
            if dil == 1:
                return pl.ds(start, size)
            if stage > 1:
                return pl.ds(residue // stage + start * hop2, size, stride=hop2)
            return pl.ds(residue + start * dil, size, stride=dil)

        def load(t, residue, start, size, g=g, stage=stage, rows=rows):
            if stage > 1:
                return stage_in[t, residue % stage, rows(residue, start, size), :].astype(BF16)
            return qkv_refs[g][t][0, rows(residue, start, size), :].astype(BF16)

        def store(t, residue, start, value, g=g, stage=stage, rows=rows, accs=accs):
            if stage > 1:
                stage_out[t, residue % stage, rows(residue, start, Q_BLOCK), :] = value
            else:
                accs[t][g, rows(residue, start, Q_BLOCK), :] = value

        def bias(residue, kind, g=g):
            return bias_refs[g][head, kind]

        run = functools.partial(_softmax_chains, load=load, store=store, bias=bias, kw=kw,
                                want_lse=True)
        if dil == 1:
            per_iter = min(chain_group, nb)
            assert nb % per_iter == 0

            def body(it_, carry, run=run, per_iter=per_iter, sub=sub, half=half):
                run([(0,) + _block_geometry(it_ * per_iter + di, sub, half) for di in range(per_iter)])
                return carry

            lax.fori_loop(0, nb // per_iter, body, 0)
            continue
        if stage > 1:
            for t in range(3):
                for r1 in range(stage):
                    stage_in[t, r1] = qkv_refs[g][t][0, pl.ds(r1, seq // stage, stride=stage), :]
        work = [(r,) + _block_geometry(i, sub, half) for r in range(dil) for i in range(nb)]
        for c in range(0, len(work), chain_group):
            run(work[c:c + chain_group])
        if stage > 1:
            for t in range(2):
                for r1 in range(stage):
                    accs[t][g, pl.ds(r1, seq // stage, stride=stage), :] = stage_out[t, r1]

    def merge(c, carry):
        rws = pl.ds(pl.multiple_of(c * MERGE_ROWS, MERGE_ROWS), MERGE_ROWS)
        ls = [l_acc[g, rws, :] for g in range(ng)]
        mx = functools.reduce(jnp.maximum, ls)
        es = [jnp.exp(l - mx) for l in ls]
        tot = functools.reduce(lambda a, e: a + e, es)
        y = functools.reduce(lambda a, e: a + e, [e * o_acc[g, rws, :] for g, e in enumerate(es)])
        yb_ref[0, rws, :] = (y / tot).astype(yb_ref.dtype)
        return carry

    lax.fori_loop(0, seq // MERGE_ROWS, merge, 0)


def _attn_b(pb, table_flat):
    b, s, _ = pb.shape
    nh = B_HEADS_PER_GROUP
    groups, bkts, bias_scratch = [], [], []
    for gi, (window, dil) in enumerate(B_PATTERNS):
        sub, half = s // dil, window // (2 * dil)
        kw, _, offsets = _window_geometry(sub, half)
        bkt = _band_buckets(sub, half, dil)
        bkts.append(bkt)
        bias_scratch.append(pltpu.VMEM((nh, len(offsets), Q_BLOCK, kw), F32))
        groups.append(dict(dil=dil, sub=sub, half=half, chain_group=CHAIN_GROUP[gi],
                           head0=A_Q_HEADS + gi * nh, buckets_used=_buckets_used(bkt)))

    def qkv_spec(gi, part):
        base = (PB_COL_QB + part * B_W + gi * B_GROUP_W) // HEAD_DIM
        return pl.BlockSpec((1, s, HEAD_DIM), lambda i, h: (i, 0, base + h))

    ng = len(groups)
    return pl.pallas_call(
        functools.partial(_attn_b_kernel, seq=s, groups=groups),
        grid=(b, nh),
        in_specs=[pl.BlockSpec(bkt.shape, lambda i, h: (0, 0, 0)) for bkt in bkts] + [_smem_spec()]
        + [qkv_spec(gi, part) for gi in range(ng) for part in range(3)],
        out_specs=pl.BlockSpec((1, s, HEAD_DIM), lambda i, h: (i, 0, h)),
        out_shape=jax.ShapeDtypeStruct((b, s, B_GROUP_W), BF16),
        scratch_shapes=bias_scratch + [
            pltpu.VMEM((ng, s, HEAD_DIM), F32), pltpu.VMEM((ng, s, HEAD_DIM), F32),
            pltpu.VMEM((3, STAGE_STRIDE, s // STAGE_STRIDE, HEAD_DIM), F32),
            pltpu.VMEM((2, STAGE_STRIDE, s // STAGE_STRIDE, HEAD_DIM), F32)],
        compiler_params=_params("arbitrary", "arbitrary"),
        name="attn_b",
    )(*[jnp.asarray(bkt) for bkt in bkts], table_flat, *([pb] * (3 * ng)))


def _mix_kernel(ya_ref, yb_ref, ga_ref, gb_ref, x_ref, wa_ref, wb_ref, wo_ref, g_ref, x1_ref, hf_ref):
    ta = jnp.dot(ya_ref[...], wa_ref[...], preferred_element_type=F32)
    tb = jnp.dot(yb_ref[...], wb_ref[...], preferred_element_type=F32)
    merged = (jax.nn.sigmoid(ga_ref[...].astype(F32)) * ta
              + jax.nn.sigmoid(gb_ref[...].astype(F32)) * tb)
    x1 = x_ref[...] + jnp.dot(merged.astype(BF16), wo_ref[...], preferred_element_type=F32)
    x1_ref[...] = x1
    hf_ref[...] = _rmsnorm(x1, g_ref[...]).astype(BF16)


def _resident(shape):
    return pl.BlockSpec(shape, lambda i: (0,) * len(shape), pipeline_mode=pl.Buffered(1))


def _mix(ya, yb, pa2d, x2d, wa, wb, wo, gain, *, tm=256):
    m, d = x2d.shape
    assert PA_COL_GA % d == 0 and PA_COL_GB % d == 0

    def rows(width, col_block=0):
        return pl.BlockSpec((tm, width), lambda i: (i, col_block))

    return pl.pallas_call(
        _mix_kernel,
        grid=(m // tm,),
        in_specs=[rows(A_Q_W), rows(B_GROUP_W),
                  rows(d, PA_COL_GA // d), rows(d, PA_COL_GB // d), rows(d),
                  _resident(wa.shape), _resident(wb.shape), _resident(wo.shape), _resident(gain.shape)],
        out_specs=[rows(d), rows(d)],
        out_shape=[jax.ShapeDtypeStruct((m, d), F32), jax.ShapeDtypeStruct((m, d), BF16)],
        compiler_params=_params("parallel"),
        name="mix",
    )(ya, yb, pa2d, pa2d, x2d, wa, wb, wo, gain)


def _ffn_up_kernel(hf_ref, wg_ref, wu_ref, cw_ref, cb_ref, act_ref):
    hf = hf_ref[...]
    g = jnp.dot(hf, wg_ref[...].astype(BF16), preferred_element_type=F32)
    u = jnp.dot(hf, wu_ref[...].astype(BF16), preferred_element_type=F32)
    s = g.shape[0]
    row = lax.broadcasted_iota(jnp.int32, g.shape, 0)
    prev = jnp.where(row == 0, 0.0, pltpu.roll(g, 1, 0))
    nxt = jnp.where(row == s - 1, 0.0, pltpu.roll(g, s - 1, 0))
    cw = cw_ref[...]
    conv = prev * cw[0:1] + g * cw[1:2] + nxt * cw[2:3] + cb_ref[...]
    act_ref[...] = (jax.nn.gelu(conv) * u).astype(act_ref.dtype)


def _ffn_up(hf, wg, wu, cw, cb, *, seq, tf=512):
    m, d = hf.shape
    f = wg.shape[1]
    return pl.pallas_call(
        _ffn_up_kernel,
        grid=(m // seq, f // tf),
        in_specs=[
            pl.BlockSpec((seq, d), lambda i, j: (i, 0)),
            pl.BlockSpec((d, tf), lambda i, j: (0, j)),
            pl.BlockSpec((d, tf), lambda i, j: (0, j)),
            pl.BlockSpec((cw.shape[0], tf), lambda i, j: (0, j)),
            pl.BlockSpec((1, tf), lambda i, j: (0, j)),
        ],
        out_specs=pl.BlockSpec((seq, tf), lambda i, j: (i, j)),
        out_shape=jax.ShapeDtypeStruct((m, f), BF16),
        compiler_params=_params("parallel", "arbitrary"),
        name="ffn_up",
    )(hf, wg, wu, cw, cb)


def _ffn_down_kernel(act_ref, w_ref, x_ref, o_ref):
    o_ref[...] = x_ref[...] + jnp.dot(act_ref[...], w_ref[...], preferred_element_type=F32)


def _ffn_down(act, wd, x1, *, tm=512, tn=1024):
    m, f = act.shape
    n = wd.shape[1]
    return pl.pallas_call(
        _ffn_down_kernel,
        grid=(n // tn, m // tm),
        in_specs=[
            pl.BlockSpec((tm, f), lambda j, i: (i, 0)),
            pl.BlockSpec((f, tn), lambda j, i: (0, j)),
            pl.BlockSpec((tm, tn), lambda j, i: (i, j)),
        ],
        out_specs=pl.BlockSpec((tm, tn), lambda j, i: (i, j)),
        out_shape=jax.ShapeDtypeStruct((m, n), F32),
        compiler_params=_params("parallel", "parallel"),
        name="ffn_down",
    )(act, wd, x1)


def _ple_kernel(x_ref, p_ref, gp_ref, wg_ref, wp_ref, gf_ref, o_ref, *, final):
    x = x_ref[...]
    hp = _rmsnorm(x, gp_ref[...]).astype(BF16)
    gate = jax.nn.sigmoid(jnp.dot(hp, wg_ref[...], preferred_element_type=F32))
    emb = jnp.dot(p_ref[...].astype(BF16), wp_ref[...], preferred_element_type=F32)
    y = x + gate * emb
    o_ref[...] = _rmsnorm(y, gf_ref[...]) if final else y


def _ple(x2, p2d, gain_p, wpg, wpp, gain_f, *, final, tm=256):
    m, d = x2.shape
    pd = p2d.shape[1]
    return pl.pallas_call(
        functools.partial(_ple_kernel, final=final),
        grid=(m // tm,),
        in_specs=[
            pl.BlockSpec((tm, d), lambda i: (i, 0)),
            pl.BlockSpec((tm, pd), lambda i: (i, 0)),
            _resident(gain_p.shape), _resident(wpg.shape), _resident(wpp.shape),
            _resident(gain_f.shape),
        ],
        out_specs=pl.BlockSpec((tm, d), lambda i: (i, 0)),
        out_shape=jax.ShapeDtypeStruct((m, d), F32),
        compiler_params=_params("parallel"),
        name="ple",
    )(x2, p2d, gain_p, wpg, wpp, gain_f)


def kernel(x, p, rel_bias_table, attn_norm, w_in, sink_a, w_branch_a, w_branch_b, w_out,
           ffn_norm, w_ffn_gate, w_ffn_up, conv_w, conv_b, w_ffn_down,
           ple_norm, w_ple_gate, w_ple_proj, final_norm):
    b, s, d = x.shape
    depth = w_in.shape[0]
    assert d == D_MODEL and w_in.shape[2] == IN_PROJ_W and s % (Q_BLOCK * B_PATTERNS[-1][1]) == 0
    assert rel_bias_table.shape == (N_BUCKETS, N_BIAS_HEADS)
    m = b * s
    x2d = x.reshape(m, d)
    table_flat = rel_bias_table.reshape(-1)
    for i in range(depth):
        pb2d, pa2d = _inproj(x2d, attn_norm[i][None], w_in[i].astype(BF16))
        pb = pb2d.reshape(b, s, PB_W)
        ya = _attn_a(pb, table_flat, sink_a[i]).reshape(m, A_Q_W)
        yb = _attn_b(pb, table_flat).reshape(m, B_GROUP_W)
        x1, hf = _mix(ya, yb, pa2d, x2d, w_branch_a[i].astype(BF16), w_branch_b[i].astype(BF16),
                      w_out[i].astype(BF16), ffn_norm[i][None])
        act = _ffn_up(hf, w_ffn_gate[i], w_ffn_up[i], conv_w[i], conv_b[i][None], seq=s)
        x2 = _ffn_down(act, w_ffn_down[i].astype(BF16), x1)
        x2d = _ple(x2, p[i].reshape(m, -1), ple_norm[i][None], w_ple_gate[i].astype(BF16),
                   w_ple_proj[i].astype(BF16), final_norm[None], final=i == depth - 1)
    return x2d.reshape(b, s, d)
```

```python
import functools
import math

import jax
import jax.numpy as jnp
import numpy as np
from jax import lax
from jax.experimental import pallas as pl
from jax.experimental.pallas import tpu as pltpu

D_MODEL = 2048
HEAD_DIM = 128
A_Q_HEADS = 8
A_KV_HEADS = 2
A_GROUP = A_Q_HEADS // A_KV_HEADS
A_HALF_WINDOW = 128
B_PATTERNS = ((128, 1), (512, 4), (2048, 16))
B_HEADS_PER_GROUP = 4
N_BUCKETS = 32
MAX_DISTANCE = 1024
N_BIAS_HEADS = A_Q_HEADS + len(B_PATTERNS) * B_HEADS_PER_GROUP
A_Q_W = A_Q_HEADS * HEAD_DIM
A_KV_W = A_KV_HEADS * HEAD_DIM
B_GROUP_W = B_HEADS_PER_GROUP * HEAD_DIM
B_W = len(B_PATTERNS) * B_GROUP_W
IN_PROJ_W = A_Q_W + 2 * A_KV_W + 3 * B_W + 2 * D_MODEL
RMS_EPS = 1e-6
NEG_INF = -1e30

PB_W = A_Q_W + 2 * A_KV_W + 3 * B_W
PB_COL_KA = A_Q_W
PB_COL_VA = PB_COL_KA + A_KV_W
PB_COL_QB = PB_COL_VA + A_KV_W
PA_W = 2 * D_MODEL
PA_COL_GA = 0
PA_COL_GB = D_MODEL

Q_BLOCK = 128
ATTN_UNROLL = 2
STAGE_STRIDE = 4
CHAIN_GROUP = (16, 1, 8)
A_BLOCKS_PER_GROUP = 2
MERGE_ROWS = 256

VMEM_LIMIT_BYTES = 56 * 1024 * 1024

BF16 = jnp.bfloat16
F32 = jnp.float32


def _params(*semantics):
    return pltpu.CompilerParams(dimension_semantics=semantics, vmem_limit_bytes=VMEM_LIMIT_BYTES)


def _rmsnorm(x, g):
    y = x * lax.rsqrt(jnp.mean(x * x, axis=-1, keepdims=True) + RMS_EPS)
    return y * g


def _inproj_kernel(x_ref, g_ref, w_ref, pb_ref, pa_ref, h_ref, *, row_chunk, n_b_tiles):
    j = pl.program_id(1)

    @pl.when(j == 0)
    def _():
        def body(c, _):
            rows = pl.ds(pl.multiple_of(c * row_chunk, row_chunk), row_chunk)
            h_ref[rows, :] = _rmsnorm(x_ref[rows, :], g_ref[...]).astype(BF16)
            return 0

        lax.fori_loop(0, x_ref.shape[0] // row_chunk, body, 0)

    @pl.when(j < n_b_tiles)
    def _():
        pb_ref[...] = jnp.dot(h_ref[...], w_ref[...], preferred_element_type=F32)

    @pl.when(j >= n_b_tiles)
    def _():
        pa_ref[...] = jnp.dot(h_ref[...], w_ref[...], preferred_element_type=F32).astype(BF16)


def _inproj(x2d, gain, w, *, tm=1024, tn=1024):
    m, k = x2d.shape
    assert PB_W % tn == 0 and PA_W % tn == 0 and w.shape[1] == PB_W + PA_W
    nb_t = PB_W // tn
    n_steps = w.shape[1] // tn

    return pl.pallas_call(
        functools.partial(_inproj_kernel, row_chunk=128, n_b_tiles=nb_t),
        grid=(m // tm, n_steps),
        in_specs=[
            pl.BlockSpec((tm, k), lambda i, j: (i, 0)),
            pl.BlockSpec((1, k), lambda i, j: (0, 0)),
            pl.BlockSpec((k, tn), lambda i, j: (0, j)),
        ],
        out_specs=[
            pl.BlockSpec((tm, tn), lambda i, j: (i, jnp.minimum(j, nb_t - 1))),
            pl.BlockSpec((tm, tn), lambda i, j: (i, jnp.maximum(j - nb_t, 0))),
        ],
        out_shape=[jax.ShapeDtypeStruct((m, PB_W), F32), jax.ShapeDtypeStruct((m, PA_W), BF16)],
        scratch_shapes=[pltpu.VMEM((tm, k), BF16)],
        compiler_params=_params("arbitrary", "arbitrary"),
        name="inproj",
    )(x2d, gain, w)


def _t5_bucket_static(rel):
    half = N_BUCKETS // 2
    max_exact = half // 2
    n = np.abs(rel)
    side = np.where(rel > 0, half, 0)
    nf = np.maximum(n, 1).astype(np.float32)
    large = max_exact + (np.log(nf / max_exact) / math.log(MAX_DISTANCE / max_exact)
                         * (half - max_exact)).astype(np.int32)
    large = np.minimum(large, half - 1)
    return side + np.where(n < max_exact, n, large)


def _window_geometry(seq_len, half_w):
    nb = seq_len // Q_BLOCK
    if nb == 1:
        return seq_len, nb, (0,)
    kw = Q_BLOCK + 2 * half_w
    assert half_w <= Q_BLOCK and kw <= seq_len
    return kw, nb, (0, -half_w, Q_BLOCK - kw)


def _band_buckets(seq_len, half_w, dilation):
    kw, _, offsets = _window_geometry(seq_len, half_w)
    qi = np.arange(Q_BLOCK)[:, None]
    kj = np.arange(kw)[None, :]
    rel = np.stack([off + kj - qi for off in offsets])
    buckets = _t5_bucket_static(rel * dilation)
    return np.where(np.abs(rel) <= half_w, buckets, -1).astype(np.int32)


def _block_geometry(i, seq_len, half_w):
    kw, nb, _ = _window_geometry(seq_len, half_w)
    if isinstance(i, int):
        qs = i * Q_BLOCK
        return qs, min(max(qs - half_w, 0), seq_len - kw), 0 if i == 0 else (2 if i == nb - 1 else 1)
    qs = pl.multiple_of(i * Q_BLOCK, Q_BLOCK)
    ks = pl.multiple_of(jnp.clip(qs - half_w, 0, seq_len - kw), half_w)
    return qs, ks, jnp.where(i == 0, 0, jnp.where(i == nb - 1, 2, 1))


def _build_bias(bkt_ref, table_ref, bias_ref, slot, first_head, n_heads, buckets_used):
    for kind in range(bkt_ref.shape[0]):
        bkt = bkt_ref[kind]
        for h in range(n_heads):
            acc = jnp.full(bkt.shape, NEG_INF, F32)
            for bucket in buckets_used:
                acc = jnp.where(bkt == bucket, table_ref[bucket * N_BIAS_HEADS + first_head + h], acc)
            bias_ref[slot, kind, h * Q_BLOCK:(h + 1) * Q_BLOCK, :] = acc


def _softmax_chains(items, *, load, store, bias, kw, sink=None, want_lse):
    scale = HEAD_DIM ** -0.5
    ones = jnp.ones((kw, HEAD_DIM), BF16)
    scores = []
    for tag, qs, ks, kind in items:
        s = lax.dot_general(load(0, tag, qs, Q_BLOCK), load(1, tag, ks, kw),
                            (((1,), (1,)), ((), ())), preferred_element_type=F32)
        scores.append(s * scale + bias(tag, kind))
    for (tag, qs, ks, _), s in zip(items, scores):
        v_ext = jnp.concatenate([load(2, tag, ks, kw), ones], axis=1)
        m = jnp.max(s, axis=-1, keepdims=True)
        if sink is not None:
            sk = sink(tag)
            m = jnp.maximum(m, sk)
        p = jnp.exp(s - m).astype(BF16)
        ov = jnp.dot(p, v_ext, preferred_element_type=F32)
        denom = ov[:, HEAD_DIM:]
        if sink is not None:
            denom = denom + jnp.exp(sk - m)
        store(0, tag, qs, ov[:, :HEAD_DIM] / denom)
        if want_lse:
            store(1, tag, qs, m + jnp.log(denom))


def _attn_a_kernel(bkt_ref, table_ref, sink_ref, q_ref, k_ref, v_ref, o_ref, bias_ref, *,
                   seq_len, buckets_used):
    kv = pl.program_id(1)
    kw, nb, _ = _window_geometry(seq_len, A_HALF_WINDOW)

    @pl.when(pl.program_id(0) == 0)
    def _():
        _build_bias(bkt_ref, table_ref, bias_ref, kv, kv * A_GROUP, A_GROUP, buckets_used)

    def load(t, h, start, size):
        if t == 0:
            return q_ref[0, pl.ds(start, size), h * HEAD_DIM:(h + 1) * HEAD_DIM].astype(BF16)
        return (k_ref, v_ref)[t - 1][0, pl.ds(start, size), :].astype(BF16)

    def store(t, h, start, value):
        o_ref[0, pl.ds(start, Q_BLOCK), h * HEAD_DIM:(h + 1) * HEAD_DIM] = value.astype(o_ref.dtype)

    def body(it, carry):
        items = [(h,) + _block_geometry(it * A_BLOCKS_PER_GROUP + di, seq_len, A_HALF_WINDOW)
                 for di in range(A_BLOCKS_PER_GROUP) for h in range(A_GROUP)]
        _softmax_chains(
            items, load=load, store=store, kw=kw, want_lse=False,
            bias=lambda h, kind: bias_ref[kv, kind, h * Q_BLOCK:(h + 1) * Q_BLOCK, :],
            sink=lambda h: jnp.full((Q_BLOCK, 1), sink_ref[kv * A_GROUP + h], F32))
        return carry

    assert nb % A_BLOCKS_PER_GROUP == 0
    lax.fori_loop(0, nb // A_BLOCKS_PER_GROUP, body, 0, unroll=ATTN_UNROLL)


def _buckets_used(bkt):
    return tuple(int(v) for v in np.unique(bkt) if v >= 0)


def _smem_spec():
    return pl.BlockSpec(memory_space=pltpu.SMEM)


def _attn_a(pb, table_flat, sink):
    b, s, _ = pb.shape
    gw = A_GROUP * HEAD_DIM
    kw, _, offsets = _window_geometry(s, A_HALF_WINDOW)
    bkt = _band_buckets(s, A_HALF_WINDOW, 1)
    kernel = functools.partial(_attn_a_kernel, seq_len=s, buckets_used=_buckets_used(bkt))
    return pl.pallas_call(
        kernel,
        grid=(b, A_KV_HEADS),
        in_specs=[
            pl.BlockSpec(bkt.shape, lambda i, j: (0, 0, 0)),
            _smem_spec(),
            _smem_spec(),
            pl.BlockSpec((1, s, gw), lambda i, j: (i, 0, j)),
            pl.BlockSpec((1, s, HEAD_DIM), lambda i, j: (i, 0, PB_COL_KA // HEAD_DIM + j)),
            pl.BlockSpec((1, s, HEAD_DIM), lambda i, j: (i, 0, PB_COL_VA // HEAD_DIM + j)),
        ],
        out_specs=pl.BlockSpec((1, s, gw), lambda i, j: (i, 0, j)),
        out_shape=jax.ShapeDtypeStruct((b, s, A_Q_W), BF16),
        scratch_shapes=[pltpu.VMEM((A_KV_HEADS, len(offsets), A_GROUP * Q_BLOCK, kw), F32)],
        compiler_params=_params("arbitrary", "arbitrary"),
        name="attn_a",
    )(jnp.asarray(bkt), table_flat, sink, pb, pb, pb)


def _attn_b_kernel(*refs, seq, groups):
    ng = len(groups)
    it = iter(refs)
    bkt_refs = [next(it) for _ in range(ng)]
    table_ref = next(it)
    qkv_refs = [[next(it) for _ in range(3)] for _ in range(ng)]
    yb_ref = next(it)
    bias_refs = [next(it) for _ in range(ng)]
    o_acc, l_acc, stage_in, stage_out = next(it), next(it), next(it), next(it)
    head = pl.program_id(1)

    @pl.when(pl.program_id(0) == 0)
    def _():
        for g, grp in enumerate(groups):
            _build_bias(bkt_refs[g], table_ref, bias_refs[g], head, grp["head0"] + head, 1,
                        grp["buckets_used"])

    for g, grp in enumerate(groups):
        dil, sub, half, chain_group = grp["dil"], grp["sub"], grp["half"], grp["chain_group"]
        kw, nb, _ = _window_geometry(sub, half)
        stage = STAGE_STRIDE if dil > STAGE_STRIDE else 1
        hop2 = dil // stage
        accs = (o_acc, l_acc)

        def rows(residue, start, size, dil=dil, stage=stage, hop2=hop2):
            if dil == 1:
                return pl.ds(start, size)
            if stage > 1:
                return pl.ds(residue // stage + start * hop2, size, stride=hop2)
            return pl.ds(residue + start * dil, size, stride=dil)

        def load(t, residue, start, size, g=g, stage=stage, rows=rows):
            if stage > 1:
                return stage_in[t, residue % stage, rows(residue, start, size), :].astype(BF16)
            return qkv_refs[g][t][0, rows(residue, start, size), :].astype(BF16)

        def store(t, residue, start, value, g=g, stage=stage, rows=rows, accs=accs):
            if stage > 1:
                stage_out[t, residue % stage, rows(residue, start, Q_BLOCK), :] = value
            else:
                accs[t][g, rows(residue, start, Q_BLOCK), :] = value

        def bias(residue, kind, g=g):
            return bias_refs[g][head, kind]

        run = functools.partial(_softmax_chains, load=load, store=store, bias=bias, kw=kw,
                                want_lse=True)
        if dil == 1:
            per_iter = min(chain_group, nb)
            assert nb % per_iter == 0

            def body(it_, carry, run=run, per_iter=per_iter, sub=sub, half=half):
                run([(0,) + _block_geometry(it_ * per_iter + di, sub, half) for di in range(per_iter)])
                return carry

            lax.fori_loop(0, nb // per_iter, body, 0)
            continue
        if stage > 1:
            for t in range(3):
                for r1 in range(stage):
                    stage_in[t, r1] = qkv_refs[g][t][0, pl.ds(r1, seq // stage, stride=stage), :]
        work = [(r,) + _block_geometry(i, sub, half) for r in range(dil) for i in range(nb)]
        for c in range(0, len(work), chain_group):
            run(work[c:c + chain_group])
        if stage > 1:
            for t in range(2):
                for r1 in range(stage):
                    accs[t][g, pl.ds(r1, seq // stage, stride=stage), :] = stage_out[t, r1]

    def merge(c, carry):
        rws = pl.ds(pl.multiple_of(c * MERGE_ROWS, MERGE_ROWS), MERGE_ROWS)
        ls = [l_acc[g, rws, :] for g in range(ng)]
        mx = functools.reduce(jnp.maximum, ls)
        es = [jnp.exp(l - mx) for l in ls]
        tot = functools.reduce(lambda a, e: a + e, es)
        y = functools.reduce(lambda a, e: a + e, [e * o_acc[g, rws, :] for g, e in enumerate(es)])
        yb_ref[0, rws, :] = (y / tot).astype(yb_ref.dtype)
        return carry

    lax.fori_loop(0, seq // MERGE_ROWS, merge, 0)


def _attn_b(pb, table_flat):
    b, s, _ = pb.shape
    nh = B_HEADS_PER_GROUP
    groups, bkts, bias_scratch = [], [], []
    for gi, (window, dil) in enumerate(B_PATTERNS):
        sub, half = s // dil, window // (2 * dil)
        kw, _, offsets = _window_geometry(sub, half)
        bkt = _band_buckets(sub, half, dil)
        bkts.append(bkt)
        bias_scratch.append(pltpu.VMEM((nh, len(offsets), Q_BLOCK, kw), F32))
        groups.append(dict(dil=dil, sub=sub, half=half, chain_group=CHAIN_GROUP[gi],
                           head0=A_Q_HEADS + gi * nh, buckets_used=_buckets_used(bkt)))

    def qkv_spec(gi, part):
        base = (PB_COL_QB + part * B_W + gi * B_GROUP_W) // HEAD_DIM
        return pl.BlockSpec((1, s, HEAD_DIM), lambda i, h: (i, 0, base + h))

    ng = len(groups)
    return pl.pallas_call(
        functools.partial(_attn_b_kernel, seq=s, groups=groups),
        grid=(b, nh),
        in_specs=[pl.BlockSpec(bkt.shape, lambda i, h: (0, 0, 0)) for bkt in bkts] + [_smem_spec()]
        + [qkv_spec(gi, part) for gi in range(ng) for part in range(3)],
        out_specs=pl.BlockSpec((1, s, HEAD_DIM), lambda i, h: (i, 0, h)),
        out_shape=jax.ShapeDtypeStruct((b, s, B_GROUP_W), BF16),
        scratch_shapes=bias_scratch + [
            pltpu.VMEM((ng, s, HEAD_DIM), F32), pltpu.VMEM((ng, s, HEAD_DIM), F32),
            pltpu.VMEM((3, STAGE_STRIDE, s // STAGE_STRIDE, HEAD_DIM), F32),
            pltpu.VMEM((2, STAGE_STRIDE, s // STAGE_STRIDE, HEAD_DIM), F32)],
        compiler_params=_params("arbitrary", "arbitrary"),
        name="attn_b",
    )(*[jnp.asarray(bkt) for bkt in bkts], table_flat, *([pb] * (3 * ng)))


def _mix_kernel(ya_ref, yb_ref, ga_ref, gb_ref, x_ref, wa_ref, wb_ref, wo_ref, g_ref, x1_ref, hf_ref):
    ta = jnp.dot(ya_ref[...], wa_ref[...], preferred_element_type=F32)
    tb = jnp.dot(yb_ref[...], wb_ref[...], preferred_element_type=F32)
    merged = (jax.nn.sigmoid(ga_ref[...].astype(F32)) * ta
              + jax.nn.sigmoid(gb_ref[...].astype(F32)) * tb)
    x1 = x_ref[...] + jnp.dot(merged.astype(BF16), wo_ref[...], preferred_element_type=F32)
    x1_ref[...] = x1
    hf_ref[...] = _rmsnorm(x1, g_ref[...]).astype(BF16)


def _resident(shape):
    return pl.BlockSpec(shape, lambda i: (0,) * len(shape), pipeline_mode=pl.Buffered(1))


def _mix(ya, yb, pa2d, x2d, wa, wb, wo, gain, *, tm=512):
    m, d = x2d.shape
    assert PA_COL_GA % d == 0 and PA_COL_GB % d == 0

    def rows(width, col_block=0):
        return pl.BlockSpec((tm, width), lambda i: (i, col_block))

    return pl.pallas_call(
        _mix_kernel,
        grid=(m // tm,),
        in_specs=[rows(A_Q_W), rows(B_GROUP_W),
                  rows(d, PA_COL_GA // d), rows(d, PA_COL_GB // d), rows(d),
                  _resident(wa.shape), _resident(wb.shape), _resident(wo.shape), _resident(gain.shape)],
        out_specs=[rows(d), rows(d)],
        out_shape=[jax.ShapeDtypeStruct((m, d), F32), jax.ShapeDtypeStruct((m, d), BF16)],
        compiler_params=_params("parallel"),
        name="mix",
    )(ya, yb, pa2d, pa2d, x2d, wa, wb, wo, gain)


def _ffn_up_kernel(hf_ref, wg_ref, wu_ref, cw_ref, cb_ref, act_ref):
    hf = hf_ref[...]
    g = jnp.dot(hf, wg_ref[...].astype(BF16), preferred_element_type=F32)
    u = jnp.dot(hf, wu_ref[...].astype(BF16), preferred_element_type=F32)
    s = g.shape[0]
    row = lax.broadcasted_iota(jnp.int32, g.shape, 0)
    prev = jnp.where(row == 0, 0.0, pltpu.roll(g, 1, 0))
    nxt = jnp.where(row == s - 1, 0.0, pltpu.roll(g, s - 1, 0))
    cw = cw_ref[...]
    conv = prev * cw[0:1] + g * cw[1:2] + nxt * cw[2:3] + cb_ref[...]
    act_ref[...] = (jax.nn.gelu(conv) * u).astype(act_ref.dtype)


def _ffn_up(hf, wg, wu, cw, cb, *, seq, tf=512):
    m, d = hf.shape
    f = wg.shape[1]
    return pl.pallas_call(
        _ffn_up_kernel,
        grid=(m // seq, f // tf),
        in_specs=[
            pl.BlockSpec((seq, d), lambda i, j: (i, 0)),
            pl.BlockSpec((d, tf), lambda i, j: (0, j)),
            pl.BlockSpec((d, tf), lambda i, j: (0, j)),
            pl.BlockSpec((cw.shape[0], tf), lambda i, j: (0, j)),
            pl.BlockSpec((1, tf), lambda i, j: (0, j)),
        ],
        out_specs=pl.BlockSpec((seq, tf), lambda i, j: (i, j)),
        out_shape=jax.ShapeDtypeStruct((m, f), BF16),
        compiler_params=_params("parallel", "arbitrary"),
        name="ffn_up",
    )(hf, wg, wu, cw, cb)


def _ffn_down_kernel(act_ref, w_ref, x_ref, o_ref):
    o_ref[...] = x_ref[...] + jnp.dot(act_ref[...], w_ref[...], preferred_element_type=F32)


def _ffn_down(act, wd, x1, *, tm=512, tn=1024):
    m, f = act.shape
    n = wd.shape[1]
    return pl.pallas_call(
        _ffn_down_kernel,
        grid=(n // tn, m // tm),
        in_specs=[
            pl.BlockSpec((tm, f), lambda j, i: (i, 0)),
            pl.BlockSpec((f, tn), lambda j, i: (0, j)),
            pl.BlockSpec((tm, tn), lambda j, i: (i, j)),
        ],
        out_specs=pl.BlockSpec((tm, tn), lambda j, i: (i, j)),
        out_shape=jax.ShapeDtypeStruct((m, n), F32),
        compiler_params=_params("parallel", "parallel"),
        name="ffn_down",
    )(act, wd, x1)


def _ple_kernel(x_ref, p_ref, gp_ref, wg_ref, wp_ref, gf_ref, o_ref, *, final):
    x = x_ref[...]
    hp = _rmsnorm(x, gp_ref[...]).astype(BF16)
    gate = jax.nn.sigmoid(jnp.dot(hp, wg_ref[...], preferred_element_type=F32))
    emb = jnp.dot(p_ref[...].astype(BF16), wp_ref[...], preferred_element_type=F32)
    y = x + gate * emb
    o_ref[...] = _rmsnorm(y, gf_ref[...]) if final else y


def _ple(x2, p2d, gain_p, wpg, wpp, gain_f, *, final, tm=512):
    m, d = x2.shape
    pd = p2d.shape[1]
    return pl.pallas_call(
        functools.partial(_ple_kernel, final=final),
        grid=(m // tm,),
        in_specs=[
            pl.BlockSpec((tm, d), lambda i: (i, 0)),
            pl.BlockSpec((tm, pd), lambda i: (i, 0)),
            _resident(gain_p.shape), _resident(wpg.shape), _resident(wpp.shape),
            _resident(gain_f.shape),
        ],
        out_specs=pl.BlockSpec((tm, d), lambda i: (i, 0)),
        out_shape=jax.ShapeDtypeStruct((m, d), F32),
        compiler_params=_params("parallel"),
        name="ple",
    )(x2, p2d, gain_p, wpg, wpp, gain_f)


def kernel(x, p, rel_bias_table, attn_norm, w_in, sink_a, w_branch_a, w_branch_b, w_out,
           ffn_norm, w_ffn_gate, w_ffn_up, conv_w, conv_b, w_ffn_down,
           ple_norm, w_ple_gate, w_ple_proj, final_norm):
    b, s, d = x.shape
    depth = w_in.shape[0]
    assert d == D_MODEL and w_in.shape[2] == IN_PROJ_W and s % (Q_BLOCK * B_PATTERNS[-1][1]) == 0
    assert rel_bias_table.shape == (N_BUCKETS, N_BIAS_HEADS)
    m = b * s
    x2d = x.reshape(m, d)
    table_flat = rel_bias_table.reshape(-1)
    for i in range(depth):
        pb2d, pa2d = _inproj(x2d, attn_norm[i][None], w_in[i].astype(BF16))
        pb = pb2d.reshape(b, s, PB_W)
        ya = _attn_a(pb, table_flat, sink_a[i]).reshape(m, A_Q_W)
        yb = _attn_b(pb, table_flat).reshape(m, B_GROUP_W)
        x1, hf = _mix(ya, yb, pa2d, x2d, w_branch_a[i].astype(BF16), w_branch_b[i].astype(BF16),
                      w_out[i].astype(BF16), ffn_norm[i][None])
        act = _ffn_up(hf, w_ffn_gate[i], w_ffn_up[i], conv_w[i], conv_b[i][None], seq=s)
        x2 = _ffn_down(act, w_ffn_down[i].astype(BF16), x1)
        x2d = _ple(x2, p[i].reshape(m, -1), ple_norm[i][None], w_ple_gate[i].astype(BF16),
                   w_ple_proj[i].astype(BF16), final_norm[None], final=i == depth - 1)
    return x2d.reshape(b, s, d)
```

```python
import functools
import math

import jax
import jax.numpy as jnp
import numpy as np
from jax import lax
from jax.experimental import pallas as pl
from jax.experimental.pallas import tpu as pltpu

D_MODEL = 2048
HEAD_DIM = 128
A_Q_HEADS = 8
A_KV_HEADS = 2
A_GROUP = A_Q_HEADS // A_KV_HEADS
A_HALF_WINDOW = 128
B_PATTERNS = ((128, 1), (512, 4), (2048, 16))
B_HEADS_PER_GROUP = 4
N_BUCKETS = 32
MAX_DISTANCE = 1024
N_BIAS_HEADS = A_Q_HEADS + len(B_PATTERNS) * B_HEADS_PER_GROUP
A_Q_W = A_Q_HEADS * HEAD_DIM
A_KV_W = A_KV_HEADS * HEAD_DIM
B_GROUP_W = B_HEADS_PER_GROUP * HEAD_DIM
B_W = len(B_PATTERNS) * B_GROUP_W
IN_PROJ_W = A_Q_W + 2 * A_KV_W + 3 * B_W + 2 * D_MODEL
RMS_EPS = 1e-6
NEG_INF = -1e30
LOG2_E = math.log2(math.e)
QK_SCALE_LOG2 = HEAD_DIM ** -0.5 * LOG2_E

PB_W = A_Q_W + 2 * A_KV_W + 3 * B_W
PB_COL_KA = A_Q_W
PB_COL_VA = PB_COL_KA + A_KV_W
PB_COL_QB = PB_COL_VA + A_KV_W
PA_W = 2 * D_MODEL
PA_COL_GA = 0
PA_COL_GB = D_MODEL

Q_BLOCK = 128
ATTN_UNROLL = 2
STAGE_STRIDE = 4
CHAIN_GROUP = (16, 1, 16)
A_BLOCKS_PER_GROUP = 2
MERGE_ROWS = 256

VMEM_LIMIT_BYTES = 56 * 1024 * 1024

BF16 = jnp.bfloat16
F32 = jnp.float32


def _params(*semantics):
    return pltpu.CompilerParams(dimension_semantics=semantics, vmem_limit_bytes=VMEM_LIMIT_BYTES)


def _rmsnorm(x, g):
    y = x * lax.rsqrt(jnp.mean(x * x, axis=-1, keepdims=True) + RMS_EPS)
    return y * g


def _inproj_kernel(x_ref, g_ref, w_ref, pb_ref, pa_ref, h_ref, *, row_chunk, n_b_tiles):
    j = pl.program_id(1)

    @pl.when(j == 0)
    def _():
        def body(c, _):
            rows = pl.ds(pl.multiple_of(c * row_chunk, row_chunk), row_chunk)
            h_ref[rows, :] = _rmsnorm(x_ref[rows, :], g_ref[...]).astype(BF16)
            return 0

        lax.fori_loop(0, x_ref.shape[0] // row_chunk, body, 0)

    @pl.when(j < n_b_tiles)
    def _():
        pb_ref[...] = jnp.dot(h_ref[...], w_ref[...], preferred_element_type=F32)

    @pl.when(j >= n_b_tiles)
    def _():
        pa_ref[...] = jnp.dot(h_ref[...], w_ref[...], preferred_element_type=F32).astype(BF16)


def _inproj(x2d, gain, w, *, tm=1024, tn=1024):
    m, k = x2d.shape
    assert PB_W % tn == 0 and PA_W % tn == 0 and w.shape[1] == PB_W + PA_W
    nb_t = PB_W // tn
    n_steps = w.shape[1] // tn

    return pl.pallas_call(
        functools.partial(_inproj_kernel, row_chunk=128, n_b_tiles=nb_t),
        grid=(m // tm, n_steps),
        in_specs=[
            pl.BlockSpec((tm, k), lambda i, j: (i, 0)),
            pl.BlockSpec((1, k), lambda i, j: (0, 0)),
            pl.BlockSpec((k, tn), lambda i, j: (0, j)),
        ],
        out_specs=[
            pl.BlockSpec((tm, tn), lambda i, j: (i, jnp.minimum(j, nb_t - 1))),
            pl.BlockSpec((tm, tn), lambda i, j: (i, jnp.maximum(j - nb_t, 0))),
        ],
        out_shape=[jax.ShapeDtypeStruct((m, PB_W), F32), jax.ShapeDtypeStruct((m, PA_W), BF16)],
        scratch_shapes=[pltpu.VMEM((tm, k), BF16)],
        compiler_params=_params("arbitrary", "arbitrary"),
        name="inproj",
    )(x2d, gain, w)


def _t5_bucket_static(rel):
    half = N_BUCKETS // 2
    max_exact = half // 2
    n = np.abs(rel)
    side = np.where(rel > 0, half, 0)
    nf = np.maximum(n, 1).astype(np.float32)
    large = max_exact + (np.log(nf / max_exact) / math.log(MAX_DISTANCE / max_exact)
                         * (half - max_exact)).astype(np.int32)
    large = np.minimum(large, half - 1)
    return side + np.where(n < max_exact, n, large)


def _window_geometry(seq_len, half_w):
    nb = seq_len // Q_BLOCK
    if nb == 1:
        return seq_len, nb, (0,)
    kw = Q_BLOCK + 2 * half_w
    assert half_w <= Q_BLOCK and kw <= seq_len
    return kw, nb, (0, -half_w, Q_BLOCK - kw)


def _band_buckets(seq_len, half_w, dilation):
    kw, _, offsets = _window_geometry(seq_len, half_w)
    qi = np.arange(Q_BLOCK)[:, None]
    kj = np.arange(kw)[None, :]
    rel = np.stack([off + kj - qi for off in offsets])
    buckets = _t5_bucket_static(rel * dilation)
    return np.where(np.abs(rel) <= half_w, buckets, -1).astype(np.int32)


def _block_geometry(i, seq_len, half_w):
    kw, nb, _ = _window_geometry(seq_len, half_w)
    if isinstance(i, int):
        qs = i * Q_BLOCK
        return qs, min(max(qs - half_w, 0), seq_len - kw), 0 if i == 0 else (2 if i == nb - 1 else 1)
    qs = pl.multiple_of(i * Q_BLOCK, Q_BLOCK)
    ks = pl.multiple_of(jnp.clip(qs - half_w, 0, seq_len - kw), half_w)
    return qs, ks, jnp.where(i == 0, 0, jnp.where(i == nb - 1, 2, 1))


def _build_bias(bkt_ref, table_ref, bias_ref, slot, first_head, n_heads, buckets_used):
    for kind in range(bkt_ref.shape[0]):
        bkt = bkt_ref[kind]
        for h in range(n_heads):
            acc = jnp.full(bkt.shape, NEG_INF, F32)
            for bucket in buckets_used:
                acc = jnp.where(bkt == bucket,
                                table_ref[bucket * N_BIAS_HEADS + first_head + h] * LOG2_E, acc)
            bias_ref[slot, kind, h * Q_BLOCK:(h + 1) * Q_BLOCK, :] = acc


def _softmax_chains(items, *, load, store, bias, kw, sink=None, want_lse):
    ones = jnp.ones((kw, HEAD_DIM), BF16)
    scores = []
    for tag, qs, ks, kind in items:
        q = (load(0, tag, qs, Q_BLOCK) * QK_SCALE_LOG2).astype(BF16)
        s = lax.dot_general(q, load(1, tag, ks, kw).astype(BF16),
                            (((1,), (1,)), ((), ())), preferred_element_type=F32)
        scores.append(s + bias(tag, kind))
    for (tag, qs, ks, _), s in zip(items, scores):
        v_ext = jnp.concatenate([load(2, tag, ks, kw).astype(BF16), ones], axis=1)
        m = jnp.max(s, axis=-1, keepdims=True)
        if sink is not None:
            sk = sink(tag) * LOG2_E
            m = jnp.maximum(m, sk)
        p = jnp.exp2(s - m).astype(BF16)
        ov = jnp.dot(p, v_ext, preferred_element_type=F32)
        denom = ov[:, HEAD_DIM:]
        if sink is not None:
            denom = denom + jnp.exp2(sk - m)
        store(0, tag, qs, ov[:, :HEAD_DIM] / denom)
        if want_lse:
            store(1, tag, qs, m * (1.0 / LOG2_E) + jnp.log(denom))


def _attn_a_kernel(bkt_ref, table_ref, sink_ref, q_ref, k_ref, v_ref, o_ref, bias_ref, *,
                   seq_len, buckets_used):
    kv = pl.program_id(1)
    kw, nb, _ = _window_geometry(seq_len, A_HALF_WINDOW)

    @pl.when(pl.program_id(0) == 0)
    def _():
        _build_bias(bkt_ref, table_ref, bias_ref, kv, kv * A_GROUP, A_GROUP, buckets_used)

    def load(t, h, start, size):
        if t == 0:
            return q_ref[0, pl.ds(start, size), h * HEAD_DIM:(h + 1) * HEAD_DIM]
        return (k_ref, v_ref)[t - 1][0, pl.ds(start, size), :]

    def store(t, h, start, value):
        o_ref[0, pl.ds(start, Q_BLOCK), h * HEAD_DIM:(h + 1) * HEAD_DIM] = value.astype(o_ref.dtype)

    def body(it, carry):
        items = [(h,) + _block_geometry(it * A_BLOCKS_PER_GROUP + di, seq_len, A_HALF_WINDOW)
                 for di in range(A_BLOCKS_PER_GROUP) for h in range(A_GROUP)]
        _softmax_chains(
            items, load=load, store=store, kw=kw, want_lse=False,
            bias=lambda h, kind: bias_ref[kv, kind, h * Q_BLOCK:(h + 1) * Q_BLOCK, :],
            sink=lambda h: jnp.full((Q_BLOCK, 1), sink_ref[kv * A_GROUP + h], F32))
        return carry

    assert nb % A_BLOCKS_PER_GROUP == 0
    lax.fori_loop(0, nb // A_BLOCKS_PER_GROUP, body, 0, unroll=ATTN_UNROLL)


def _buckets_used(bkt):
    return tuple(int(v) for v in np.unique(bkt) if v >= 0)


def _smem_spec():
    return pl.BlockSpec(memory_space=pltpu.SMEM)


def _attn_a(pb, table_flat, sink):
    b, s, _ = pb.shape
    gw = A_GROUP * HEAD_DIM
    kw, _, offsets = _window_geometry(s, A_HALF_WINDOW)
    bkt = _band_buckets(s, A_HALF_WINDOW, 1)
    kernel = functools.partial(_attn_a_kernel, seq_len=s, buckets_used=_buckets_used(bkt))
    return pl.pallas_call(
        kernel,
        grid=(b, A_KV_HEADS),
        in_specs=[
            pl.BlockSpec(bkt.shape, lambda i, j: (0, 0, 0)),
            _smem_spec(),
            _smem_spec(),
            pl.BlockSpec((1, s, gw), lambda i, j: (i, 0, j)),
            pl.BlockSpec((1, s, HEAD_DIM), lambda i, j: (i, 0, PB_COL_KA // HEAD_DIM + j)),
            pl.BlockSpec((1, s, HEAD_DIM), lambda i, j: (i, 0, PB_COL_VA // HEAD_DIM + j)),
        ],
        out_specs=pl.BlockSpec((1, s, gw), lambda i, j: (i, 0, j)),
        out_shape=jax.ShapeDtypeStruct((b, s, A_Q_W), BF16),
        scratch_shapes=[pltpu.VMEM((A_KV_HEADS, len(offsets), A_GROUP * Q_BLOCK, kw), F32)],
        compiler_params=_params("arbitrary", "arbitrary"),
        name="attn_a",
    )(jnp.asarray(bkt), table_flat, sink, pb, pb, pb)


def _attn_b_kernel(*refs, seq, groups):
    ng = len(groups)
    it = iter(refs)
    bkt_refs = [next(it) for _ in range(ng)]
    table_ref = next(it)
    qkv_refs = [[next(it) for _ in range(3)] for _ in range(ng)]
    yb_ref = next(it)
    bias_refs = [next(it) for _ in range(ng)]
    o_acc, l_acc, stage_in, stage_out = next(it), next(it), next(it), next(it)
    head = pl.program_id(1)

    @pl.when(pl.program_id(0) == 0)
    def _():
        for g, grp in enumerate(groups):
            _build_bias(bkt_refs[g], table_ref, bias_refs[g], head, grp["head0"] + head, 1,
                        grp["buckets_used"])

    for g, grp in enumerate(groups):
        dil, sub, half, chain_group = grp["dil"], grp["sub"], grp["half"], grp["chain_group"]
        kw, nb, _ = _window_geometry(sub, half)
        stage = STAGE_STRIDE if dil > STAGE_STRIDE else 1
        hop2 = dil // stage
        accs = (o_acc, l_acc)

        def rows(residue, start, size, dil=dil, stage=stage, hop2=hop2):
            if dil == 1:
                return pl.ds(start, size)
            if stage > 1:
                return pl.ds(residue // stage + start * hop2, size, stride=hop2)
            return pl.ds(residue + start * dil, size, stride=dil)

        def load(t, residue, start, size, g=g, stage=stage, rows=rows):
            if stage > 1:
                return stage_in[t, residue % stage, rows(residue, start, size), :]
            return qkv_refs[g][t][0, rows(residue, start, size), :]

        def store(t, residue, start, value, g=g, stage=stage, rows=rows, accs=accs):
            if stage > 1:
                stage_out[t, residue % stage, rows(residue, start, Q_BLOCK), :] = value
            else:
                accs[t][g, rows(residue, start, Q_BLOCK), :] = value

        def bias(residue, kind, g=g):
            return bias_refs[g][head, kind]

        run = functools.partial(_softmax_chains, load=load, store=store, bias=bias, kw=kw,
                                want_lse=True)
        if dil == 1:
            per_iter = min(chain_group, nb)
            assert nb % per_iter == 0

            def body(it_, carry, run=run, per_iter=per_iter, sub=sub, half=half):
                run([(0,) + _block_geometry(it_ * per_iter + di, sub, half) for di in range(per_iter)])
                return carry

            lax.fori_loop(0, nb // per_iter, body, 0)
            continue
        if stage > 1:
            for t in range(3):
                for r1 in range(stage):
                    stage_in[t, r1] = qkv_refs[g][t][0, pl.ds(r1, seq // stage, stride=stage), :]
        work = [(r,) + _block_geometry(i, sub, half) for r in range(dil) for i in range(nb)]
        for c in range(0, len(work), chain_group):
            run(work[c:c + chain_group])
        if stage > 1:
            for t in range(2):
                for r1 in range(stage):
                    accs[t][g, pl.ds(r1, seq // stage, stride=stage), :] = stage_out[t, r1]

    def merge(c, carry):
        rws = pl.ds(pl.multiple_of(c * MERGE_ROWS, MERGE_ROWS), MERGE_ROWS)
        ls = [l_acc[g, rws, :] for g in range(ng)]
        mx = functools.reduce(jnp.maximum, ls)
        es = [jnp.exp(l - mx) for l in ls]
        tot = functools.reduce(lambda a, e: a + e, es)
        y = functools.reduce(lambda a, e: a + e, [e * o_acc[g, rws, :] for g, e in enumerate(es)])
        yb_ref[0, rws, :] = (y / tot).astype(yb_ref.dtype)
        return carry

    lax.fori_loop(0, seq // MERGE_ROWS, merge, 0)


def _attn_b(pb, table_flat):
    b, s, _ = pb.shape
    nh = B_HEADS_PER_GROUP
    groups, bkts, bias_scratch = [], [], []
    for gi, (window, dil) in enumerate(B_PATTERNS):
        sub, half = s // dil, window // (2 * dil)
        kw, _, offsets = _window_geometry(sub, half)
        bkt = _band_buckets(sub, half, dil)
        bkts.append(bkt)
        bias_scratch.append(pltpu.VMEM((nh, len(offsets), Q_BLOCK, kw), F32))
        groups.append(dict(dil=dil, sub=sub, half=half, chain_group=CHAIN_GROUP[gi],
                           head0=A_Q_HEADS + gi * nh, buckets_used=_buckets_used(bkt)))

    def qkv_spec(gi, part):
        base = (PB_COL_QB + part * B_W + gi * B_GROUP_W) // HEAD_DIM
        return pl.BlockSpec((1, s, HEAD_DIM), lambda i, h: (i, 0, base + h))

    ng = len(groups)
    return pl.pallas_call(
        functools.partial(_attn_b_kernel, seq=s, groups=groups),
        grid=(b, nh),
        in_specs=[pl.BlockSpec(bkt.shape, lambda i, h: (0, 0, 0)) for bkt in bkts] + [_smem_spec()]
        + [qkv_spec(gi, part) for gi in range(ng) for part in range(3)],
        out_specs=pl.BlockSpec((1, s, HEAD_DIM), lambda i, h: (i, 0, h)),
        out_shape=jax.ShapeDtypeStruct((b, s, B_GROUP_W), BF16),
        scratch_shapes=bias_scratch + [
            pltpu.VMEM((ng, s, HEAD_DIM), F32), pltpu.VMEM((ng, s, HEAD_DIM), F32),
            pltpu.VMEM((3, STAGE_STRIDE, s // STAGE_STRIDE, HEAD_DIM), F32),
            pltpu.VMEM((2, STAGE_STRIDE, s // STAGE_STRIDE, HEAD_DIM), F32)],
        compiler_params=_params("arbitrary", "arbitrary"),
        name="attn_b",
    )(*[jnp.asarray(bkt) for bkt in bkts], table_flat, *([pb] * (3 * ng)))


def _mix_kernel(ya_ref, yb_ref, ga_ref, gb_ref, x_ref, wa_ref, wb_ref, wo_ref, g_ref, x1_ref, hf_ref):
    ta = jnp.dot(ya_ref[...], wa_ref[...], preferred_element_type=F32)
    tb = jnp.dot(yb_ref[...], wb_ref[...], preferred_element_type=F32)
    merged = (jax.nn.sigmoid(ga_ref[...].astype(F32)) * ta
              + jax.nn.sigmoid(gb_ref[...].astype(F32)) * tb)
    x1 = x_ref[...] + jnp.dot(merged.astype(BF16), wo_ref[...], preferred_element_type=F32)
    x1_ref[...] = x1
    hf_ref[...] = _rmsnorm(x1, g_ref[...]).astype(BF16)


def _resident(shape):
    return pl.BlockSpec(shape, lambda i: (0,) * len(shape), pipeline_mode=pl.Buffered(1))


def _mix(ya, yb, pa2d, x2d, wa, wb, wo, gain, *, tm=512):
    m, d = x2d.shape
    assert PA_COL_GA % d == 0 and PA_COL_GB % d == 0

    def rows(width, col_block=0):
        return pl.BlockSpec((tm, width), lambda i: (i, col_block))

    return pl.pallas_call(
        _mix_kernel,
        grid=(m // tm,),
        in_specs=[rows(A_Q_W), rows(B_GROUP_W),
                  rows(d, PA_COL_GA // d), rows(d, PA_COL_GB // d), rows(d),
                  _resident(wa.shape), _resident(wb.shape), _resident(wo.shape), _resident(gain.shape)],
        out_specs=[rows(d), rows(d)],
        out_shape=[jax.ShapeDtypeStruct((m, d), F32), jax.ShapeDtypeStruct((m, d), BF16)],
        compiler_params=_params("parallel"),
        name="mix",
    )(ya, yb, pa2d, pa2d, x2d, wa, wb, wo, gain)


def _ffn_up_kernel(hf_ref, wg_ref, wu_ref, cw_ref, cb_ref, act_ref):
    hf = hf_ref[...]
    g = jnp.dot(hf, wg_ref[...].astype(BF16), preferred_element_type=F32)
    u = jnp.dot(hf, wu_ref[...].astype(BF16), preferred_element_type=F32)
    s = g.shape[0]
    row = lax.broadcasted_iota(jnp.int32, g.shape, 0)
    prev = jnp.where(row == 0, 0.0, pltpu.roll(g, 1, 0))
    nxt = jnp.where(row == s - 1, 0.0, pltpu.roll(g, s - 1, 0))
    cw = cw_ref[...]
    conv = prev * cw[0:1] + g * cw[1:2] + nxt * cw[2:3] + cb_ref[...]
    act_ref[...] = (jax.nn.gelu(conv) * u).astype(act_ref.dtype)


def _ffn_up(hf, wg, wu, cw, cb, *, seq, tf=512):
    m, d = hf.shape
    f = wg.shape[1]
    return pl.pallas_call(
        _ffn_up_kernel,
        grid=(m // seq, f // tf),
        in_specs=[
            pl.BlockSpec((seq, d), lambda i, j: (i, 0)),
            pl.BlockSpec((d, tf), lambda i, j: (0, j)),
            pl.BlockSpec((d, tf), lambda i, j: (0, j)),
            pl.BlockSpec((cw.shape[0], tf), lambda i, j: (0, j)),
            pl.BlockSpec((1, tf), lambda i, j: (0, j)),
        ],
        out_specs=pl.BlockSpec((seq, tf), lambda i, j: (i, j)),
        out_shape=jax.ShapeDtypeStruct((m, f), BF16),
        compiler_params=_params("parallel", "arbitrary"),
        name="ffn_up",
    )(hf, wg, wu, cw, cb)


def _ffn_down_kernel(act_ref, w_ref, x_ref, o_ref):
    o_ref[...] = x_ref[...] + jnp.dot(act_ref[...], w_ref[...], preferred_element_type=F32)


def _ffn_down(act, wd, x1, *, tm=512, tn=1024):
    m, f = act.shape
    n = wd.shape[1]
    return pl.pallas_call(
        _ffn_down_kernel,
        grid=(n // tn, m // tm),
        in_specs=[
            pl.BlockSpec((tm, f), lambda j, i: (i, 0)),
            pl.BlockSpec((f, tn), lambda j, i: (0, j)),
            pl.BlockSpec((tm, tn), lambda j, i: (i, j)),
        ],
        out_specs=pl.BlockSpec((tm, tn), lambda j, i: (i, j)),
        out_shape=jax.ShapeDtypeStruct((m, n), F32),
        compiler_params=_params("parallel", "parallel"),
        name="ffn_down",
    )(act, wd, x1)


def _ple_kernel(x_ref, p_ref, gp_ref, wg_ref, wp_ref, gf_ref, o_ref, *, final):
    x = x_ref[...]
    hp = _rmsnorm(x, gp_ref[...]).astype(BF16)
    gate = jax.nn.sigmoid(jnp.dot(hp, wg_ref[...], preferred_element_type=F32))
    emb = jnp.dot(p_ref[...].astype(BF16), wp_ref[...], preferred_element_type=F32)
    y = x + gate * emb
    o_ref[...] = _rmsnorm(y, gf_ref[...]) if final else y


def _ple(x2, p2d, gain_p, wpg, wpp, gain_f, *, final, tm=512):
    m, d = x2.shape
    pd = p2d.shape[1]
    return pl.pallas_call(
        functools.partial(_ple_kernel, final=final),
        grid=(m // tm,),
        in_specs=[
            pl.BlockSpec((tm, d), lambda i: (i, 0)),
            pl.BlockSpec((tm, pd), lambda i: (i, 0)),
            _resident(gain_p.shape), _resident(wpg.shape), _resident(wpp.shape),
            _resident(gain_f.shape),
        ],
        out_specs=pl.BlockSpec((tm, d), lambda i: (i, 0)),
        out_shape=jax.ShapeDtypeStruct((m, d), F32),
        compiler_params=_params("parallel"),
        name="ple",
    )(x2, p2d, gain_p, wpg, wpp, gain_f)


def kernel(x, p, rel_bias_table, attn_norm, w_in, sink_a, w_branch_a, w_branch_b, w_out,
           ffn_norm, w_ffn_gate, w_ffn_up, conv_w, conv_b, w_ffn_down,
           ple_norm, w_ple_gate, w_ple_proj, final_norm):
    b, s, d = x.shape
    depth = w_in.shape[0]
    assert d == D_MODEL and w_in.shape[2] == IN_PROJ_W and s % (Q_BLOCK * B_PATTERNS[-1][1]) == 0
    assert rel_bias_table.shape == (N_BUCKETS, N_BIAS_HEADS)
    m = b * s
    x2d = x.reshape(m, d)
    table_flat = rel_bias_table.reshape(-1)
    for i in range(depth):
        pb2d, pa2d = _inproj(x2d, attn_norm[i][None], w_in[i].astype(BF16))
        pb = pb2d.reshape(b, s, PB_W)
        ya = _attn_a(pb, table_flat, sink_a[i]).reshape(m, A_Q_W)
        yb = _attn_b(pb, table_flat).reshape(m, B_GROUP_W)
        x1, hf = _mix(ya, yb, pa2d, x2d, w_branch_a[i].astype(BF16), w_branch_b[i].astype(BF16),
                      w_out[i].astype(BF16), ffn_norm[i][None])
        act = _ffn_up(hf, w_ffn_gate[i], w_ffn_up[i], conv_w[i], conv_b[i][None], seq=s)
        x2 = _ffn_down(act, w_ffn_down[i].astype(BF16), x1)
        x2d = _ple(x2, p[i].reshape(m, -1), ple_norm[i][None], w_ple_gate[i].astype(BF16),
                   w_ple_proj[i].astype(BF16), final_norm[None], final=i == depth - 1)
    return x2d.reshape(b, s, d)
```

```python
import functools
import math

import jax
import jax.numpy as jnp
import numpy as np
from jax import lax
from jax.experimental import pallas as pl
from jax.experimental.pallas import tpu as pltpu

D_MODEL = 2048
HEAD_DIM = 128
A_Q_HEADS = 8
A_KV_HEADS = 2
A_GROUP = A_Q_HEADS // A_KV_HEADS
A_HALF_WINDOW = 128
B_PATTERNS = ((128, 1), (512, 4), (2048, 16))
B_HEADS_PER_GROUP = 4
N_BUCKETS = 32
MAX_DISTANCE = 1024
N_BIAS_HEADS = A_Q_HEADS + len(B_PATTERNS) * B_HEADS_PER_GROUP
A_Q_W = A_Q_HEADS * HEAD_DIM
A_KV_W = A_KV_HEADS * HEAD_DIM
B_GROUP_W = B_HEADS_PER_GROUP * HEAD_DIM
B_W = len(B_PATTERNS) * B_GROUP_W
IN_PROJ_W = A_Q_W + 2 * A_KV_W + 3 * B_W + 2 * D_MODEL
RMS_EPS = 1e-6
NEG_INF = -1e30
LOG2_E = math.log2(math.e)
QK_SCALE_LOG2 = HEAD_DIM ** -0.5 * LOG2_E

PB_W = A_Q_W + 2 * A_KV_W + 3 * B_W
PB_COL_KA = A_Q_W
PB_COL_VA = PB_COL_KA + A_KV_W
PB_COL_QB = PB_COL_VA + A_KV_W
PA_W = 2 * D_MODEL
PA_COL_GA = 0
PA_COL_GB = D_MODEL

Q_BLOCK = 128
ATTN_UNROLL = 2
STAGE_STRIDE = 4
CHAIN_GROUP = (16, 1, 16)
A_BLOCKS_PER_GROUP = 2
MERGE_ROWS = 256
FFN_ROW_PARTS = 2
MIX_ROW_PARTS = 2
PLE_ROW_PARTS = 1

VMEM_LIMIT_BYTES = 56 * 1024 * 1024

BF16 = jnp.bfloat16
F32 = jnp.float32


def _params(*semantics):
    return pltpu.CompilerParams(dimension_semantics=semantics, vmem_limit_bytes=VMEM_LIMIT_BYTES)


def _rmsnorm(x, g):
    y = x * lax.rsqrt(jnp.mean(x * x, axis=-1, keepdims=True) + RMS_EPS)
    return y * g


def _inproj_kernel(x_ref, g_ref, w_ref, pb_ref, pa_ref, h_ref, *, row_chunk, n_b_tiles):
    j = pl.program_id(1)

    @pl.when(j == 0)
    def _():
        def body(c, _):
            rows = pl.ds(pl.multiple_of(c * row_chunk, row_chunk), row_chunk)
            h_ref[rows, :] = _rmsnorm(x_ref[rows, :], g_ref[...]).astype(BF16)
            return 0

        lax.fori_loop(0, x_ref.shape[0] // row_chunk, body, 0)

    @pl.when(j < n_b_tiles)
    def _():
        pb_ref[...] = jnp.dot(h_ref[...], w_ref[...], preferred_element_type=F32)

    @pl.when(j >= n_b_tiles)
    def _():
        pa_ref[...] = jnp.dot(h_ref[...], w_ref[...], preferred_element_type=F32).astype(BF16)


def _inproj(x2d, gain, w, *, tm=1024, tn=1024):
    m, k = x2d.shape
    assert PB_W % tn == 0 and PA_W % tn == 0 and w.shape[1] == PB_W + PA_W
    nb_t = PB_W // tn
    n_steps = w.shape[1] // tn

    return pl.pallas_call(
        functools.partial(_inproj_kernel, row_chunk=128, n_b_tiles=nb_t),
        grid=(m // tm, n_steps),
        in_specs=[
            pl.BlockSpec((tm, k), lambda i, j: (i, 0)),
            pl.BlockSpec((1, k), lambda i, j: (0, 0)),
            pl.BlockSpec((k, tn), lambda i, j: (0, j)),
        ],
        out_specs=[
            pl.BlockSpec((tm, tn), lambda i, j: (i, jnp.minimum(j, nb_t - 1))),
            pl.BlockSpec((tm, tn), lambda i, j: (i, jnp.maximum(j - nb_t, 0))),
        ],
        out_shape=[jax.ShapeDtypeStruct((m, PB_W), F32), jax.ShapeDtypeStruct((m, PA_W), BF16)],
        scratch_shapes=[pltpu.VMEM((tm, k), BF16)],
        compiler_params=_params("arbitrary", "arbitrary"),
        name="inproj",
    )(x2d, gain, w)


def _t5_bucket_static(rel):
    half = N_BUCKETS // 2
    max_exact = half // 2
    n = np.abs(rel)
    side = np.where(rel > 0, half, 0)
    nf = np.maximum(n, 1).astype(np.float32)
    large = max_exact + (np.log(nf / max_exact) / math.log(MAX_DISTANCE / max_exact)
                         * (half - max_exact)).astype(np.int32)
    large = np.minimum(large, half - 1)
    return side + np.where(n < max_exact, n, large)


def _window_geometry(seq_len, half_w):
    nb = seq_len // Q_BLOCK
    if nb == 1:
        return seq_len, nb, (0,)
    kw = Q_BLOCK + 2 * half_w
    assert half_w <= Q_BLOCK and kw <= seq_len
    return kw, nb, (0, -half_w, Q_BLOCK - kw)


def _band_buckets(seq_len, half_w, dilation):
    kw, _, offsets = _window_geometry(seq_len, half_w)
    qi = np.arange(Q_BLOCK)[:, None]
    kj = np.arange(kw)[None, :]
    rel = np.stack([off + kj - qi for off in offsets])
    buckets = _t5_bucket_static(rel * dilation)
    return np.where(np.abs(rel) <= half_w, buckets, -1).astype(np.int32)


def _block_geometry(i, seq_len, half_w):
    kw, nb, _ = _window_geometry(seq_len, half_w)
    if isinstance(i, int):
        qs = i * Q_BLOCK
        return qs, min(max(qs - half_w, 0), seq_len - kw), 0 if i == 0 else (2 if i == nb - 1 else 1)
    qs = pl.multiple_of(i * Q_BLOCK, Q_BLOCK)
    ks = pl.multiple_of(jnp.clip(qs - half_w, 0, seq_len - kw), half_w)
    return qs, ks, jnp.where(i == 0, 0, jnp.where(i == nb - 1, 2, 1))


def _build_bias(bkt_ref, table_ref, bias_ref, slot, first_head, n_heads, buckets_used):
    for kind in range(bkt_ref.shape[0]):
        bkt = bkt_ref[kind]
        for h in range(n_heads):
            acc = jnp.full(bkt.shape, NEG_INF, F32)
            for bucket in buckets_used:
                acc = jnp.where(bkt == bucket,
                                table_ref[bucket * N_BIAS_HEADS + first_head + h] * LOG2_E, acc)
            bias_ref[slot, kind, h * Q_BLOCK:(h + 1) * Q_BLOCK, :] = acc


def _softmax_chains(items, *, load, store, bias, kw, sink=None, want_lse):
    ones = jnp.ones((kw, HEAD_DIM), BF16)
    scores = []
    for tag, qs, ks, kind in items:
        q = (load(0, tag, qs, Q_BLOCK) * QK_SCALE_LOG2).astype(BF16)
        s = lax.dot_general(q, load(1, tag, ks, kw).astype(BF16),
                            (((1,), (1,)), ((), ())), preferred_element_type=F32)
        scores.append(s + bias(tag, kind))
    for (tag, qs, ks, _), s in zip(items, scores):
        v_ext = jnp.concatenate([load(2, tag, ks, kw).astype(BF16), ones], axis=1)
        m = jnp.max(s, axis=-1, keepdims=True)
        if sink is not None:
            sk = sink(tag) * LOG2_E
            m = jnp.maximum(m, sk)
        p = jnp.exp2(s - m).astype(BF16)
        ov = jnp.dot(p, v_ext, preferred_element_type=F32)
        denom = ov[:, HEAD_DIM:]
        if sink is not None:
            denom = denom + jnp.exp2(sk - m)
        store(0, tag, qs, ov[:, :HEAD_DIM] / denom)
        if want_lse:
            store(1, tag, qs, m * (1.0 / LOG2_E) + jnp.log(denom))


def _attn_a_kernel(bkt_ref, table_ref, sink_ref, q_ref, k_ref, v_ref, o_ref, bias_ref, *,
                   seq_len, buckets_used):
    kv = pl.program_id(1)
    kw, nb, _ = _window_geometry(seq_len, A_HALF_WINDOW)

    @pl.when(pl.program_id(0) == 0)
    def _():
        _build_bias(bkt_ref, table_ref, bias_ref, kv, kv * A_GROUP, A_GROUP, buckets_used)

    def load(t, h, start, size):
        if t == 0:
            return q_ref[0, pl.ds(start, size), h * HEAD_DIM:(h + 1) * HEAD_DIM]
        return (k_ref, v_ref)[t - 1][0, pl.ds(start, size), :]

    def store(t, h, start, value):
        o_ref[0, pl.ds(start, Q_BLOCK), h * HEAD_DIM:(h + 1) * HEAD_DIM] = value.astype(o_ref.dtype)

    def body(it, carry):
        items = [(h,) + _block_geometry(it * A_BLOCKS_PER_GROUP + di, seq_len, A_HALF_WINDOW)
                 for di in range(A_BLOCKS_PER_GROUP) for h in range(A_GROUP)]
        _softmax_chains(
            items, load=load, store=store, kw=kw, want_lse=False,
            bias=lambda h, kind: bias_ref[kv, kind, h * Q_BLOCK:(h + 1) * Q_BLOCK, :],
            sink=lambda h: jnp.full((Q_BLOCK, 1), sink_ref[kv * A_GROUP + h], F32))
        return carry

    assert nb % A_BLOCKS_PER_GROUP == 0
    lax.fori_loop(0, nb // A_BLOCKS_PER_GROUP, body, 0, unroll=ATTN_UNROLL)


def _buckets_used(bkt):
    return tuple(int(v) for v in np.unique(bkt) if v >= 0)


def _smem_spec():
    return pl.BlockSpec(memory_space=pltpu.SMEM)


def _attn_a(pb, table_flat, sink):
    b, s, _ = pb.shape
    gw = A_GROUP * HEAD_DIM
    kw, _, offsets = _window_geometry(s, A_HALF_WINDOW)
    bkt = _band_buckets(s, A_HALF_WINDOW, 1)
    kernel = functools.partial(_attn_a_kernel, seq_len=s, buckets_used=_buckets_used(bkt))
    return pl.pallas_call(
        kernel,
        grid=(b, A_KV_HEADS),
        in_specs=[
            pl.BlockSpec(bkt.shape, lambda i, j: (0, 0, 0)),
            _smem_spec(),
            _smem_spec(),
            pl.BlockSpec((1, s, gw), lambda i, j: (i, 0, j)),
            pl.BlockSpec((1, s, HEAD_DIM), lambda i, j: (i, 0, PB_COL_KA // HEAD_DIM + j)),
            pl.BlockSpec((1, s, HEAD_DIM), lambda i, j: (i, 0, PB_COL_VA // HEAD_DIM + j)),
        ],
        out_specs=pl.BlockSpec((1, s, gw), lambda i, j: (i, 0, j)),
        out_shape=jax.ShapeDtypeStruct((b, s, A_Q_W), BF16),
        scratch_shapes=[pltpu.VMEM((A_KV_HEADS, len(offsets), A_GROUP * Q_BLOCK, kw), F32)],
        compiler_params=_params("arbitrary", "arbitrary"),
        name="attn_a",
    )(jnp.asarray(bkt), table_flat, sink, pb, pb, pb)


def _attn_b_kernel(*refs, seq, groups):
    ng = len(groups)
    it = iter(refs)
    bkt_refs = [next(it) for _ in range(ng)]
    table_ref = next(it)
    qkv_refs = [[next(it) for _ in range(3)] for _ in range(ng)]
    yb_ref = next(it)
    bias_refs = [next(it) for _ in range(ng)]
    o_acc, l_acc, stage_in, stage_out = next(it), next(it), next(it), next(it)
    head = pl.program_id(1)

    @pl.when(pl.program_id(0) == 0)
    def _():
        for g, grp in enumerate(groups):
            _build_bias(bkt_refs[g], table_ref, bias_refs[g], head, grp["head0"] + head, 1,
                        grp["buckets_used"])

    for g, grp in enumerate(groups):
        dil, sub, half, chain_group = grp["dil"], grp["sub"], grp["half"], grp["chain_group"]
        kw, nb, _ = _window_geometry(sub, half)
        stage = STAGE_STRIDE if dil > STAGE_STRIDE else 1
        hop2 = dil // stage
        accs = (o_acc, l_acc)

        def rows(residue, start, size, dil=dil, stage=stage, hop2=hop2):
            if dil == 1:
                return pl.ds(start, size)
            if stage > 1:
                return pl.ds(residue // stage + start * hop2, size, stride=hop2)
            return pl.ds(residue + start * dil, size, stride=dil)

        def load(t, residue, start, size, g=g, stage=stage, rows=rows):
            if stage > 1:
                return stage_in[t, residue % stage, rows(residue, start, size), :]
            return qkv_refs[g][t][0, rows(residue, start, size), :]

        def store(t, residue, start, value, g=g, stage=stage, rows=rows, accs=accs):
            if stage > 1:
                stage_out[t, residue % stage, rows(residue, start, Q_BLOCK), :] = value
            else:
                accs[t][g, rows(residue, start, Q_BLOCK), :] = value

        def bias(residue, kind, g=g):
            return bias_refs[g][head, kind]

        run = functools.partial(_softmax_chains, load=load, store=store, bias=bias, kw=kw,
                                want_lse=True)
        if dil == 1:
            per_iter = min(chain_group, nb)
            assert nb % per_iter == 0

            def body(it_, carry, run=run, per_iter=per_iter, sub=sub, half=half):
                run([(0,) + _block_geometry(it_ * per_iter + di, sub, half) for di in range(per_iter)])
                return carry

            lax.fori_loop(0, nb // per_iter, body, 0)
            continue
        if stage > 1:
            for t in range(3):
                for r1 in range(stage):
                    stage_in[t, r1] = qkv_refs[g][t][0, pl.ds(r1, seq // stage, stride=stage), :]
        work = [(r,) + _block_geometry(i, sub, half) for r in range(dil) for i in range(nb)]
        for c in range(0, len(work), chain_group):
            run(work[c:c + chain_group])
        if stage > 1:
            for t in range(2):
                for r1 in range(stage):
                    accs[t][g, pl.ds(r1, seq // stage, stride=stage), :] = stage_out[t, r1]

    def merge(c, carry):
        rws = pl.ds(pl.multiple_of(c * MERGE_ROWS, MERGE_ROWS), MERGE_ROWS)
        ls = [l_acc[g, rws, :] for g in range(ng)]
        mx = functools.reduce(jnp.maximum, ls)
        es = [jnp.exp(l - mx) for l in ls]
        tot = functools.reduce(lambda a, e: a + e, es)
        y = functools.reduce(lambda a, e: a + e, [e * o_acc[g, rws, :] for g, e in enumerate(es)])
        yb_ref[0, rws, :] = (y / tot).astype(yb_ref.dtype)
        return carry

    lax.fori_loop(0, seq // MERGE_ROWS, merge, 0)


def _attn_b(pb, table_flat):
    b, s, _ = pb.shape
    nh = B_HEADS_PER_GROUP
    groups, bkts, bias_scratch = [], [], []
    for gi, (window, dil) in enumerate(B_PATTERNS):
        sub, half = s // dil, window // (2 * dil)
        kw, _, offsets = _window_geometry(sub, half)
        bkt = _band_buckets(sub, half, dil)
        bkts.append(bkt)
        bias_scratch.append(pltpu.VMEM((nh, len(offsets), Q_BLOCK, kw), F32))
        groups.append(dict(dil=dil, sub=sub, half=half, chain_group=CHAIN_GROUP[gi],
                           head0=A_Q_HEADS + gi * nh, buckets_used=_buckets_used(bkt)))

    def qkv_spec(gi, part):
        base = (PB_COL_QB + part * B_W + gi * B_GROUP_W) // HEAD_DIM
        return pl.BlockSpec((1, s, HEAD_DIM), lambda i, h: (i, 0, base + h))

    ng = len(groups)
    return pl.pallas_call(
        functools.partial(_attn_b_kernel, seq=s, groups=groups),
        grid=(b, nh),
        in_specs=[pl.BlockSpec(bkt.shape, lambda i, h: (0, 0, 0)) for bkt in bkts] + [_smem_spec()]
        + [qkv_spec(gi, part) for gi in range(ng) for part in range(3)],
        out_specs=pl.BlockSpec((1, s, HEAD_DIM), lambda i, h: (i, 0, h)),
        out_shape=jax.ShapeDtypeStruct((b, s, B_GROUP_W), BF16),
        scratch_shapes=bias_scratch + [
            pltpu.VMEM((ng, s, HEAD_DIM), F32), pltpu.VMEM((ng, s, HEAD_DIM), F32),
            pltpu.VMEM((3, STAGE_STRIDE, s // STAGE_STRIDE, HEAD_DIM), F32),
            pltpu.VMEM((2, STAGE_STRIDE, s // STAGE_STRIDE, HEAD_DIM), F32)],
        compiler_params=_params("arbitrary", "arbitrary"),
        name="attn_b",
    )(*[jnp.asarray(bkt) for bkt in bkts], table_flat, *([pb] * (3 * ng)))


def _mix_kernel(ya_ref, yb_ref, ga_ref, gb_ref, x_ref, wa_ref, wb_ref, wo_ref, g_ref, x1_ref, hf_ref):
    rp = x_ref.shape[0] // MIX_ROW_PARTS
    for r in range(0, x_ref.shape[0], rp):
        rows = slice(r, r + rp)
        ta = jnp.dot(ya_ref[rows, :], wa_ref[...], preferred_element_type=F32)
        tb = jnp.dot(yb_ref[rows, :], wb_ref[...], preferred_element_type=F32)
        merged = (jax.nn.sigmoid(ga_ref[rows, :].astype(F32)) * ta
                  + jax.nn.sigmoid(gb_ref[rows, :].astype(F32)) * tb)
        x1 = x_ref[rows, :] + jnp.dot(merged.astype(BF16), wo_ref[...], preferred_element_type=F32)
        x1_ref[rows, :] = x1
        hf_ref[rows, :] = _rmsnorm(x1, g_ref[...]).astype(BF16)


def _resident(shape):
    return pl.BlockSpec(shape, lambda i: (0,) * len(shape), pipeline_mode=pl.Buffered(1))


def _mix(ya, yb, pa2d, x2d, wa, wb, wo, gain, *, tm=512):
    m, d = x2d.shape
    assert PA_COL_GA % d == 0 and PA_COL_GB % d == 0

    def rows(width, col_block=0):
        return pl.BlockSpec((tm, width), lambda i: (i, col_block))

    return pl.pallas_call(
        _mix_kernel,
        grid=(m // tm,),
        in_specs=[rows(A_Q_W), rows(B_GROUP_W),
                  rows(d, PA_COL_GA // d), rows(d, PA_COL_GB // d), rows(d),
                  _resident(wa.shape), _resident(wb.shape), _resident(wo.shape), _resident(gain.shape)],
        out_specs=[rows(d), rows(d)],
        out_shape=[jax.ShapeDtypeStruct((m, d), F32), jax.ShapeDtypeStruct((m, d), BF16)],
        compiler_params=_params("parallel"),
        name="mix",
    )(ya, yb, pa2d, pa2d, x2d, wa, wb, wo, gain)


def _ffn_up_kernel(hf_ref, wg_ref, wu_ref, cw_ref, cb_ref, act_ref):
    wg = wg_ref[...].astype(BF16)
    wu = wu_ref[...].astype(BF16)
    rp = hf_ref.shape[0] // FFN_ROW_PARTS
    parts = [(jnp.dot(hf_ref[r:r + rp, :], wg, preferred_element_type=F32),
              jnp.dot(hf_ref[r:r + rp, :], wu, preferred_element_type=F32))
             for r in range(0, hf_ref.shape[0], rp)]
    g = jnp.concatenate([p[0] for p in parts], axis=0)
    u = jnp.concatenate([p[1] for p in parts], axis=0)
    s = g.shape[0]
    row = lax.broadcasted_iota(jnp.int32, g.shape, 0)
    prev = jnp.where(row == 0, 0.0, pltpu.roll(g, 1, 0))
    nxt = jnp.where(row == s - 1, 0.0, pltpu.roll(g, s - 1, 0))
    cw = cw_ref[...]
    conv = prev * cw[0:1] + g * cw[1:2] + nxt * cw[2:3] + cb_ref[...]
    act_ref[...] = (jax.nn.gelu(conv) * u).astype(act_ref.dtype)


def _ffn_up(hf, wg, wu, cw, cb, *, seq, tf=512):
    m, d = hf.shape
    f = wg.shape[1]
    return pl.pallas_call(
        _ffn_up_kernel,
        grid=(m // seq, f // tf),
        in_specs=[
            pl.BlockSpec((seq, d), lambda i, j: (i, 0)),
            pl.BlockSpec((d, tf), lambda i, j: (0, j)),
            pl.BlockSpec((d, tf), lambda i, j: (0, j)),
            pl.BlockSpec((cw.shape[0], tf), lambda i, j: (0, j)),
            pl.BlockSpec((1, tf), lambda i, j: (0, j)),
        ],
        out_specs=pl.BlockSpec((seq, tf), lambda i, j: (i, j)),
        out_shape=jax.ShapeDtypeStruct((m, f), BF16),
        compiler_params=_params("parallel", "arbitrary"),
        name="ffn_up",
    )(hf, wg, wu, cw, cb)


def _ffn_down_kernel(act_ref, w_ref, x_ref, o_ref):
    o_ref[...] = x_ref[...] + jnp.dot(act_ref[...], w_ref[...], preferred_element_type=F32)


def _ffn_down(act, wd, x1, *, tm=512, tn=1024):
    m, f = act.shape
    n = wd.shape[1]
    return pl.pallas_call(
        _ffn_down_kernel,
        grid=(n // tn, m // tm),
        in_specs=[
            pl.BlockSpec((tm, f), lambda j, i: (i, 0)),
            pl.BlockSpec((f, tn), lambda j, i: (0, j)),
            pl.BlockSpec((tm, tn), lambda j, i: (i, j)),
        ],
        out_specs=pl.BlockSpec((tm, tn), lambda j, i: (i, j)),
        out_shape=jax.ShapeDtypeStruct((m, n), F32),
        compiler_params=_params("parallel", "parallel"),
        name="ffn_down",
    )(act, wd, x1)


def _ple_kernel(x_ref, p_ref, gp_ref, wg_ref, wp_ref, gf_ref, o_ref, *, final):
    rp = x_ref.shape[0] // PLE_ROW_PARTS
    for r in range(0, x_ref.shape[0], rp):
        rows = slice(r, r + rp)
        x = x_ref[rows, :]
        hp = _rmsnorm(x, gp_ref[...]).astype(BF16)
        gate = jax.nn.sigmoid(jnp.dot(hp, wg_ref[...], preferred_element_type=F32))
        emb = jnp.dot(p_ref[rows, :].astype(BF16), wp_ref[...], preferred_element_type=F32)
        y = x + gate * emb
        o_ref[rows, :] = _rmsnorm(y, gf_ref[...]) if final else y


def _ple(x2, p2d, gain_p, wpg, wpp, gain_f, *, final, tm=512):
    m, d = x2.shape
    pd = p2d.shape[1]
    return pl.pallas_call(
        functools.partial(_ple_kernel, final=final),
        grid=(m // tm,),
        in_specs=[
            pl.BlockSpec((tm, d), lambda i: (i, 0)),
            pl.BlockSpec((tm, pd), lambda i: (i, 0)),
            _resident(gain_p.shape), _resident(wpg.shape), _resident(wpp.shape),
            _resident(gain_f.shape),
        ],
        out_specs=pl.BlockSpec((tm, d), lambda i: (i, 0)),
        out_shape=jax.ShapeDtypeStruct((m, d), F32),
        compiler_params=_params("parallel"),
        name="ple",
    )(x2, p2d, gain_p, wpg, wpp, gain_f)


def kernel(x, p, rel_bias_table, attn_norm, w_in, sink_a, w_branch_a, w_branch_b, w_out,
           ffn_norm, w_ffn_gate, w_ffn_up, conv_w, conv_b, w_ffn_down,
           ple_norm, w_ple_gate, w_ple_proj, final_norm):
    b, s, d = x.shape
    depth = w_in.shape[0]
    assert d == D_MODEL and w_in.shape[2] == IN_PROJ_W and s % (Q_BLOCK * B_PATTERNS[-1][1]) == 0
    assert rel_bias_table.shape == (N_BUCKETS, N_BIAS_HEADS)
    m = b * s
    x2d = x.reshape(m, d)
    table_flat = rel_bias_table.reshape(-1)
    for i in range(depth):
        pb2d, pa2d = _inproj(x2d, attn_norm[i][None], w_in[i].astype(BF16))
        pb = pb2d.reshape(b, s, PB_W)
        ya = _attn_a(pb, table_flat, sink_a[i]).reshape(m, A_Q_W)
        yb = _attn_b(pb, table_flat).reshape(m, B_GROUP_W)
        x1, hf = _mix(ya, yb, pa2d, x2d, w_branch_a[i].astype(BF16), w_branch_b[i].astype(BF16),
                      w_out[i].astype(BF16), ffn_norm[i][None])
        act = _ffn_up(hf, w_ffn_gate[i], w_ffn_up[i], conv_w[i], conv_b[i][None], seq=s)
        x2 = _ffn_down(act, w_ffn_down[i].astype(BF16), x1)
        x2d = _ple(x2, p[i].reshape(m, -1), ple_norm[i][None], w_ple_gate[i].astype(BF16),
                   w_ple_proj[i].astype(BF16), final_norm[None], final=i == depth - 1)
    return x2d.reshape(b, s, d)
```

```python
import functools
import math

import jax
import jax.numpy as jnp
import numpy as np
from jax import lax
from jax.experimental import pallas as pl
from jax.experimental.pallas import tpu as pltpu

D_MODEL = 2048
HEAD_DIM = 128
A_Q_HEADS = 8
A_KV_HEADS = 2
A_GROUP = A_Q_HEADS // A_KV_HEADS
A_HALF_WINDOW = 128
B_PATTERNS = ((128, 1), (512, 4), (2048, 16))
B_HEADS_PER_GROUP = 4
N_BUCKETS = 32
MAX_DISTANCE = 1024
N_BIAS_HEADS = A_Q_HEADS + len(B_PATTERNS) * B_HEADS_PER_GROUP
A_Q_W = A_Q_HEADS * HEAD_DIM
A_KV_W = A_KV_HEADS * HEAD_DIM
B_GROUP_W = B_HEADS_PER_GROUP * HEAD_DIM
B_W = len(B_PATTERNS) * B_GROUP_W
IN_PROJ_W = A_Q_W + 2 * A_KV_W + 3 * B_W + 2 * D_MODEL
RMS_EPS = 1e-6
NEG_INF = -1e30
LOG2_E = math.log2(math.e)
QK_SCALE_LOG2 = HEAD_DIM ** -0.5 * LOG2_E

PB_W = A_Q_W + 2 * A_KV_W + 3 * B_W
PB_COL_KA = A_Q_W
PB_COL_VA = PB_COL_KA + A_KV_W
PB_COL_QB = PB_COL_VA + A_KV_W
PA_W = 2 * D_MODEL
PA_COL_GA = 0
PA_COL_GB = D_MODEL

Q_BLOCK = 128
ATTN_UNROLL = 2
STAGE_STRIDE = 4
CHAIN_GROUP = (16, 1, 16)
A_BLOCKS_PER_GROUP = 2
MERGE_ROWS = 256
FFN_ROW_PARTS = 2
MIX_ROW_PARTS = 2
PLE_ROW_PARTS = 1

VMEM_LIMIT_BYTES = 56 * 1024 * 1024

BF16 = jnp.bfloat16
F32 = jnp.float32


def _params(*semantics):
    return pltpu.CompilerParams(dimension_semantics=semantics, vmem_limit_bytes=VMEM_LIMIT_BYTES)


def _rmsnorm(x, g):
    y = x * lax.rsqrt(jnp.mean(x * x, axis=-1, keepdims=True) + RMS_EPS)
    return y * g


def _inproj_kernel(x_ref, g_ref, w_ref, pb_ref, pa_ref, h_ref, *, row_chunk, n_b_tiles):
    j = pl.program_id(1)

    @pl.when(j == 0)
    def _():
        for r in range(0, x_ref.shape[0], row_chunk):
            rows = slice(r, r + row_chunk)
            h = _rmsnorm(x_ref[rows, :], g_ref[...]).astype(BF16)
            h_ref[rows, :] = h
            pb_ref[rows, :] = jnp.dot(h, w_ref[...], preferred_element_type=F32)

    @pl.when((j > 0) & (j < n_b_tiles))
    def _():
        pb_ref[...] = jnp.dot(h_ref[...], w_ref[...], preferred_element_type=F32)

    @pl.when(j >= n_b_tiles)
    def _():
        pa_ref[...] = jnp.dot(h_ref[...], w_ref[...], preferred_element_type=F32).astype(BF16)


def _inproj(x2d, gain, w, *, tm=1024, tn=1024):
    m, k = x2d.shape
    assert PB_W % tn == 0 and PA_W % tn == 0 and w.shape[1] == PB_W + PA_W
    nb_t = PB_W // tn
    n_steps = w.shape[1] // tn

    return pl.pallas_call(
        functools.partial(_inproj_kernel, row_chunk=256, n_b_tiles=nb_t),
        grid=(m // tm, n_steps),
        in_specs=[
            pl.BlockSpec((tm, k), lambda i, j: (i, 0)),
            pl.BlockSpec((1, k), lambda i, j: (0, 0)),
            pl.BlockSpec((k, tn), lambda i, j: (0, j)),
        ],
        out_specs=[
            pl.BlockSpec((tm, tn), lambda i, j: (i, jnp.minimum(j, nb_t - 1))),
            pl.BlockSpec((tm, tn), lambda i, j: (i, jnp.maximum(j - nb_t, 0))),
        ],
        out_shape=[jax.ShapeDtypeStruct((m, PB_W), F32), jax.ShapeDtypeStruct((m, PA_W), BF16)],
        scratch_shapes=[pltpu.VMEM((tm, k), BF16)],
        compiler_params=_params("arbitrary", "arbitrary"),
        name="inproj",
    )(x2d, gain, w)


def _t5_bucket_static(rel):
    half = N_BUCKETS // 2
    max_exact = half // 2
    n = np.abs(rel)
    side = np.where(rel > 0, half, 0)
    nf = np.maximum(n, 1).astype(np.float32)
    large = max_exact + (np.log(nf / max_exact) / math.log(MAX_DISTANCE / max_exact)
                         * (half - max_exact)).astype(np.int32)
    large = np.minimum(large, half - 1)
    return side + np.where(n < max_exact, n, large)


def _window_geometry(seq_len, half_w):
    nb = seq_len // Q_BLOCK
    if nb == 1:
        return seq_len, nb, (0,)
    kw = Q_BLOCK + 2 * half_w
    assert half_w <= Q_BLOCK and kw <= seq_len
    return kw, nb, (0, -half_w, Q_BLOCK - kw)


def _band_buckets(seq_len, half_w, dilation):
    kw, _, offsets = _window_geometry(seq_len, half_w)
    qi = np.arange(Q_BLOCK)[:, None]
    kj = np.arange(kw)[None, :]
    rel = np.stack([off + kj - qi for off in offsets])
    buckets = _t5_bucket_static(rel * dilation)
    return np.where(np.abs(rel) <= half_w, buckets, -1).astype(np.int32)


def _block_geometry(i, seq_len, half_w):
    kw, nb, _ = _window_geometry(seq_len, half_w)
    if isinstance(i, int):
        qs = i * Q_BLOCK
        return qs, min(max(qs - half_w, 0), seq_len - kw), 0 if i == 0 else (2 if i == nb - 1 else 1)
    qs = pl.multiple_of(i * Q_BLOCK, Q_BLOCK)
    ks = pl.multiple_of(jnp.clip(qs - half_w, 0, seq_len - kw), half_w)
    return qs, ks, jnp.where(i == 0, 0, jnp.where(i == nb - 1, 2, 1))


def _build_bias(bkt_ref, table_ref, bias_ref, slot, first_head, n_heads, buckets_used):
    for kind in range(bkt_ref.shape[0]):
        bkt = bkt_ref[kind]
        for h in range(n_heads):
            acc = jnp.full(bkt.shape, NEG_INF, F32)
            for bucket in buckets_used:
                acc = jnp.where(bkt == bucket,
                                table_ref[bucket * N_BIAS_HEADS + first_head + h] * LOG2_E, acc)
            bias_ref[slot, kind, h * Q_BLOCK:(h + 1) * Q_BLOCK, :] = acc


def _softmax_chains(items, *, load, store, bias, kw, sink=None, want_lse):
    ones = jnp.ones((kw, HEAD_DIM), BF16)
    scores = []
    for tag, qs, ks, kind in items:
        q = (load(0, tag, qs, Q_BLOCK) * QK_SCALE_LOG2).astype(BF16)
        s = lax.dot_general(q, load(1, tag, ks, kw).astype(BF16),
                            (((1,), (1,)), ((), ())), preferred_element_type=F32)
        scores.append(s + bias(tag, kind))
    for (tag, qs, ks, _), s in zip(items, scores):
        v_ext = jnp.concatenate([load(2, tag, ks, kw).astype(BF16), ones], axis=1)
        m = jnp.max(s, axis=-1, keepdims=True)
        if sink is not None:
            sk = sink(tag) * LOG2_E
            m = jnp.maximum(m, sk)
        p = jnp.exp2(s - m).astype(BF16)
        ov = jnp.dot(p, v_ext, preferred_element_type=F32)
        denom = ov[:, HEAD_DIM:]
        if sink is not None:
            denom = denom + jnp.exp2(sk - m)
        store(0, tag, qs, ov[:, :HEAD_DIM] / denom)
        if want_lse:
            store(1, tag, qs, m * (1.0 / LOG2_E) + jnp.log(denom))


def _attn_a_kernel(bkt_ref, table_ref, sink_ref, q_ref, k_ref, v_ref, o_ref, bias_ref, *,
                   seq_len, buckets_used):
    kv = pl.program_id(1)
    kw, nb, _ = _window_geometry(seq_len, A_HALF_WINDOW)

    @pl.when(pl.program_id(0) == 0)
    def _():
        _build_bias(bkt_ref, table_ref, bias_ref, kv, kv * A_GROUP, A_GROUP, buckets_used)

    def load(t, h, start, size):
        if t == 0:
            return q_ref[0, pl.ds(start, size), h * HEAD_DIM:(h + 1) * HEAD_DIM]
        return (k_ref, v_ref)[t - 1][0, pl.ds(start, size), :]

    def store(t, h, start, value):
        o_ref[0, pl.ds(start, Q_BLOCK), h * HEAD_DIM:(h + 1) * HEAD_DIM] = value.astype(o_ref.dtype)

    def body(it, carry):
        items = [(h,) + _block_geometry(it * A_BLOCKS_PER_GROUP + di, seq_len, A_HALF_WINDOW)
                 for di in range(A_BLOCKS_PER_GROUP) for h in range(A_GROUP)]
        _softmax_chains(
            items, load=load, store=store, kw=kw, want_lse=False,
            bias=lambda h, kind: bias_ref[kv, kind, h * Q_BLOCK:(h + 1) * Q_BLOCK, :],
            sink=lambda h: jnp.full((Q_BLOCK, 1), sink_ref[kv * A_GROUP + h], F32))
        return carry

    assert nb % A_BLOCKS_PER_GROUP == 0
    lax.fori_loop(0, nb // A_BLOCKS_PER_GROUP, body, 0, unroll=ATTN_UNROLL)


def _buckets_used(bkt):
    return tuple(int(v) for v in np.unique(bkt) if v >= 0)


def _smem_spec():
    return pl.BlockSpec(memory_space=pltpu.SMEM)


def _attn_a(pb, table_flat, sink):
    b, s, _ = pb.shape
    gw = A_GROUP * HEAD_DIM
    kw, _, offsets = _window_geometry(s, A_HALF_WINDOW)
    bkt = _band_buckets(s, A_HALF_WINDOW, 1)
    kernel = functools.partial(_attn_a_kernel, seq_len=s, buckets_used=_buckets_used(bkt))
    return pl.pallas_call(
        kernel,
        grid=(b, A_KV_HEADS),
        in_specs=[
            pl.BlockSpec(bkt.shape, lambda i, j: (0, 0, 0)),
            _smem_spec(),
            _smem_spec(),
            pl.BlockSpec((1, s, gw), lambda i, j: (i, 0, j)),
            pl.BlockSpec((1, s, HEAD_DIM), lambda i, j: (i, 0, PB_COL_KA // HEAD_DIM + j)),
            pl.BlockSpec((1, s, HEAD_DIM), lambda i, j: (i, 0, PB_COL_VA // HEAD_DIM + j)),
        ],
        out_specs=pl.BlockSpec((1, s, gw), lambda i, j: (i, 0, j)),
        out_shape=jax.ShapeDtypeStruct((b, s, A_Q_W), BF16),
        scratch_shapes=[pltpu.VMEM((A_KV_HEADS, len(offsets), A_GROUP * Q_BLOCK, kw), F32)],
        compiler_params=_params("arbitrary", "arbitrary"),
        name="attn_a",
    )(jnp.asarray(bkt), table_flat, sink, pb, pb, pb)


def _attn_b_kernel(*refs, seq, groups):
    ng = len(groups)
    it = iter(refs)
    bkt_refs = [next(it) for _ in range(ng)]
    table_ref = next(it)
    qkv_refs = [[next(it) for _ in range(3)] for _ in range(ng)]
    yb_ref = next(it)
    bias_refs = [next(it) for _ in range(ng)]
    o_acc, l_acc, stage_in, stage_out = next(it), next(it), next(it), next(it)
    head = pl.program_id(1)

    @pl.when(pl.program_id(0) == 0)
    def _():
        for g, grp in enumerate(groups):
            _build_bias(bkt_refs[g], table_ref, bias_refs[g], head, grp["head0"] + head, 1,
                        grp["buckets_used"])

    for g, grp in enumerate(groups):
        dil, sub, half, chain_group = grp["dil"], grp["sub"], grp["half"], grp["chain_group"]
        kw, nb, _ = _window_geometry(sub, half)
        stage = STAGE_STRIDE if dil > STAGE_STRIDE else 1
        hop2 = dil // stage
        accs = (o_acc, l_acc)

        def rows(residue, start, size, dil=dil, stage=stage, hop2=hop2):
            if dil == 1:
                return pl.ds(start, size)
            if stage > 1:
                return pl.ds(residue // stage + start * hop2, size, stride=hop2)
            return pl.ds(residue + start * dil, size, stride=dil)

        def load(t, residue, start, size, g=g, stage=stage, rows=rows):
            if stage > 1:
                return stage_in[t, residue % stage, rows(residue, start, size), :]
            return qkv_refs[g][t][0, rows(residue, start, size), :]

        def store(t, residue, start, value, g=g, stage=stage, rows=rows, accs=accs):
            if stage > 1:
                stage_out[t, residue % stage, rows(residue, start, Q_BLOCK), :] = value
            else:
                accs[t][g, rows(residue, start, Q_BLOCK), :] = value

        def bias(residue, kind, g=g):
            return bias_refs[g][head, kind]

        run = functools.partial(_softmax_chains, load=load, store=store, bias=bias, kw=kw,
                                want_lse=True)
        if dil == 1:
            per_iter = min(chain_group, nb)
            assert nb % per_iter == 0

            def body(it_, carry, run=run, per_iter=per_iter, sub=sub, half=half):
                run([(0,) + _block_geometry(it_ * per_iter + di, sub, half) for di in range(per_iter)])
                return carry

            lax.fori_loop(0, nb // per_iter, body, 0)
            continue
        if stage > 1:
            for t in range(3):
                for r1 in range(stage):
                    stage_in[t, r1] = qkv_refs[g][t][0, pl.ds(r1, seq // stage, stride=stage), :]
        work = [(r,) + _block_geometry(i, sub, half) for r in range(dil) for i in range(nb)]
        for c in range(0, len(work), chain_group):
            run(work[c:c + chain_group])
        if stage > 1:
            for t in range(2):
                for r1 in range(stage):
                    accs[t][g, pl.ds(r1, seq // stage, stride=stage), :] = stage_out[t, r1]

    def merge(c, carry):
        rws = pl.ds(pl.multiple_of(c * MERGE_ROWS, MERGE_ROWS), MERGE_ROWS)
        ls = [l_acc[g, rws, :] for g in range(ng)]
        mx = functools.reduce(jnp.maximum, ls)
        es = [jnp.exp(l - mx) for l in ls]
        tot = functools.reduce(lambda a, e: a + e, es)
        y = functools.reduce(lambda a, e: a + e, [e * o_acc[g, rws, :] for g, e in enumerate(es)])
        yb_ref[0, rws, :] = (y / tot).astype(yb_ref.dtype)
        return carry

    lax.fori_loop(0, seq // MERGE_ROWS, merge, 0)


def _attn_b(pb, table_flat):
    b, s, _ = pb.shape
    nh = B_HEADS_PER_GROUP
    groups, bkts, bias_scratch = [], [], []
    for gi, (window, dil) in enumerate(B_PATTERNS):
        sub, half = s // dil, window // (2 * dil)
        kw, _, offsets = _window_geometry(sub, half)
        bkt = _band_buckets(sub, half, dil)
        bkts.append(bkt)
        bias_scratch.append(pltpu.VMEM((nh, len(offsets), Q_BLOCK, kw), F32))
        groups.append(dict(dil=dil, sub=sub, half=half, chain_group=CHAIN_GROUP[gi],
                           head0=A_Q_HEADS + gi * nh, buckets_used=_buckets_used(bkt)))

    def qkv_spec(gi, part):
        base = (PB_COL_QB + part * B_W + gi * B_GROUP_W) // HEAD_DIM
        return pl.BlockSpec((1, s, HEAD_DIM), lambda i, h: (i, 0, base + h))

    ng = len(groups)
    return pl.pallas_call(
        functools.partial(_attn_b_kernel, seq=s, groups=groups),
        grid=(b, nh),
        in_specs=[pl.BlockSpec(bkt.shape, lambda i, h: (0, 0, 0)) for bkt in bkts] + [_smem_spec()]
        + [qkv_spec(gi, part) for gi in range(ng) for part in range(3)],
        out_specs=pl.BlockSpec((1, s, HEAD_DIM), lambda i, h: (i, 0, h)),
        out_shape=jax.ShapeDtypeStruct((b, s, B_GROUP_W), BF16),
        scratch_shapes=bias_scratch + [
            pltpu.VMEM((ng, s, HEAD_DIM), F32), pltpu.VMEM((ng, s, HEAD_DIM), F32),
            pltpu.VMEM((3, STAGE_STRIDE, s // STAGE_STRIDE, HEAD_DIM), F32),
            pltpu.VMEM((2, STAGE_STRIDE, s // STAGE_STRIDE, HEAD_DIM), F32)],
        compiler_params=_params("arbitrary", "arbitrary"),
        name="attn_b",
    )(*[jnp.asarray(bkt) for bkt in bkts], table_flat, *([pb] * (3 * ng)))


def _mix_kernel(ya_ref, yb_ref, ga_ref, gb_ref, x_ref, wa_ref, wb_ref, wo_ref, g_ref, x1_ref, hf_ref):
    rp = x_ref.shape[0] // MIX_ROW_PARTS
    for r in range(0, x_ref.shape[0], rp):
        rows = slice(r, r + rp)
        ta = jnp.dot(ya_ref[rows, :], wa_ref[...], preferred_element_type=F32)
        tb = jnp.dot(yb_ref[rows, :], wb_ref[...], preferred_element_type=F32)
        merged = (jax.nn.sigmoid(ga_ref[rows, :].astype(F32)) * ta
                  + jax.nn.sigmoid(gb_ref[rows, :].astype(F32)) * tb)
        x1 = x_ref[rows, :] + jnp.dot(merged.astype(BF16), wo_ref[...], preferred_element_type=F32)
        x1_ref[rows, :] = x1
        hf_ref[rows, :] = _rmsnorm(x1, g_ref[...]).astype(BF16)


def _resident(shape):
    return pl.BlockSpec(shape, lambda i: (0,) * len(shape), pipeline_mode=pl.Buffered(1))


def _mix(ya, yb, pa2d, x2d, wa, wb, wo, gain, *, tm=512):
    m, d = x2d.shape
    assert PA_COL_GA % d == 0 and PA_COL_GB % d == 0

    def rows(width, col_block=0):
        return pl.BlockSpec((tm, width), lambda i: (i, col_block))

    return pl.pallas_call(
        _mix_kernel,
        grid=(m // tm,),
        in_specs=[rows(A_Q_W), rows(B_GROUP_W),
                  rows(d, PA_COL_GA // d), rows(d, PA_COL_GB // d), rows(d),
                  _resident(wa.shape), _resident(wb.shape), _resident(wo.shape), _resident(gain.shape)],
        out_specs=[rows(d), rows(d)],
        out_shape=[jax.ShapeDtypeStruct((m, d), F32), jax.ShapeDtypeStruct((m, d), BF16)],
        compiler_params=_params("parallel"),
        name="mix",
    )(ya, yb, pa2d, pa2d, x2d, wa, wb, wo, gain)


def _ffn_up_kernel(hf_ref, wg_ref, wu_ref, cw_ref, cb_ref, act_ref):
    wg = wg_ref[...].astype(BF16)
    wu = wu_ref[...].astype(BF16)
    rp = hf_ref.shape[0] // FFN_ROW_PARTS
    parts = [(jnp.dot(hf_ref[r:r + rp, :], wg, preferred_element_type=F32),
              jnp.dot(hf_ref[r:r + rp, :], wu, preferred_element_type=F32))
             for r in range(0, hf_ref.shape[0], rp)]
    g = jnp.concatenate([p[0] for p in parts], axis=0)
    u = jnp.concatenate([p[1] for p in parts], axis=0)
    s = g.shape[0]
    row = lax.broadcasted_iota(jnp.int32, g.shape, 0)
    prev = jnp.where(row == 0, 0.0, pltpu.roll(g, 1, 0))
    nxt = jnp.where(row == s - 1, 0.0, pltpu.roll(g, s - 1, 0))
    cw = cw_ref[...]
    conv = prev * cw[0:1] + g * cw[1:2] + nxt * cw[2:3] + cb_ref[...]
    act_ref[...] = (jax.nn.gelu(conv) * u).astype(act_ref.dtype)


def _ffn_up(hf, wg, wu, cw, cb, *, seq, tf=512):
    m, d = hf.shape
    f = wg.shape[1]
    return pl.pallas_call(
        _ffn_up_kernel,
        grid=(m // seq, f // tf),
        in_specs=[
            pl.BlockSpec((seq, d), lambda i, j: (i, 0)),
            pl.BlockSpec((d, tf), lambda i, j: (0, j)),
            pl.BlockSpec((d, tf), lambda i, j: (0, j)),
            pl.BlockSpec((cw.shape[0], tf), lambda i, j: (0, j)),
            pl.BlockSpec((1, tf), lambda i, j: (0, j)),
        ],
        out_specs=pl.BlockSpec((seq, tf), lambda i, j: (i, j)),
        out_shape=jax.ShapeDtypeStruct((m, f), BF16),
        compiler_params=_params("parallel", "arbitrary"),
        name="ffn_up",
    )(hf, wg, wu, cw, cb)


def _ffn_down_kernel(act_ref, w_ref, x_ref, o_ref):
    o_ref[...] = x_ref[...] + jnp.dot(act_ref[...], w_ref[...], preferred_element_type=F32)


def _ffn_down(act, wd, x1, *, tm=512, tn=1024):
    m, f = act.shape
    n = wd.shape[1]
    return pl.pallas_call(
        _ffn_down_kernel,
        grid=(n // tn, m // tm),
        in_specs=[
            pl.BlockSpec((tm, f), lambda j, i: (i, 0)),
            pl.BlockSpec((f, tn), lambda j, i: (0, j)),
            pl.BlockSpec((tm, tn), lambda j, i: (i, j)),
        ],
        out_specs=pl.BlockSpec((tm, tn), lambda j, i: (i, j)),
        out_shape=jax.ShapeDtypeStruct((m, n), F32),
        compiler_params=_params("parallel", "parallel"),
        name="ffn_down",
    )(act, wd, x1)


def _ple_kernel(x_ref, p_ref, gp_ref, wg_ref, wp_ref, gf_ref, o_ref, *, final):
    rp = x_ref.shape[0] // PLE_ROW_PARTS
    for r in range(0, x_ref.shape[0], rp):
        rows = slice(r, r + rp)
        x = x_ref[rows, :]
        hp = _rmsnorm(x, gp_ref[...]).astype(BF16)
        gate = jax.nn.sigmoid(jnp.dot(hp, wg_ref[...], preferred_element_type=F32))
        emb = jnp.dot(p_ref[rows, :].astype(BF16), wp_ref[...], preferred_element_type=F32)
        y = x + gate * emb
        o_ref[rows, :] = _rmsnorm(y, gf_ref[...]) if final else y


def _ple(x2, p2d, gain_p, wpg, wpp, gain_f, *, final, tm=512):
    m, d = x2.shape
    pd = p2d.shape[1]
    return pl.pallas_call(
        functools.partial(_ple_kernel, final=final),
        grid=(m // tm,),
        in_specs=[
            pl.BlockSpec((tm, d), lambda i: (i, 0)),
            pl.BlockSpec((tm, pd), lambda i: (i, 0)),
            _resident(gain_p.shape), _resident(wpg.shape), _resident(wpp.shape),
            _resident(gain_f.shape),
        ],
        out_specs=pl.BlockSpec((tm, d), lambda i: (i, 0)),
        out_shape=jax.ShapeDtypeStruct((m, d), F32),
        compiler_params=_params("parallel"),
        name="ple",
    )(x2, p2d, gain_p, wpg, wpp, gain_f)


def kernel(x, p, rel_bias_table, attn_norm, w_in, sink_a, w_branch_a, w_branch_b, w_out,
           ffn_norm, w_ffn_gate, w_ffn_up, conv_w, conv_b, w_ffn_down,
           ple_norm, w_ple_gate, w_ple_proj, final_norm):
    b, s, d = x.shape
    depth = w_in.shape[0]
    assert d == D_MODEL and w_in.shape[2] == IN_PROJ_W and s % (Q_BLOCK * B_PATTERNS[-1][1]) == 0
    assert rel_bias_table.shape == (N_BUCKETS, N_BIAS_HEADS)
    m = b * s
    x2d = x.reshape(m, d)
    table_flat = rel_bias_table.reshape(-1)
    for i in range(depth):
        pb2d, pa2d = _inproj(x2d, attn_norm[i][None], w_in[i].astype(BF16))
        pb = pb2d.reshape(b, s, PB_W)
        ya = _attn_a(pb, table_flat, sink_a[i]).reshape(m, A_Q_W)
        yb = _attn_b(pb, table_flat).reshape(m, B_GROUP_W)
        x1, hf = _mix(ya, yb, pa2d, x2d, w_branch_a[i].astype(BF16), w_branch_b[i].astype(BF16),
                      w_out[i].astype(BF16), ffn_norm[i][None])
        act = _ffn_up(hf, w_ffn_gate[i], w_ffn_up[i], conv_w[i], conv_b[i][None], seq=s)
        x2 = _ffn_down(act, w_ffn_down[i].astype(BF16), x1)
        x2d = _ple(x2, p[i].reshape(m, -1), ple_norm[i][None], w_ple_gate[i].astype(BF16),
                   w_ple_proj[i].astype(BF16), final_norm[None], final=i == depth - 1)
    return x2d.reshape(b, s, d)
```

```python
import functools
import math

import jax
import jax.numpy as jnp
import numpy as np
from jax import lax
from jax.experimental import pallas as pl
from jax.experimental.pallas import tpu as pltpu

D_MODEL = 2048
HEAD_DIM = 128
A_Q_HEADS = 8
A_KV_HEADS = 2
A_GROUP = A_Q_HEADS // A_KV_HEADS
A_HALF_WINDOW = 128
B_PATTERNS = ((128, 1), (512, 4), (2048, 16))
B_HEADS_PER_GROUP = 4
N_BUCKETS = 32
MAX_DISTANCE = 1024
N_BIAS_HEADS = A_Q_HEADS + len(B_PATTERNS) * B_HEADS_PER_GROUP
A_Q_W = A_Q_HEADS * HEAD_DIM
A_KV_W = A_KV_HEADS * HEAD_DIM
B_GROUP_W = B_HEADS_PER_GROUP * HEAD_DIM
B_W = len(B_PATTERNS) * B_GROUP_W
IN_PROJ_W = A_Q_W + 2 * A_KV_W + 3 * B_W + 2 * D_MODEL
RMS_EPS = 1e-6
NEG_INF = -1e30
LOG2_E = math.log2(math.e)
QK_SCALE_LOG2 = HEAD_DIM ** -0.5 * LOG2_E

PB_W = A_Q_W + 2 * A_KV_W + 3 * B_W
PB_COL_KA = A_Q_W
PB_COL_VA = PB_COL_KA + A_KV_W
PB_COL_QB = PB_COL_VA + A_KV_W
PA_W = 2 * D_MODEL
PA_COL_GA = 0
PA_COL_GB = D_MODEL

Q_BLOCK = 128
ATTN_UNROLL = 2
STAGE_STRIDE = 4
CHAIN_GROUP = (16, 1, 16)
A_BLOCKS_PER_GROUP = 2
MERGE_ROWS = 256
FFN_ROW_PARTS = 2
MIX_ROW_PARTS = 2
PLE_ROW_PARTS = 1

VMEM_LIMIT_BYTES = 56 * 1024 * 1024

BF16 = jnp.bfloat16
F32 = jnp.float32


def _params(*semantics):
    return pltpu.CompilerParams(dimension_semantics=semantics, vmem_limit_bytes=VMEM_LIMIT_BYTES)


def _rmsnorm(x, g):
    y = x * lax.rsqrt(jnp.mean(x * x, axis=-1, keepdims=True) + RMS_EPS)
    return y * g


def _inproj_kernel(x_ref, g_ref, w_ref, pb_ref, pa_ref, h_ref, *, row_chunk, n_b_tiles):
    j = pl.program_id(1)

    def store_slabs(rows, res):
        for s in range(pb_ref.shape[0]):
            pb_ref[s, rows, :] = res[:, s * HEAD_DIM:(s + 1) * HEAD_DIM]

    @pl.when(j == 0)
    def _():
        for r in range(0, x_ref.shape[0], row_chunk):
            rows = slice(r, r + row_chunk)
            h = _rmsnorm(x_ref[rows, :], g_ref[...]).astype(BF16)
            h_ref[rows, :] = h
            store_slabs(rows, jnp.dot(h, w_ref[...], preferred_element_type=F32))

    @pl.when((j > 0) & (j < n_b_tiles))
    def _():
        store_slabs(slice(None), jnp.dot(h_ref[...], w_ref[...], preferred_element_type=F32))

    @pl.when(j >= n_b_tiles)
    def _():
        pa_ref[...] = jnp.dot(h_ref[...], w_ref[...], preferred_element_type=F32).astype(BF16)


def _inproj(x2d, gain, w, *, tm=1024, tn=1024):
    m, k = x2d.shape
    assert PB_W % tn == 0 and PA_W % tn == 0 and w.shape[1] == PB_W + PA_W
    nb_t = PB_W // tn
    n_steps = w.shape[1] // tn

    return pl.pallas_call(
        functools.partial(_inproj_kernel, row_chunk=256, n_b_tiles=nb_t),
        grid=(m // tm, n_steps),
        in_specs=[
            pl.BlockSpec((tm, k), lambda i, j: (i, 0)),
            pl.BlockSpec((1, k), lambda i, j: (0, 0)),
            pl.BlockSpec((k, tn), lambda i, j: (0, j)),
        ],
        out_specs=[
            pl.BlockSpec((tn // HEAD_DIM, tm, HEAD_DIM), lambda i, j: (jnp.minimum(j, nb_t - 1), i, 0)),
            pl.BlockSpec((tm, tn), lambda i, j: (i, jnp.maximum(j - nb_t, 0))),
        ],
        out_shape=[jax.ShapeDtypeStruct((PB_W // HEAD_DIM, m, HEAD_DIM), F32),
                   jax.ShapeDtypeStruct((m, PA_W), BF16)],
        scratch_shapes=[pltpu.VMEM((tm, k), BF16)],
        compiler_params=_params("arbitrary", "arbitrary"),
        name="inproj",
    )(x2d, gain, w)


def _t5_bucket_static(rel):
    half = N_BUCKETS // 2
    max_exact = half // 2
    n = np.abs(rel)
    side = np.where(rel > 0, half, 0)
    nf = np.maximum(n, 1).astype(np.float32)
    large = max_exact + (np.log(nf / max_exact) / math.log(MAX_DISTANCE / max_exact)
                         * (half - max_exact)).astype(np.int32)
    large = np.minimum(large, half - 1)
    return side + np.where(n < max_exact, n, large)


def _window_geometry(seq_len, half_w):
    nb = seq_len // Q_BLOCK
    if nb == 1:
        return seq_len, nb, (0,)
    kw = Q_BLOCK + 2 * half_w
    assert half_w <= Q_BLOCK and kw <= seq_len
    return kw, nb, (0, -half_w, Q_BLOCK - kw)


def _band_buckets(seq_len, half_w, dilation):
    kw, _, offsets = _window_geometry(seq_len, half_w)
    qi = np.arange(Q_BLOCK)[:, None]
    kj = np.arange(kw)[None, :]
    rel = np.stack([off + kj - qi for off in offsets])
    buckets = _t5_bucket_static(rel * dilation)
    return np.where(np.abs(rel) <= half_w, buckets, -1).astype(np.int32)


def _block_geometry(i, seq_len, half_w):
    kw, nb, _ = _window_geometry(seq_len, half_w)
    if isinstance(i, int):
        qs = i * Q_BLOCK
        return qs, min(max(qs - half_w, 0), seq_len - kw), 0 if i == 0 else (2 if i == nb - 1 else 1)
    qs = pl.multiple_of(i * Q_BLOCK, Q_BLOCK)
    ks = pl.multiple_of(jnp.clip(qs - half_w, 0, seq_len - kw), half_w)
    return qs, ks, jnp.where(i == 0, 0, jnp.where(i == nb - 1, 2, 1))


def _build_bias(bkt_ref, table_ref, bias_ref, slot, first_head, n_heads, buckets_used):
    for kind in range(bkt_ref.shape[0]):
        bkt = bkt_ref[kind]
        for h in range(n_heads):
            acc = jnp.full(bkt.shape, NEG_INF, F32)
            for bucket in buckets_used:
                acc = jnp.where(bkt == bucket,
                                table_ref[bucket * N_BIAS_HEADS + first_head + h] * LOG2_E, acc)
            bias_ref[slot, kind, h * Q_BLOCK:(h + 1) * Q_BLOCK, :] = acc


def _softmax_chains(items, *, load, store, bias, kw, sink=None, want_lse):
    ones = jnp.ones((kw, HEAD_DIM), BF16)
    scores = []
    for tag, qs, ks, kind in items:
        q = (load(0, tag, qs, Q_BLOCK) * QK_SCALE_LOG2).astype(BF16)
        s = lax.dot_general(q, load(1, tag, ks, kw).astype(BF16),
                            (((1,), (1,)), ((), ())), preferred_element_type=F32)
        scores.append(s + bias(tag, kind))
    for (tag, qs, ks, _), s in zip(items, scores):
        v_ext = jnp.concatenate([load(2, tag, ks, kw).astype(BF16), ones], axis=1)
        m = jnp.max(s, axis=-1, keepdims=True)
        if sink is not None:
            sk = sink(tag) * LOG2_E
            m = jnp.maximum(m, sk)
        p = jnp.exp2(s - m).astype(BF16)
        ov = jnp.dot(p, v_ext, preferred_element_type=F32)
        denom = ov[:, HEAD_DIM:]
        if sink is not None:
            denom = denom + jnp.exp2(sk - m)
        store(0, tag, qs, ov[:, :HEAD_DIM] / denom)
        if want_lse:
            store(1, tag, qs, m * (1.0 / LOG2_E) + jnp.log(denom))


def _attn_a_kernel(bkt_ref, table_ref, sink_ref, q_ref, k_ref, v_ref, o_ref, bias_ref, *,
                   seq_len, buckets_used):
    kv = pl.program_id(1)
    kw, nb, _ = _window_geometry(seq_len, A_HALF_WINDOW)

    @pl.when(pl.program_id(0) == 0)
    def _():
        _build_bias(bkt_ref, table_ref, bias_ref, kv, kv * A_GROUP, A_GROUP, buckets_used)

    def load(t, h, start, size):
        if t == 0:
            return q_ref[h, 0, pl.ds(start, size), :]
        return (k_ref, v_ref)[t - 1][0, 0, pl.ds(start, size), :]

    def store(t, h, start, value):
        o_ref[0, pl.ds(start, Q_BLOCK), h * HEAD_DIM:(h + 1) * HEAD_DIM] = value.astype(o_ref.dtype)

    def body(it, carry):
        items = [(h,) + _block_geometry(it * A_BLOCKS_PER_GROUP + di, seq_len, A_HALF_WINDOW)
                 for di in range(A_BLOCKS_PER_GROUP) for h in range(A_GROUP)]
        _softmax_chains(
            items, load=load, store=store, kw=kw, want_lse=False,
            bias=lambda h, kind: bias_ref[kv, kind, h * Q_BLOCK:(h + 1) * Q_BLOCK, :],
            sink=lambda h: jnp.full((Q_BLOCK, 1), sink_ref[kv * A_GROUP + h], F32))
        return carry

    assert nb % A_BLOCKS_PER_GROUP == 0
    lax.fori_loop(0, nb // A_BLOCKS_PER_GROUP, body, 0, unroll=ATTN_UNROLL)


def _buckets_used(bkt):
    return tuple(int(v) for v in np.unique(bkt) if v >= 0)


def _smem_spec():
    return pl.BlockSpec(memory_space=pltpu.SMEM)


def _attn_a(pb, table_flat, sink):
    _, b, s, _ = pb.shape
    gw = A_GROUP * HEAD_DIM
    kw, _, offsets = _window_geometry(s, A_HALF_WINDOW)
    bkt = _band_buckets(s, A_HALF_WINDOW, 1)
    kernel = functools.partial(_attn_a_kernel, seq_len=s, buckets_used=_buckets_used(bkt))
    return pl.pallas_call(
        kernel,
        grid=(b, A_KV_HEADS),
        in_specs=[
            pl.BlockSpec(bkt.shape, lambda i, j: (0, 0, 0)),
            _smem_spec(),
            _smem_spec(),
            pl.BlockSpec((A_GROUP, 1, s, HEAD_DIM), lambda i, j: (j, i, 0, 0)),
            pl.BlockSpec((1, 1, s, HEAD_DIM), lambda i, j: (PB_COL_KA // HEAD_DIM + j, i, 0, 0)),
            pl.BlockSpec((1, 1, s, HEAD_DIM), lambda i, j: (PB_COL_VA // HEAD_DIM + j, i, 0, 0)),
        ],
        out_specs=pl.BlockSpec((1, s, gw), lambda i, j: (i, 0, j)),
        out_shape=jax.ShapeDtypeStruct((b, s, A_Q_W), BF16),
        scratch_shapes=[pltpu.VMEM((A_KV_HEADS, len(offsets), A_GROUP * Q_BLOCK, kw), F32)],
        compiler_params=_params("arbitrary", "arbitrary"),
        name="attn_a",
    )(jnp.asarray(bkt), table_flat, sink, pb, pb, pb)


def _attn_b_kernel(*refs, seq, groups):
    ng = len(groups)
    it = iter(refs)
    bkt_refs = [next(it) for _ in range(ng)]
    table_ref = next(it)
    qkv_refs = [[next(it) for _ in range(3)] for _ in range(ng)]
    yb_ref = next(it)
    bias_refs = [next(it) for _ in range(ng)]
    o_acc, l_acc, stage_in, stage_out = next(it), next(it), next(it), next(it)
    head = pl.program_id(1)

    @pl.when(pl.program_id(0) == 0)
    def _():
        for g, grp in enumerate(groups):
            _build_bias(bkt_refs[g], table_ref, bias_refs[g], head, grp["head0"] + head, 1,
                        grp["buckets_used"])

    for g, grp in enumerate(groups):
        dil, sub, half, chain_group = grp["dil"], grp["sub"], grp["half"], grp["chain_group"]
        kw, nb, _ = _window_geometry(sub, half)
        stage = STAGE_STRIDE if dil > STAGE_STRIDE else 1
        hop2 = dil // stage
        accs = (o_acc, l_acc)

        def rows(residue, start, size, dil=dil, stage=stage, hop2=hop2):
            if dil == 1:
                return pl.ds(start, size)
            if stage > 1:
                return pl.ds(residue // stage + start * hop2, size, stride=hop2)
            return pl.ds(residue + start * dil, size, stride=dil)

        def load(t, residue, start, size, g=g, stage=stage, rows=rows):
            if stage > 1:
                return stage_in[t, residue % stage, rows(residue, start, size), :]
            return qkv_refs[g][t][0, 0, rows(residue, start, size), :]

        def store(t, residue, start, value, g=g, stage=stage, rows=rows, accs=accs):
            if stage > 1:
                stage_out[t, residue % stage, rows(residue, start, Q_BLOCK), :] = value
            else:
                accs[t][g, rows(residue, start, Q_BLOCK), :] = value

        def bias(residue, kind, g=g):
            return bias_refs[g][head, kind]

        run = functools.partial(_softmax_chains, load=load, store=store, bias=bias, kw=kw,
                                want_lse=True)
        if dil == 1:
            per_iter = min(chain_group, nb)
            assert nb % per_iter == 0

            def body(it_, carry, run=run, per_iter=per_iter, sub=sub, half=half):
                run([(0,) + _block_geometry(it_ * per_iter + di, sub, half) for di in range(per_iter)])
                return carry

            lax.fori_loop(0, nb // per_iter, body, 0)
            continue
        if stage > 1:
            for t in range(3):
                for r1 in range(stage):
                    stage_in[t, r1] = qkv_refs[g][t][0, 0, pl.ds(r1, seq // stage, stride=stage), :]
        work = [(r,) + _block_geometry(i, sub, half) for r in range(dil) for i in range(nb)]
        for c in range(0, len(work), chain_group):
            run(work[c:c + chain_group])
        if stage > 1:
            for t in range(2):
                for r1 in range(stage):
                    accs[t][g, pl.ds(r1, seq // stage, stride=stage), :] = stage_out[t, r1]

    def merge(c, carry):
        rws = pl.ds(pl.multiple_of(c * MERGE_ROWS, MERGE_ROWS), MERGE_ROWS)
        ls = [l_acc[g, rws, :] for g in range(ng)]
        mx = functools.reduce(jnp.maximum, ls)
        es = [jnp.exp(l - mx) for l in ls]
        tot = functools.reduce(lambda a, e: a + e, es)
        y = functools.reduce(lambda a, e: a + e, [e * o_acc[g, rws, :] for g, e in enumerate(es)])
        yb_ref[0, rws, :] = (y / tot).astype(yb_ref.dtype)
        return carry

    lax.fori_loop(0, seq // MERGE_ROWS, merge, 0)


def _attn_b(pb, table_flat):
    _, b, s, _ = pb.shape
    nh = B_HEADS_PER_GROUP
    groups, bkts, bias_scratch = [], [], []
    for gi, (window, dil) in enumerate(B_PATTERNS):
        sub, half = s // dil, window // (2 * dil)
        kw, _, offsets = _window_geometry(sub, half)
        bkt = _band_buckets(sub, half, dil)
        bkts.append(bkt)
        bias_scratch.append(pltpu.VMEM((nh, len(offsets), Q_BLOCK, kw), F32))
        groups.append(dict(dil=dil, sub=sub, half=half, chain_group=CHAIN_GROUP[gi],
                           head0=A_Q_HEADS + gi * nh, buckets_used=_buckets_used(bkt)))

    def qkv_spec(gi, part):
        base = (PB_COL_QB + part * B_W + gi * B_GROUP_W) // HEAD_DIM
        return pl.BlockSpec((1, 1, s, HEAD_DIM), lambda i, h: (base + h, i, 0, 0))

    ng = len(groups)
    return pl.pallas_call(
        functools.partial(_attn_b_kernel, seq=s, groups=groups),
        grid=(b, nh),
        in_specs=[pl.BlockSpec(bkt.shape, lambda i, h: (0, 0, 0)) for bkt in bkts] + [_smem_spec()]
        + [qkv_spec(gi, part) for gi in range(ng) for part in range(3)],
        out_specs=pl.BlockSpec((1, s, HEAD_DIM), lambda i, h: (i, 0, h)),
        out_shape=jax.ShapeDtypeStruct((b, s, B_GROUP_W), BF16),
        scratch_shapes=bias_scratch + [
            pltpu.VMEM((ng, s, HEAD_DIM), F32), pltpu.VMEM((ng, s, HEAD_DIM), F32),
            pltpu.VMEM((3, STAGE_STRIDE, s // STAGE_STRIDE, HEAD_DIM), F32),
            pltpu.VMEM((2, STAGE_STRIDE, s // STAGE_STRIDE, HEAD_DIM), F32)],
        compiler_params=_params("arbitrary", "arbitrary"),
        name="attn_b",
    )(*[jnp.asarray(bkt) for bkt in bkts], table_flat, *([pb] * (3 * ng)))


def _mix_kernel(ya_ref, yb_ref, ga_ref, gb_ref, x_ref, wa_ref, wb_ref, wo_ref, g_ref, x1_ref, hf_ref):
    rp = x_ref.shape[0] // MIX_ROW_PARTS
    for r in range(0, x_ref.shape[0], rp):
        rows = slice(r, r + rp)
        ta = jnp.dot(ya_ref[rows, :], wa_ref[...], preferred_element_type=F32)
        tb = jnp.dot(yb_ref[rows, :], wb_ref[...], preferred_element_type=F32)
        merged = (jax.nn.sigmoid(ga_ref[rows, :].astype(F32)) * ta
                  + jax.nn.sigmoid(gb_ref[rows, :].astype(F32)) * tb)
        x1 = x_ref[rows, :] + jnp.dot(merged.astype(BF16), wo_ref[...], preferred_element_type=F32)
        x1_ref[rows, :] = x1
        hf_ref[rows, :] = _rmsnorm(x1, g_ref[...]).astype(BF16)


def _resident(shape):
    return pl.BlockSpec(shape, lambda i: (0,) * len(shape), pipeline_mode=pl.Buffered(1))


def _mix(ya, yb, pa2d, x2d, wa, wb, wo, gain, *, tm=512):
    m, d = x2d.shape
    assert PA_COL_GA % d == 0 and PA_COL_GB % d == 0

    def rows(width, col_block=0):
        return pl.BlockSpec((tm, width), lambda i: (i, col_block))

    return pl.pallas_call(
        _mix_kernel,
        grid=(m // tm,),
        in_specs=[rows(A_Q_W), rows(B_GROUP_W),
                  rows(d, PA_COL_GA // d), rows(d, PA_COL_GB // d), rows(d),
                  _resident(wa.shape), _resident(wb.shape), _resident(wo.shape), _resident(gain.shape)],
        out_specs=[rows(d), rows(d)],
        out_shape=[jax.ShapeDtypeStruct((m, d), F32), jax.ShapeDtypeStruct((m, d), BF16)],
        compiler_params=_params("parallel"),
        name="mix",
    )(ya, yb, pa2d, pa2d, x2d, wa, wb, wo, gain)


def _ffn_up_kernel(hf_ref, wg_ref, wu_ref, cw_ref, cb_ref, act_ref):
    wg = wg_ref[...].astype(BF16)
    wu = wu_ref[...].astype(BF16)
    rp = hf_ref.shape[0] // FFN_ROW_PARTS
    parts = [(jnp.dot(hf_ref[r:r + rp, :], wg, preferred_element_type=F32),
              jnp.dot(hf_ref[r:r + rp, :], wu, preferred_element_type=F32))
             for r in range(0, hf_ref.shape[0], rp)]
    g = jnp.concatenate([p[0] for p in parts], axis=0)
    u = jnp.concatenate([p[1] for p in parts], axis=0)
    s = g.shape[0]
    row = lax.broadcasted_iota(jnp.int32, g.shape, 0)
    prev = jnp.where(row == 0, 0.0, pltpu.roll(g, 1, 0))
    nxt = jnp.where(row == s - 1, 0.0, pltpu.roll(g, s - 1, 0))
    cw = cw_ref[...]
    conv = prev * cw[0:1] + g * cw[1:2] + nxt * cw[2:3] + cb_ref[...]
    act_ref[...] = (jax.nn.gelu(conv) * u).astype(act_ref.dtype)


def _ffn_up(hf, wg, wu, cw, cb, *, seq, tf=512):
    m, d = hf.shape
    f = wg.shape[1]
    return pl.pallas_call(
        _ffn_up_kernel,
        grid=(m // seq, f // tf),
        in_specs=[
            pl.BlockSpec((seq, d), lambda i, j: (i, 0)),
            pl.BlockSpec((d, tf), lambda i, j: (0, j)),
            pl.BlockSpec((d, tf), lambda i, j: (0, j)),
            pl.BlockSpec((cw.shape[0], tf), lambda i, j: (0, j)),
            pl.BlockSpec((1, tf), lambda i, j: (0, j)),
        ],
        out_specs=pl.BlockSpec((seq, tf), lambda i, j: (i, j)),
        out_shape=jax.ShapeDtypeStruct((m, f), BF16),
        compiler_params=_params("parallel", "arbitrary"),
        name="ffn_up",
    )(hf, wg, wu, cw, cb)


def _ffn_down_kernel(act_ref, w_ref, x_ref, o_ref):
    o_ref[...] = x_ref[...] + jnp.dot(act_ref[...], w_ref[...], preferred_element_type=F32)


def _ffn_down(act, wd, x1, *, tm=512, tn=1024):
    m, f = act.shape
    n = wd.shape[1]
    return pl.pallas_call(
        _ffn_down_kernel,
        grid=(n // tn, m // tm),
        in_specs=[
            pl.BlockSpec((tm, f), lambda j, i: (i, 0)),
            pl.BlockSpec((f, tn), lambda j, i: (0, j)),
            pl.BlockSpec((tm, tn), lambda j, i: (i, j)),
        ],
        out_specs=pl.BlockSpec((tm, tn), lambda j, i: (i, j)),
        out_shape=jax.ShapeDtypeStruct((m, n), F32),
        compiler_params=_params("parallel", "parallel"),
        name="ffn_down",
    )(act, wd, x1)


def _ple_kernel(x_ref, p_ref, gp_ref, wg_ref, wp_ref, gf_ref, o_ref, *, final):
    rp = x_ref.shape[0] // PLE_ROW_PARTS
    for r in range(0, x_ref.shape[0], rp):
        rows = slice(r, r + rp)
        x = x_ref[rows, :]
        hp = _rmsnorm(x, gp_ref[...]).astype(BF16)
        gate = jax.nn.sigmoid(jnp.dot(hp, wg_ref[...], preferred_element_type=F32))
        emb = jnp.dot(p_ref[rows, :].astype(BF16), wp_ref[...], preferred_element_type=F32)
        y = x + gate * emb
        o_ref[rows, :] = _rmsnorm(y, gf_ref[...]) if final else y


def _ple(x2, p2d, gain_p, wpg, wpp, gain_f, *, final, tm=512):
    m, d = x2.shape
    pd = p2d.shape[1]
    return pl.pallas_call(
        functools.partial(_ple_kernel, final=final),
        grid=(m // tm,),
        in_specs=[
            pl.BlockSpec((tm, d), lambda i: (i, 0)),
            pl.BlockSpec((tm, pd), lambda i: (i, 0)),
            _resident(gain_p.shape), _resident(wpg.shape), _resident(wpp.shape),
            _resident(gain_f.shape),
        ],
        out_specs=pl.BlockSpec((tm, d), lambda i: (i, 0)),
        out_shape=jax.ShapeDtypeStruct((m, d), F32),
        compiler_params=_params("parallel"),
        name="ple",
    )(x2, p2d, gain_p, wpg, wpp, gain_f)


def kernel(x, p, rel_bias_table, attn_norm, w_in, sink_a, w_branch_a, w_branch_b, w_out,
           ffn_norm, w_ffn_gate, w_ffn_up, conv_w, conv_b, w_ffn_down,
           ple_norm, w_ple_gate, w_ple_proj, final_norm):
    b, s, d = x.shape
    depth = w_in.shape[0]
    assert d == D_MODEL and w_in.shape[2] == IN_PROJ_W and s % (Q_BLOCK * B_PATTERNS[-1][1]) == 0
    assert rel_bias_table.shape == (N_BUCKETS, N_BIAS_HEADS)
    m = b * s
    x2d = x.reshape(m, d)
    table_flat = rel_bias_table.reshape(-1)
    for i in range(depth):
        pb2d, pa2d = _inproj(x2d, attn_norm[i][None], w_in[i].astype(BF16))
        pb = pb2d.reshape(PB_W // HEAD_DIM, b, s, HEAD_DIM)
        ya = _attn_a(pb, table_flat, sink_a[i]).reshape(m, A_Q_W)
        yb = _attn_b(pb, table_flat).reshape(m, B_GROUP_W)
        x1, hf = _mix(ya, yb, pa2d, x2d, w_branch_a[i].astype(BF16), w_branch_b[i].astype(BF16),
                      w_out[i].astype(BF16), ffn_norm[i][None])
        act = _ffn_up(hf, w_ffn_gate[i], w_ffn_up[i], conv_w[i], conv_b[i][None], seq=s)
        x2 = _ffn_down(act, w_ffn_down[i].astype(BF16), x1)
        x2d = _ple(x2, p[i].reshape(m, -1), ple_norm[i][None], w_ple_gate[i].astype(BF16),
                   w_ple_proj[i].astype(BF16), final_norm[None], final=i == depth - 1)
    return x2d.reshape(b, s, d)
```

```python
import functools
import math

import jax
import jax.numpy as jnp
import numpy as np
from jax import lax
from jax.experimental import pallas as pl
from jax.experimental.pallas import tpu as pltpu

D_MODEL = 2048
HEAD_DIM = 128
A_Q_HEADS = 8
A_KV_HEADS = 2
A_GROUP = A_Q_HEADS // A_KV_HEADS
A_HALF_WINDOW = 128
B_PATTERNS = ((128, 1), (512, 4), (2048, 16))
B_HEADS_PER_GROUP = 4
N_BUCKETS = 32
MAX_DISTANCE = 1024
N_BIAS_HEADS = A_Q_HEADS + len(B_PATTERNS) * B_HEADS_PER_GROUP
A_Q_W = A_Q_HEADS * HEAD_DIM
A_KV_W = A_KV_HEADS * HEAD_DIM
B_GROUP_W = B_HEADS_PER_GROUP * HEAD_DIM
B_W = len(B_PATTERNS) * B_GROUP_W
IN_PROJ_W = A_Q_W + 2 * A_KV_W + 3 * B_W + 2 * D_MODEL
RMS_EPS = 1e-6
NEG_INF = -1e30
LOG2_E = math.log2(math.e)
QK_SCALE_LOG2 = HEAD_DIM ** -0.5 * LOG2_E

PB_W = A_Q_W + 2 * A_KV_W + 3 * B_W
PB_COL_KA = A_Q_W
PB_COL_VA = PB_COL_KA + A_KV_W
PB_COL_QB = PB_COL_VA + A_KV_W
PA_W = 2 * D_MODEL
PA_COL_GA = 0
PA_COL_GB = D_MODEL

Q_BLOCK = 128
ATTN_UNROLL = 2
STAGE_STRIDE = 4
CHAIN_GROUP = (16, 1, 16)
A_BLOCKS_PER_GROUP = 2
MERGE_ROWS = 256
FFN_ROW_PARTS = 2
MIX_ROW_PARTS = 2
PLE_ROW_PARTS = 1

VMEM_LIMIT_BYTES = 56 * 1024 * 1024

BF16 = jnp.bfloat16
F32 = jnp.float32


def _params(*semantics):
    return pltpu.CompilerParams(dimension_semantics=semantics, vmem_limit_bytes=VMEM_LIMIT_BYTES)


def _rmsnorm(x, g):
    y = x * lax.rsqrt(jnp.mean(x * x, axis=-1, keepdims=True) + RMS_EPS)
    return y * g


def _inproj_kernel(x_ref, g_ref, w_ref, pb_ref, pa_ref, h_ref, *, row_chunk, n_b_tiles):
    j = pl.program_id(1)

    def store_slabs(rows, res):
        for s in range(pb_ref.shape[0]):
            pb_ref[s, rows, :] = res[:, s * HEAD_DIM:(s + 1) * HEAD_DIM]

    @pl.when(j == 0)
    def _():
        for r in range(0, x_ref.shape[0], row_chunk):
            rows = slice(r, r + row_chunk)
            h = _rmsnorm(x_ref[rows, :], g_ref[...]).astype(BF16)
            h_ref[rows, :] = h
            store_slabs(rows, jnp.dot(h, w_ref[...], preferred_element_type=F32))

    @pl.when((j > 0) & (j < n_b_tiles))
    def _():
        store_slabs(slice(None), jnp.dot(h_ref[...], w_ref[...], preferred_element_type=F32))

    @pl.when(j >= n_b_tiles)
    def _():
        pa_ref[...] = jnp.dot(h_ref[...], w_ref[...], preferred_element_type=F32).astype(BF16)


def _inproj(x2d, gain, w, *, tm=1024, tn=1024):
    m, k = x2d.shape
    assert PB_W % tn == 0 and PA_W % tn == 0 and w.shape[1] == PB_W + PA_W
    nb_t = PB_W // tn
    n_steps = w.shape[1] // tn

    return pl.pallas_call(
        functools.partial(_inproj_kernel, row_chunk=256, n_b_tiles=nb_t),
        grid=(m // tm, n_steps),
        in_specs=[
            pl.BlockSpec((tm, k), lambda i, j: (i, 0)),
            pl.BlockSpec((1, k), lambda i, j: (0, 0)),
            pl.BlockSpec((k, tn), lambda i, j: (0, j)),
        ],
        out_specs=[
            pl.BlockSpec((tn // HEAD_DIM, tm, HEAD_DIM), lambda i, j: (jnp.minimum(j, nb_t - 1), i, 0)),
            pl.BlockSpec((tm, tn), lambda i, j: (i, jnp.maximum(j - nb_t, 0))),
        ],
        out_shape=[jax.ShapeDtypeStruct((PB_W // HEAD_DIM, m, HEAD_DIM), F32),
                   jax.ShapeDtypeStruct((m, PA_W), BF16)],
        scratch_shapes=[pltpu.VMEM((tm, k), BF16)],
        compiler_params=_params("arbitrary", "arbitrary"),
        name="inproj",
    )(x2d, gain, w)


def _t5_bucket_static(rel):
    half = N_BUCKETS // 2
    max_exact = half // 2
    n = np.abs(rel)
    side = np.where(rel > 0, half, 0)
    nf = np.maximum(n, 1).astype(np.float32)
    large = max_exact + (np.log(nf / max_exact) / math.log(MAX_DISTANCE / max_exact)
                         * (half - max_exact)).astype(np.int32)
    large = np.minimum(large, half - 1)
    return side + np.where(n < max_exact, n, large)


def _window_geometry(seq_len, half_w):
    nb = seq_len // Q_BLOCK
    if nb == 1:
        return seq_len, nb, (0,)
    kw = Q_BLOCK + 2 * half_w
    assert half_w <= Q_BLOCK and kw <= seq_len
    return kw, nb, (0, -half_w, Q_BLOCK - kw)


def _band_buckets(seq_len, half_w, dilation):
    kw, _, offsets = _window_geometry(seq_len, half_w)
    qi = np.arange(Q_BLOCK)[:, None]
    kj = np.arange(kw)[None, :]
    rel = np.stack([off + kj - qi for off in offsets])
    buckets = _t5_bucket_static(rel * dilation)
    return np.where(np.abs(rel) <= half_w, buckets, -1).astype(np.int32)


def _block_geometry(i, seq_len, half_w):
    kw, nb, _ = _window_geometry(seq_len, half_w)
    if isinstance(i, int):
        qs = i * Q_BLOCK
        return qs, min(max(qs - half_w, 0), seq_len - kw), 0 if i == 0 else (2 if i == nb - 1 else 1)
    qs = pl.multiple_of(i * Q_BLOCK, Q_BLOCK)
    ks = pl.multiple_of(jnp.clip(qs - half_w, 0, seq_len - kw), half_w)
    return qs, ks, jnp.where(i == 0, 0, jnp.where(i == nb - 1, 2, 1))


def _build_bias(bkt_ref, table_ref, bias_ref, slot, first_head, n_heads, buckets_used):
    for kind in range(bkt_ref.shape[0]):
        bkt = bkt_ref[kind]
        for h in range(n_heads):
            acc = jnp.full(bkt.shape, NEG_INF, F32)
            for bucket in buckets_used:
                acc = jnp.where(bkt == bucket,
                                table_ref[bucket * N_BIAS_HEADS + first_head + h] * LOG2_E, acc)
            bias_ref[slot, kind, h * Q_BLOCK:(h + 1) * Q_BLOCK, :] = acc


def _softmax_chains(items, *, load, store, bias, kw, sink=None, want_lse):
    ones = jnp.ones((kw, HEAD_DIM), BF16)
    scores = []
    for tag, qs, ks, kind in items:
        q = (load(0, tag, qs, Q_BLOCK) * QK_SCALE_LOG2).astype(BF16)
        s = lax.dot_general(q, load(1, tag, ks, kw).astype(BF16),
                            (((1,), (1,)), ((), ())), preferred_element_type=F32)
        scores.append(s + bias(tag, kind))
    for (tag, qs, ks, _), s in zip(items, scores):
        v_ext = jnp.concatenate([load(2, tag, ks, kw).astype(BF16), ones], axis=1)
        m = jnp.max(s, axis=-1, keepdims=True)
        if sink is not None:
            sk = sink(tag) * LOG2_E
            m = jnp.maximum(m, sk)
        p = jnp.exp2(s - m).astype(BF16)
        ov = jnp.dot(p, v_ext, preferred_element_type=F32)
        denom = ov[:, HEAD_DIM:]
        if sink is not None:
            denom = denom + jnp.exp2(sk - m)
        store(0, tag, qs, ov[:, :HEAD_DIM] / denom)
        if want_lse:
            store(1, tag, qs, m * (1.0 / LOG2_E) + jnp.log(denom))


def _attn_a_kernel(bkt_ref, table_ref, sink_ref, q_ref, k_ref, v_ref, o_ref, bias_ref, *,
                   seq_len, buckets_used):
    kv = pl.program_id(1)
    kw, nb, _ = _window_geometry(seq_len, A_HALF_WINDOW)

    @pl.when(pl.program_id(0) == 0)
    def _():
        _build_bias(bkt_ref, table_ref, bias_ref, kv, kv * A_GROUP, A_GROUP, buckets_used)

    def load(t, h, start, size):
        if t == 0:
            return q_ref[h, 0, pl.ds(start, size), :]
        return (k_ref, v_ref)[t - 1][0, 0, pl.ds(start, size), :]

    def store(t, h, start, value):
        o_ref[0, pl.ds(start, Q_BLOCK), h * HEAD_DIM:(h + 1) * HEAD_DIM] = value.astype(o_ref.dtype)

    def body(it, carry):
        items = [(h,) + _block_geometry(it * A_BLOCKS_PER_GROUP + di, seq_len, A_HALF_WINDOW)
                 for di in range(A_BLOCKS_PER_GROUP) for h in range(A_GROUP)]
        _softmax_chains(
            items, load=load, store=store, kw=kw, want_lse=False,
            bias=lambda h, kind: bias_ref[kv, kind, h * Q_BLOCK:(h + 1) * Q_BLOCK, :],
            sink=lambda h: jnp.full((Q_BLOCK, 1), sink_ref[kv * A_GROUP + h], F32))
        return carry

    assert nb % A_BLOCKS_PER_GROUP == 0
    lax.fori_loop(0, nb // A_BLOCKS_PER_GROUP, body, 0, unroll=ATTN_UNROLL)


def _buckets_used(bkt):
    return tuple(int(v) for v in np.unique(bkt) if v >= 0)


def _smem_spec():
    return pl.BlockSpec(memory_space=pltpu.SMEM)


def _attn_a(pb, table_flat, sink):
    _, b, s, _ = pb.shape
    gw = A_GROUP * HEAD_DIM
    kw, _, offsets = _window_geometry(s, A_HALF_WINDOW)
    bkt = _band_buckets(s, A_HALF_WINDOW, 1)
    kernel = functools.partial(_attn_a_kernel, seq_len=s, buckets_used=_buckets_used(bkt))
    return pl.pallas_call(
        kernel,
        grid=(b, A_KV_HEADS),
        in_specs=[
            pl.BlockSpec(bkt.shape, lambda i, j: (0, 0, 0)),
            _smem_spec(),
            _smem_spec(),
            pl.BlockSpec((A_GROUP, 1, s, HEAD_DIM), lambda i, j: (j, i, 0, 0)),
            pl.BlockSpec((1, 1, s, HEAD_DIM), lambda i, j: (PB_COL_KA // HEAD_DIM + j, i, 0, 0)),
            pl.BlockSpec((1, 1, s, HEAD_DIM), lambda i, j: (PB_COL_VA // HEAD_DIM + j, i, 0, 0)),
        ],
        out_specs=pl.BlockSpec((1, s, gw), lambda i, j: (i, 0, j)),
        out_shape=jax.ShapeDtypeStruct((b, s, A_Q_W), BF16),
        scratch_shapes=[pltpu.VMEM((A_KV_HEADS, len(offsets), A_GROUP * Q_BLOCK, kw), F32)],
        compiler_params=_params("arbitrary", "arbitrary"),
        name="attn_a",
    )(jnp.asarray(bkt), table_flat, sink, pb, pb, pb)


def _attn_b_kernel(*refs, seq, groups):
    ng = len(groups)
    it = iter(refs)
    bkt_refs = [next(it) for _ in range(ng)]
    table_ref = next(it)
    qkv_refs = [[next(it) for _ in range(3)] for _ in range(ng)]
    yb_ref = next(it)
    bias_refs = [next(it) for _ in range(ng)]
    o_acc, l_acc, stage_in, stage_out = next(it), next(it), next(it), next(it)
    head = pl.program_id(1)

    @pl.when(pl.program_id(0) == 0)
    def _():
        for g, grp in enumerate(groups):
            _build_bias(bkt_refs[g], table_ref, bias_refs[g], head, grp["head0"] + head, 1,
                        grp["buckets_used"])

    for g, grp in enumerate(groups):
        dil, sub, half, chain_group = grp["dil"], grp["sub"], grp["half"], grp["chain_group"]
        kw, nb, _ = _window_geometry(sub, half)
        stage = STAGE_STRIDE if dil > STAGE_STRIDE else 1
        hop2 = dil // stage
        accs = (o_acc, l_acc)

        def rows(residue, start, size, dil=dil, stage=stage, hop2=hop2):
            if dil == 1:
                return pl.ds(start, size)
            if stage > 1:
                return pl.ds(residue // stage + start * hop2, size, stride=hop2)
            return pl.ds(residue + start * dil, size, stride=dil)

        def load(t, residue, start, size, g=g, stage=stage, rows=rows):
            if stage > 1:
                return stage_in[t, residue % stage, rows(residue, start, size), :]
            return qkv_refs[g][t][0, 0, rows(residue, start, size), :]

        def store(t, residue, start, value, g=g, stage=stage, rows=rows, accs=accs):
            if stage > 1:
                stage_out[t, residue % stage, rows(residue, start, Q_BLOCK), :] = value
            else:
                accs[t][g, rows(residue, start, Q_BLOCK), :] = value

        def bias(residue, kind, g=g):
            return bias_refs[g][head, kind]

        run = functools.partial(_softmax_chains, load=load, store=store, bias=bias, kw=kw,
                                want_lse=True)
        if dil == 1:
            per_iter = min(chain_group, nb)
            assert nb % per_iter == 0

            def body(it_, carry, run=run, per_iter=per_iter, sub=sub, half=half):
                run([(0,) + _block_geometry(it_ * per_iter + di, sub, half) for di in range(per_iter)])
                return carry

            lax.fori_loop(0, nb // per_iter, body, 0)
            continue
        if stage > 1:
            for t in range(3):
                for r1 in range(stage):
                    stage_in[t, r1] = qkv_refs[g][t][0, 0, pl.ds(r1, seq // stage, stride=stage), :]
        work = [(r,) + _block_geometry(i, sub, half) for r in range(dil) for i in range(nb)]
        for c in range(0, len(work), chain_group):
            run(work[c:c + chain_group])
        if stage > 1:
            for t in range(2):
                for r1 in range(stage):
                    accs[t][g, pl.ds(r1, seq // stage, stride=stage), :] = stage_out[t, r1]

    def merge(c, carry):
        rws = pl.ds(pl.multiple_of(c * MERGE_ROWS, MERGE_ROWS), MERGE_ROWS)
        ls = [l_acc[g, rws, :] for g in range(ng)]
        mx = functools.reduce(jnp.maximum, ls)
        es = [jnp.exp(l - mx) for l in ls]
        tot = functools.reduce(lambda a, e: a + e, es)
        y = functools.reduce(lambda a, e: a + e, [e * o_acc[g, rws, :] for g, e in enumerate(es)])
        yb_ref[0, rws, :] = (y / tot).astype(yb_ref.dtype)
        return carry

    lax.fori_loop(0, seq // MERGE_ROWS, merge, 0)


def _attn_b(pb, table_flat):
    _, b, s, _ = pb.shape
    nh = B_HEADS_PER_GROUP
    groups, bkts, bias_scratch = [], [], []
    for gi, (window, dil) in enumerate(B_PATTERNS):
        sub, half = s // dil, window // (2 * dil)
        kw, _, offsets = _window_geometry(sub, half)
        bkt = _band_buckets(sub, half, dil)
        bkts.append(bkt)
        bias_scratch.append(pltpu.VMEM((nh, len(offsets), Q_BLOCK, kw), F32))
        groups.append(dict(dil=dil, sub=sub, half=half, chain_group=CHAIN_GROUP[gi],
                           head0=A_Q_HEADS + gi * nh, buckets_used=_buckets_used(bkt)))

    def qkv_spec(gi, part):
        base = (PB_COL_QB + part * B_W + gi * B_GROUP_W) // HEAD_DIM
        return pl.BlockSpec((1, 1, s, HEAD_DIM), lambda i, h: (base + h, i, 0, 0))

    ng = len(groups)
    return pl.pallas_call(
        functools.partial(_attn_b_kernel, seq=s, groups=groups),
        grid=(b, nh),
        in_specs=[pl.BlockSpec(bkt.shape, lambda i, h: (0, 0, 0)) for bkt in bkts] + [_smem_spec()]
        + [qkv_spec(gi, part) for gi in range(ng) for part in range(3)],
        out_specs=pl.BlockSpec((1, s, HEAD_DIM), lambda i, h: (i, 0, h)),
        out_shape=jax.ShapeDtypeStruct((b, s, B_GROUP_W), BF16),
        scratch_shapes=bias_scratch + [
            pltpu.VMEM((ng, s, HEAD_DIM), F32), pltpu.VMEM((ng, s, HEAD_DIM), F32),
            pltpu.VMEM((3, STAGE_STRIDE, s // STAGE_STRIDE, HEAD_DIM), F32),
            pltpu.VMEM((2, STAGE_STRIDE, s // STAGE_STRIDE, HEAD_DIM), F32)],
        compiler_params=_params("arbitrary", "arbitrary"),
        name="attn_b",
    )(*[jnp.asarray(bkt) for bkt in bkts], table_flat, *([pb] * (3 * ng)))


def _mix_kernel(ya_ref, yb_ref, ga_ref, gb_ref, x_ref, wa_ref, wb_ref, wo_ref, g_ref, x1_ref, hf_ref):
    rp = x_ref.shape[0] // MIX_ROW_PARTS
    for r in range(0, x_ref.shape[0], rp):
        rows = slice(r, r + rp)
        ta = jnp.dot(ya_ref[rows, :], wa_ref[...], preferred_element_type=F32)
        tb = jnp.dot(yb_ref[rows, :], wb_ref[...], preferred_element_type=F32)
        merged = (jax.nn.sigmoid(ga_ref[rows, :].astype(F32)) * ta
                  + jax.nn.sigmoid(gb_ref[rows, :].astype(F32)) * tb)
        x1 = x_ref[rows, :] + jnp.dot(merged.astype(BF16), wo_ref[...], preferred_element_type=F32)
        x1_ref[rows, :] = x1
        hf_ref[rows, :] = _rmsnorm(x1, g_ref[...]).astype(BF16)


def _resident(shape):
    return pl.BlockSpec(shape, lambda i: (0,) * len(shape), pipeline_mode=pl.Buffered(1))


def _mix(ya, yb, pa2d, x2d, wa, wb, wo, gain, *, tm=512):
    m, d = x2d.shape
    assert PA_COL_GA % d == 0 and PA_COL_GB % d == 0

    def rows(width, col_block=0):
        return pl.BlockSpec((tm, width), lambda i: (i, col_block))

    return pl.pallas_call(
        _mix_kernel,
        grid=(m // tm,),
        in_specs=[rows(A_Q_W), rows(B_GROUP_W),
                  rows(d, PA_COL_GA // d), rows(d, PA_COL_GB // d), rows(d),
                  _resident(wa.shape), _resident(wb.shape), _resident(wo.shape), _resident(gain.shape)],
        out_specs=[rows(d), rows(d)],
        out_shape=[jax.ShapeDtypeStruct((m, d), F32), jax.ShapeDtypeStruct((m, d), BF16)],
        compiler_params=_params("parallel"),
        name="mix",
    )(ya, yb, pa2d, pa2d, x2d, wa, wb, wo, gain)


def _ffn_up_kernel(hf_ref, wg_ref, wu_ref, cw_ref, cb_ref, wn_ref, act_ref, wn_out_ref):
    wn_out_ref[...] = wn_ref[...].astype(BF16)
    wg = wg_ref[...].astype(BF16)
    wu = wu_ref[...].astype(BF16)
    rp = hf_ref.shape[0] // FFN_ROW_PARTS
    parts = [(jnp.dot(hf_ref[r:r + rp, :], wg, preferred_element_type=F32),
              jnp.dot(hf_ref[r:r + rp, :], wu, preferred_element_type=F32))
             for r in range(0, hf_ref.shape[0], rp)]
    g = jnp.concatenate([p[0] for p in parts], axis=0)
    u = jnp.concatenate([p[1] for p in parts], axis=0)
    s = g.shape[0]
    row = lax.broadcasted_iota(jnp.int32, g.shape, 0)
    prev = jnp.where(row == 0, 0.0, pltpu.roll(g, 1, 0))
    nxt = jnp.where(row == s - 1, 0.0, pltpu.roll(g, s - 1, 0))
    cw = cw_ref[...]
    conv = prev * cw[0:1] + g * cw[1:2] + nxt * cw[2:3] + cb_ref[...]
    act_ref[...] = (jax.nn.gelu(conv) * u).astype(act_ref.dtype)


def _cast_rider(w, n_steps, step_of):
    rows = w.shape[0] // n_steps
    assert rows * n_steps == w.shape[0] and rows % 16 == 0
    spec = pl.BlockSpec((rows, w.shape[1]), lambda *ids: (step_of(*ids), 0))
    return spec, spec, jax.ShapeDtypeStruct(w.shape, BF16)


def _ffn_up(hf, wg, wu, cw, cb, w_next, *, seq, tf=512):
    m, d = hf.shape
    f = wg.shape[1]
    nj = f // tf
    wn_in, wn_out, wn_shape = _cast_rider(w_next, (m // seq) * nj, lambda i, j: i * nj + j)
    return pl.pallas_call(
        _ffn_up_kernel,
        grid=(m // seq, f // tf),
        in_specs=[
            pl.BlockSpec((seq, d), lambda i, j: (i, 0)),
            pl.BlockSpec((d, tf), lambda i, j: (0, j)),
            pl.BlockSpec((d, tf), lambda i, j: (0, j)),
            pl.BlockSpec((cw.shape[0], tf), lambda i, j: (0, j)),
            pl.BlockSpec((1, tf), lambda i, j: (0, j)),
            wn_in,
        ],
        out_specs=[pl.BlockSpec((seq, tf), lambda i, j: (i, j)), wn_out],
        out_shape=[jax.ShapeDtypeStruct((m, f), BF16), wn_shape],
        compiler_params=_params("parallel", "arbitrary"),
        name="ffn_up",
    )(hf, wg, wu, cw, cb, w_next)


def _ffn_down_kernel(act_ref, w_ref, x_ref, wn_ref, o_ref, wn_out_ref):
    wn_out_ref[...] = wn_ref[...].astype(BF16)
    o_ref[...] = x_ref[...] + jnp.dot(act_ref[...], w_ref[...], preferred_element_type=F32)


def _ffn_down(act, wd, x1, w_next, *, tm=512, tn=1024):
    m, f = act.shape
    n = wd.shape[1]
    ni = m // tm
    wn_in, wn_out, wn_shape = _cast_rider(w_next, (n // tn) * ni, lambda j, i: j * ni + i)
    return pl.pallas_call(
        _ffn_down_kernel,
        grid=(n // tn, m // tm),
        in_specs=[
            pl.BlockSpec((tm, f), lambda j, i: (i, 0)),
            pl.BlockSpec((f, tn), lambda j, i: (0, j)),
            pl.BlockSpec((tm, tn), lambda j, i: (i, j)),
            wn_in,
        ],
        out_specs=[pl.BlockSpec((tm, tn), lambda j, i: (i, j)), wn_out],
        out_shape=[jax.ShapeDtypeStruct((m, n), F32), wn_shape],
        compiler_params=_params("parallel", "parallel"),
        name="ffn_down",
    )(act, wd, x1, w_next)


def _ple_kernel(x_ref, p_ref, gp_ref, wg_ref, wp_ref, gf_ref, o_ref, *, final):
    rp = x_ref.shape[0] // PLE_ROW_PARTS
    for r in range(0, x_ref.shape[0], rp):
        rows = slice(r, r + rp)
        x = x_ref[rows, :]
        hp = _rmsnorm(x, gp_ref[...]).astype(BF16)
        gate = jax.nn.sigmoid(jnp.dot(hp, wg_ref[...], preferred_element_type=F32))
        emb = jnp.dot(p_ref[rows, :].astype(BF16), wp_ref[...], preferred_element_type=F32)
        y = x + gate * emb
        o_ref[rows, :] = _rmsnorm(y, gf_ref[...]) if final else y


def _ple(x2, p2d, gain_p, wpg, wpp, gain_f, *, final, tm=512):
    m, d = x2.shape
    pd = p2d.shape[1]
    return pl.pallas_call(
        functools.partial(_ple_kernel, final=final),
        grid=(m // tm,),
        in_specs=[
            pl.BlockSpec((tm, d), lambda i: (i, 0)),
            pl.BlockSpec((tm, pd), lambda i: (i, 0)),
            _resident(gain_p.shape), _resident(wpg.shape), _resident(wpp.shape),
            _resident(gain_f.shape),
        ],
        out_specs=pl.BlockSpec((tm, d), lambda i: (i, 0)),
        out_shape=jax.ShapeDtypeStruct((m, d), F32),
        compiler_params=_params("parallel"),
        name="ple",
    )(x2, p2d, gain_p, wpg, wpp, gain_f)


def kernel(x, p, rel_bias_table, attn_norm, w_in, sink_a, w_branch_a, w_branch_b, w_out,
           ffn_norm, w_ffn_gate, w_ffn_up, conv_w, conv_b, w_ffn_down,
           ple_norm, w_ple_gate, w_ple_proj, final_norm):
    b, s, d = x.shape
    depth = w_in.shape[0]
    assert d == D_MODEL and w_in.shape[2] == IN_PROJ_W and s % (Q_BLOCK * B_PATTERNS[-1][1]) == 0
    assert rel_bias_table.shape == (N_BUCKETS, N_BIAS_HEADS)
    m = b * s
    x2d = x.reshape(m, d)
    table_flat = rel_bias_table.reshape(-1)
    for i in range(depth):
        pb2d, pa2d = _inproj(x2d, attn_norm[i][None], w_in[i].astype(BF16))
        pb = pb2d.reshape(PB_W // HEAD_DIM, b, s, HEAD_DIM)
        ya = _attn_a(pb, table_flat, sink_a[i]).reshape(m, A_Q_W)
        yb = _attn_b(pb, table_flat).reshape(m, B_GROUP_W)
        x1, hf = _mix(ya, yb, pa2d, x2d, w_branch_a[i].astype(BF16), w_branch_b[i].astype(BF16),
                      w_out[i].astype(BF16), ffn_norm[i][None])
        act, wd = _ffn_up(hf, w_ffn_gate[i], w_ffn_up[i], conv_w[i], conv_b[i][None], w_ffn_down[i],
                          seq=s)
        x2, wpg = _ffn_down(act, wd, x1, w_ple_gate[i])
        x2d = _ple(x2, p[i].reshape(m, -1), ple_norm[i][None], wpg,
                   w_ple_proj[i].astype(BF16), final_norm[None], final=i == depth - 1)
    return x2d.reshape(b, s, d)
```

```python
import functools
import math

import jax
import jax.numpy as jnp
import numpy as np
from jax import lax
from jax.experimental import pallas as pl
from jax.experimental.pallas import tpu as pltpu

D_MODEL = 2048
HEAD_DIM = 128
A_Q_HEADS = 8
A_KV_HEADS = 2
A_GROUP = A_Q_HEADS // A_KV_HEADS
A_HALF_WINDOW = 128
B_PATTERNS = ((128, 1), (512, 4), (2048, 16))
B_HEADS_PER_GROUP = 4
N_BUCKETS = 32
MAX_DISTANCE = 1024
N_BIAS_HEADS = A_Q_HEADS + len(B_PATTERNS) * B_HEADS_PER_GROUP
A_Q_W = A_Q_HEADS * HEAD_DIM
A_KV_W = A_KV_HEADS * HEAD_DIM
B_GROUP_W = B_HEADS_PER_GROUP * HEAD_DIM
B_W = len(B_PATTERNS) * B_GROUP_W
IN_PROJ_W = A_Q_W + 2 * A_KV_W + 3 * B_W + 2 * D_MODEL
RMS_EPS = 1e-6
NEG_INF = -1e30
LOG2_E = math.log2(math.e)
QK_SCALE_LOG2 = HEAD_DIM ** -0.5 * LOG2_E

PB_W = A_Q_W + 2 * A_KV_W + 3 * B_W
PB_COL_KA = A_Q_W
PB_COL_VA = PB_COL_KA + A_KV_W
PB_COL_QB = PB_COL_VA + A_KV_W
PA_W = 2 * D_MODEL
PA_COL_GA = 0
PA_COL_GB = D_MODEL

Q_BLOCK = 128
ATTN_UNROLL = 2
STAGE_STRIDE = 4
CHAIN_GROUP = (16, 1, 16)
A_BLOCKS_PER_GROUP = 2
MERGE_ROWS = 256
FFN_ROW_PARTS = 2
MIX_ROW_PARTS = 2
PLE_ROW_PARTS = 1

VMEM_LIMIT_BYTES = 56 * 1024 * 1024

BF16 = jnp.bfloat16
F32 = jnp.float32


def _params(*semantics):
    return pltpu.CompilerParams(dimension_semantics=semantics, vmem_limit_bytes=VMEM_LIMIT_BYTES)


def _rmsnorm(x, g):
    y = x * lax.rsqrt(jnp.mean(x * x, axis=-1, keepdims=True) + RMS_EPS)
    return y * g


def _cast_rider(w, n_steps, step_of):
    rows = w.shape[0] // n_steps
    assert rows * n_steps == w.shape[0] and rows % 16 == 0
    spec = pl.BlockSpec((rows, w.shape[1]), lambda *ids: (step_of(*ids), 0))
    return spec, spec, jax.ShapeDtypeStruct(w.shape, BF16)


def _inproj_kernel(x_ref, g_ref, w_ref, pb_ref, pa_ref, h_ref, *, row_chunk, n_b_tiles):
    j = pl.program_id(1)

    def store_slabs(rows, res):
        for s in range(pb_ref.shape[0]):
            pb_ref[s, rows, :] = res[:, s * HEAD_DIM:(s + 1) * HEAD_DIM]

    @pl.when(j == 0)
    def _():
        for r in range(0, x_ref.shape[0], row_chunk):
            rows = slice(r, r + row_chunk)
            h = _rmsnorm(x_ref[rows, :], g_ref[...]).astype(BF16)
            h_ref[rows, :] = h
            store_slabs(rows, jnp.dot(h, w_ref[...], preferred_element_type=F32))

    @pl.when((j > 0) & (j < n_b_tiles))
    def _():
        store_slabs(slice(None), jnp.dot(h_ref[...], w_ref[...], preferred_element_type=F32))

    @pl.when(j >= n_b_tiles)
    def _():
        pa_ref[...] = jnp.dot(h_ref[...], w_ref[...], preferred_element_type=F32).astype(BF16)


def _inproj(x2d, gain, w, *, tm=1024, tn=1024):
    m, k = x2d.shape
    assert PB_W % tn == 0 and PA_W % tn == 0 and w.shape[1] == PB_W + PA_W
    nb_t = PB_W // tn
    n_steps = w.shape[1] // tn

    return pl.pallas_call(
        functools.partial(_inproj_kernel, row_chunk=256, n_b_tiles=nb_t),
        grid=(m // tm, n_steps),
        in_specs=[
            pl.BlockSpec((tm, k), lambda i, j: (i, 0)),
            pl.BlockSpec((1, k), lambda i, j: (0, 0)),
            pl.BlockSpec((k, tn), lambda i, j: (0, j)),
        ],
        out_specs=[
            pl.BlockSpec((tn // HEAD_DIM, tm, HEAD_DIM), lambda i, j: (jnp.minimum(j, nb_t - 1), i, 0)),
            pl.BlockSpec((tm, tn), lambda i, j: (i, jnp.maximum(j - nb_t, 0))),
        ],
        out_shape=[jax.ShapeDtypeStruct((PB_W // HEAD_DIM, m, HEAD_DIM), F32),
                   jax.ShapeDtypeStruct((m, PA_W), BF16)],
        scratch_shapes=[pltpu.VMEM((tm, k), BF16)],
        compiler_params=_params("arbitrary", "arbitrary"),
        name="inproj",
    )(x2d, gain, w)


def _t5_bucket_static(rel):
    half = N_BUCKETS // 2
    max_exact = half // 2
    n = np.abs(rel)
    side = np.where(rel > 0, half, 0)
    nf = np.maximum(n, 1).astype(np.float32)
    large = max_exact + (np.log(nf / max_exact) / math.log(MAX_DISTANCE / max_exact)
                         * (half - max_exact)).astype(np.int32)
    large = np.minimum(large, half - 1)
    return side + np.where(n < max_exact, n, large)


def _window_geometry(seq_len, half_w):
    nb = seq_len // Q_BLOCK
    if nb == 1:
        return seq_len, nb, (0,)
    kw = Q_BLOCK + 2 * half_w
    assert half_w <= Q_BLOCK and kw <= seq_len
    return kw, nb, (0, -half_w, Q_BLOCK - kw)


def _band_buckets(seq_len, half_w, dilation):
    kw, _, offsets = _window_geometry(seq_len, half_w)
    qi = np.arange(Q_BLOCK)[:, None]
    kj = np.arange(kw)[None, :]
    rel = np.stack([off + kj - qi for off in offsets])
    buckets = _t5_bucket_static(rel * dilation)
    return np.where(np.abs(rel) <= half_w, buckets, -1).astype(np.int32)


def _block_geometry(i, seq_len, half_w):
    kw, nb, _ = _window_geometry(seq_len, half_w)
    if isinstance(i, int):
        qs = i * Q_BLOCK
        return qs, min(max(qs - half_w, 0), seq_len - kw), 0 if i == 0 else (2 if i == nb - 1 else 1)
    qs = pl.multiple_of(i * Q_BLOCK, Q_BLOCK)
    ks = pl.multiple_of(jnp.clip(qs - half_w, 0, seq_len - kw), half_w)
    return qs, ks, jnp.where(i == 0, 0, jnp.where(i == nb - 1, 2, 1))


def _build_bias(bkt_ref, table_ref, bias_ref, slot, first_head, n_heads, buckets_used):
    for kind in range(bkt_ref.shape[0]):
        bkt = bkt_ref[kind]
        for h in range(n_heads):
            acc = jnp.full(bkt.shape, NEG_INF, F32)
            for bucket in buckets_used:
                acc = jnp.where(bkt == bucket,
                                table_ref[bucket * N_BIAS_HEADS + first_head + h] * LOG2_E, acc)
            bias_ref[slot, kind, h * Q_BLOCK:(h + 1) * Q_BLOCK, :] = acc


def _softmax_chains(items, *, load, store, bias, kw, sink=None, want_lse):
    ones = jnp.ones((kw, HEAD_DIM), BF16)
    scores = []
    for tag, qs, ks, kind in items:
        q = (load(0, tag, qs, Q_BLOCK) * QK_SCALE_LOG2).astype(BF16)
        s = lax.dot_general(q, load(1, tag, ks, kw).astype(BF16),
                            (((1,), (1,)), ((), ())), preferred_element_type=F32)
        scores.append(s + bias(tag, kind))
    for (tag, qs, ks, _), s in zip(items, scores):
        v_ext = jnp.concatenate([load(2, tag, ks, kw).astype(BF16), ones], axis=1)
        m = jnp.max(s, axis=-1, keepdims=True)
        if sink is not None:
            sk = sink(tag) * LOG2_E
            m = jnp.maximum(m, sk)
        p = jnp.exp2(s - m).astype(BF16)
        ov = jnp.dot(p, v_ext, preferred_element_type=F32)
        denom = ov[:, HEAD_DIM:]
        if sink is not None:
            denom = denom + jnp.exp2(sk - m)
        store(0, tag, qs, ov[:, :HEAD_DIM] / denom)
        if want_lse:
            store(1, tag, qs, m * (1.0 / LOG2_E) + jnp.log(denom))


def _attn_a_kernel(bkt_ref, table_ref, sink_ref, q_ref, k_ref, v_ref, wn_ref, o_ref, wn_out_ref,
                   bias_ref, *, seq_len, buckets_used):
    kv = pl.program_id(1)
    kw, nb, _ = _window_geometry(seq_len, A_HALF_WINDOW)
    wn_out_ref[...] = wn_ref[...].astype(BF16)

    @pl.when(pl.program_id(0) == 0)
    def _():
        _build_bias(bkt_ref, table_ref, bias_ref, kv, kv * A_GROUP, A_GROUP, buckets_used)

    def load(t, h, start, size):
        if t == 0:
            return q_ref[h, 0, pl.ds(start, size), :]
        return (k_ref, v_ref)[t - 1][0, 0, pl.ds(start, size), :]

    def store(t, h, start, value):
        o_ref[0, pl.ds(start, Q_BLOCK), h * HEAD_DIM:(h + 1) * HEAD_DIM] = value.astype(o_ref.dtype)

    def body(it, carry):
        items = [(h,) + _block_geometry(it * A_BLOCKS_PER_GROUP + di, seq_len, A_HALF_WINDOW)
                 for di in range(A_BLOCKS_PER_GROUP) for h in range(A_GROUP)]
        _softmax_chains(
            items, load=load, store=store, kw=kw, want_lse=False,
            bias=lambda h, kind: bias_ref[kv, kind, h * Q_BLOCK:(h + 1) * Q_BLOCK, :],
            sink=lambda h: jnp.full((Q_BLOCK, 1), sink_ref[kv * A_GROUP + h], F32))
        return carry

    assert nb % A_BLOCKS_PER_GROUP == 0
    lax.fori_loop(0, nb // A_BLOCKS_PER_GROUP, body, 0, unroll=ATTN_UNROLL)


def _buckets_used(bkt):
    return tuple(int(v) for v in np.unique(bkt) if v >= 0)


def _smem_spec():
    return pl.BlockSpec(memory_space=pltpu.SMEM)


def _attn_a(pb, table_flat, sink, w_next):
    _, b, s, _ = pb.shape
    wn_in, wn_out, wn_shape = _cast_rider(w_next, b * A_KV_HEADS, lambda i, j: i * A_KV_HEADS + j)
    gw = A_GROUP * HEAD_DIM
    kw, _, offsets = _window_geometry(s, A_HALF_WINDOW)
    bkt = _band_buckets(s, A_HALF_WINDOW, 1)
    kernel = functools.partial(_attn_a_kernel, seq_len=s, buckets_used=_buckets_used(bkt))
    return pl.pallas_call(
        kernel,
        grid=(b, A_KV_HEADS),
        in_specs=[
            pl.BlockSpec(bkt.shape, lambda i, j: (0, 0, 0)),
            _smem_spec(),
            _smem_spec(),
            pl.BlockSpec((A_GROUP, 1, s, HEAD_DIM), lambda i, j: (j, i, 0, 0)),
            pl.BlockSpec((1, 1, s, HEAD_DIM), lambda i, j: (PB_COL_KA // HEAD_DIM + j, i, 0, 0)),
            pl.BlockSpec((1, 1, s, HEAD_DIM), lambda i, j: (PB_COL_VA // HEAD_DIM + j, i, 0, 0)),
            wn_in,
        ],
        out_specs=[pl.BlockSpec((1, s, gw), lambda i, j: (i, 0, j)), wn_out],
        out_shape=[jax.ShapeDtypeStruct((b, s, A_Q_W), BF16), wn_shape],
        scratch_shapes=[pltpu.VMEM((A_KV_HEADS, len(offsets), A_GROUP * Q_BLOCK, kw), F32)],
        compiler_params=_params("arbitrary", "arbitrary"),
        name="attn_a",
    )(jnp.asarray(bkt), table_flat, sink, pb, pb, pb, w_next)


def _attn_b_kernel(*refs, seq, groups):
    ng = len(groups)
    it = iter(refs)
    bkt_refs = [next(it) for _ in range(ng)]
    table_ref = next(it)
    qkv_refs = [[next(it) for _ in range(3)] for _ in range(ng)]
    wn_ref, yb_ref, wn_out_ref = next(it), next(it), next(it)
    bias_refs = [next(it) for _ in range(ng)]
    o_acc, l_acc, stage_in, stage_out = next(it), next(it), next(it), next(it)
    head = pl.program_id(1)
    wn_out_ref[...] = wn_ref[...].astype(BF16)

    @pl.when(pl.program_id(0) == 0)
    def _():
        for g, grp in enumerate(groups):
            _build_bias(bkt_refs[g], table_ref, bias_refs[g], head, grp["head0"] + head, 1,
                        grp["buckets_used"])

    for g, grp in enumerate(groups):
        dil, sub, half, chain_group = grp["dil"], grp["sub"], grp["half"], grp["chain_group"]
        kw, nb, _ = _window_geometry(sub, half)
        stage = STAGE_STRIDE if dil > STAGE_STRIDE else 1
        hop2 = dil // stage
        accs = (o_acc, l_acc)

        def rows(residue, start, size, dil=dil, stage=stage, hop2=hop2):
            if dil == 1:
                return pl.ds(start, size)
            if stage > 1:
                return pl.ds(residue // stage + start * hop2, size, stride=hop2)
            return pl.ds(residue + start * dil, size, stride=dil)

        def load(t, residue, start, size, g=g, stage=stage, rows=rows):
            if stage > 1:
                return stage_in[t, residue % stage, rows(residue, start, size), :]
            return qkv_refs[g][t][0, 0, rows(residue, start, size), :]

        def store(t, residue, start, value, g=g, stage=stage, rows=rows, accs=accs):
            if stage > 1:
                stage_out[t, residue % stage, rows(residue, start, Q_BLOCK), :] = value
            else:
                accs[t][g, rows(residue, start, Q_BLOCK), :] = value

        def bias(residue, kind, g=g):
            return bias_refs[g][head, kind]

        run = functools.partial(_softmax_chains, load=load, store=store, bias=bias, kw=kw,
                                want_lse=True)
        if dil == 1:
            per_iter = min(chain_group, nb)
            assert nb % per_iter == 0

            def body(it_, carry, run=run, per_iter=per_iter, sub=sub, half=half):
                run([(0,) + _block_geometry(it_ * per_iter + di, sub, half) for di in range(per_iter)])
                return carry

            lax.fori_loop(0, nb // per_iter, body, 0)
            continue
        if stage > 1:
            for t in range(3):
                for r1 in range(stage):
                    stage_in[t, r1] = qkv_refs[g][t][0, 0, pl.ds(r1, seq // stage, stride=stage), :]
        work = [(r,) + _block_geometry(i, sub, half) for r in range(dil) for i in range(nb)]
        for c in range(0, len(work), chain_group):
            run(work[c:c + chain_group])
        if stage > 1:
            for t in range(2):
                for r1 in range(stage):
                    accs[t][g, pl.ds(r1, seq // stage, stride=stage), :] = stage_out[t, r1]

    def merge(c, carry):
        rws = pl.ds(pl.multiple_of(c * MERGE_ROWS, MERGE_ROWS), MERGE_ROWS)
        ls = [l_acc[g, rws, :] for g in range(ng)]
        mx = functools.reduce(jnp.maximum, ls)
        es = [jnp.exp(l - mx) for l in ls]
        tot = functools.reduce(lambda a, e: a + e, es)
        y = functools.reduce(lambda a, e: a + e, [e * o_acc[g, rws, :] for g, e in enumerate(es)])
        yb_ref[0, rws, :] = (y / tot).astype(yb_ref.dtype)
        return carry

    lax.fori_loop(0, seq // MERGE_ROWS, merge, 0)


def _attn_b(pb, table_flat, w_next):
    _, b, s, _ = pb.shape
    nh = B_HEADS_PER_GROUP
    wn_in, wn_out, wn_shape = _cast_rider(w_next, b * nh, lambda i, h: i * nh + h)
    groups, bkts, bias_scratch = [], [], []
    for gi, (window, dil) in enumerate(B_PATTERNS):
        sub, half = s // dil, window // (2 * dil)
        kw, _, offsets = _window_geometry(sub, half)
        bkt = _band_buckets(sub, half, dil)
        bkts.append(bkt)
        bias_scratch.append(pltpu.VMEM((nh, len(offsets), Q_BLOCK, kw), F32))
        groups.append(dict(dil=dil, sub=sub, half=half, chain_group=CHAIN_GROUP[gi],
                           head0=A_Q_HEADS + gi * nh, buckets_used=_buckets_used(bkt)))

    def qkv_spec(gi, part):
        base = (PB_COL_QB + part * B_W + gi * B_GROUP_W) // HEAD_DIM
        return pl.BlockSpec((1, 1, s, HEAD_DIM), lambda i, h: (base + h, i, 0, 0))

    ng = len(groups)
    return pl.pallas_call(
        functools.partial(_attn_b_kernel, seq=s, groups=groups),
        grid=(b, nh),
        in_specs=[pl.BlockSpec(bkt.shape, lambda i, h: (0, 0, 0)) for bkt in bkts] + [_smem_spec()]
        + [qkv_spec(gi, part) for gi in range(ng) for part in range(3)] + [wn_in],
        out_specs=[pl.BlockSpec((1, s, HEAD_DIM), lambda i, h: (i, 0, h)), wn_out],
        out_shape=[jax.ShapeDtypeStruct((b, s, B_GROUP_W), BF16), wn_shape],
        scratch_shapes=bias_scratch + [
            pltpu.VMEM((ng, s, HEAD_DIM), F32), pltpu.VMEM((ng, s, HEAD_DIM), F32),
            pltpu.VMEM((3, STAGE_STRIDE, s // STAGE_STRIDE, HEAD_DIM), F32),
            pltpu.VMEM((2, STAGE_STRIDE, s // STAGE_STRIDE, HEAD_DIM), F32)],
        compiler_params=_params("arbitrary", "arbitrary"),
        name="attn_b",
    )(*[jnp.asarray(bkt) for bkt in bkts], table_flat, *([pb] * (3 * ng)), w_next)


def _mix_kernel(ya_ref, yb_ref, ga_ref, gb_ref, x_ref, wa_ref, wb_ref, wo_ref, g_ref, x1_ref, hf_ref):
    rp = x_ref.shape[0] // MIX_ROW_PARTS
    for r in range(0, x_ref.shape[0], rp):
        rows = slice(r, r + rp)
        ta = jnp.dot(ya_ref[rows, :], wa_ref[...], preferred_element_type=F32)
        tb = jnp.dot(yb_ref[rows, :], wb_ref[...], preferred_element_type=F32)
        merged = (jax.nn.sigmoid(ga_ref[rows, :].astype(F32)) * ta
                  + jax.nn.sigmoid(gb_ref[rows, :].astype(F32)) * tb)
        x1 = x_ref[rows, :] + jnp.dot(merged.astype(BF16), wo_ref[...], preferred_element_type=F32)
        x1_ref[rows, :] = x1
        hf_ref[rows, :] = _rmsnorm(x1, g_ref[...]).astype(BF16)


def _resident(shape):
    return pl.BlockSpec(shape, lambda i: (0,) * len(shape), pipeline_mode=pl.Buffered(1))


def _mix(ya, yb, pa2d, x2d, wa, wb, wo, gain, *, tm=512):
    m, d = x2d.shape
    assert PA_COL_GA % d == 0 and PA_COL_GB % d == 0

    def rows(width, col_block=0):
        return pl.BlockSpec((tm, width), lambda i: (i, col_block))

    return pl.pallas_call(
        _mix_kernel,
        grid=(m // tm,),
        in_specs=[rows(A_Q_W), rows(B_GROUP_W),
                  rows(d, PA_COL_GA // d), rows(d, PA_COL_GB // d), rows(d),
                  _resident(wa.shape), _resident(wb.shape), _resident(wo.shape), _resident(gain.shape)],
        out_specs=[rows(d), rows(d)],
        out_shape=[jax.ShapeDtypeStruct((m, d), F32), jax.ShapeDtypeStruct((m, d), BF16)],
        compiler_params=_params("parallel"),
        name="mix",
    )(ya, yb, pa2d, pa2d, x2d, wa, wb, wo, gain)


def _ffn_up_kernel(hf_ref, wg_ref, wu_ref, cw_ref, cb_ref, wn_ref, act_ref, wn_out_ref):
    wn_out_ref[...] = wn_ref[...].astype(BF16)
    wg = wg_ref[...].astype(BF16)
    wu = wu_ref[...].astype(BF16)
    rp = hf_ref.shape[0] // FFN_ROW_PARTS
    parts = [(jnp.dot(hf_ref[r:r + rp, :], wg, preferred_element_type=F32),
              jnp.dot(hf_ref[r:r + rp, :], wu, preferred_element_type=F32))
             for r in range(0, hf_ref.shape[0], rp)]
    g = jnp.concatenate([p[0] for p in parts], axis=0)
    u = jnp.concatenate([p[1] for p in parts], axis=0)
    s = g.shape[0]
    row = lax.broadcasted_iota(jnp.int32, g.shape, 0)
    prev = jnp.where(row == 0, 0.0, pltpu.roll(g, 1, 0))
    nxt = jnp.where(row == s - 1, 0.0, pltpu.roll(g, s - 1, 0))
    cw = cw_ref[...]
    conv = prev * cw[0:1] + g * cw[1:2] + nxt * cw[2:3] + cb_ref[...]
    act_ref[...] = (jax.nn.gelu(conv) * u).astype(act_ref.dtype)


def _ffn_up(hf, wg, wu, cw, cb, w_next, *, seq, tf=512):
    m, d = hf.shape
    f = wg.shape[1]
    nj = f // tf
    wn_in, wn_out, wn_shape = _cast_rider(w_next, (m // seq) * nj, lambda i, j: i * nj + j)
    return pl.pallas_call(
        _ffn_up_kernel,
        grid=(m // seq, f // tf),
        in_specs=[
            pl.BlockSpec((seq, d), lambda i, j: (i, 0)),
            pl.BlockSpec((d, tf), lambda i, j: (0, j)),
            pl.BlockSpec((d, tf), lambda i, j: (0, j)),
            pl.BlockSpec((cw.shape[0], tf), lambda i, j: (0, j)),
            pl.BlockSpec((1, tf), lambda i, j: (0, j)),
            wn_in,
        ],
        out_specs=[pl.BlockSpec((seq, tf), lambda i, j: (i, j)), wn_out],
        out_shape=[jax.ShapeDtypeStruct((m, f), BF16), wn_shape],
        compiler_params=_params("parallel", "arbitrary"),
        name="ffn_up",
    )(hf, wg, wu, cw, cb, w_next)


def _ffn_down_kernel(act_ref, w_ref, x_ref, wn_ref, o_ref, wn_out_ref):
    wn_out_ref[...] = wn_ref[...].astype(BF16)
    o_ref[...] = x_ref[...] + jnp.dot(act_ref[...], w_ref[...], preferred_element_type=F32)


def _ffn_down(act, wd, x1, w_next, *, tm=512, tn=1024):
    m, f = act.shape
    n = wd.shape[1]
    ni = m // tm
    wn_in, wn_out, wn_shape = _cast_rider(w_next, (n // tn) * ni, lambda j, i: j * ni + i)
    return pl.pallas_call(
        _ffn_down_kernel,
        grid=(n // tn, m // tm),
        in_specs=[
            pl.BlockSpec((tm, f), lambda j, i: (i, 0)),
            pl.BlockSpec((f, tn), lambda j, i: (0, j)),
            pl.BlockSpec((tm, tn), lambda j, i: (i, j)),
            wn_in,
        ],
        out_specs=[pl.BlockSpec((tm, tn), lambda j, i: (i, j)), wn_out],
        out_shape=[jax.ShapeDtypeStruct((m, n), F32), wn_shape],
        compiler_params=_params("parallel", "parallel"),
        name="ffn_down",
    )(act, wd, x1, w_next)


def _ple_kernel(x_ref, p_ref, gp_ref, wg_ref, wp_ref, gf_ref, o_ref, *, final):
    rp = x_ref.shape[0] // PLE_ROW_PARTS
    for r in range(0, x_ref.shape[0], rp):
        rows = slice(r, r + rp)
        x = x_ref[rows, :]
        hp = _rmsnorm(x, gp_ref[...]).astype(BF16)
        gate = jax.nn.sigmoid(jnp.dot(hp, wg_ref[...], preferred_element_type=F32))
        emb = jnp.dot(p_ref[rows, :].astype(BF16), wp_ref[...], preferred_element_type=F32)
        y = x + gate * emb
        o_ref[rows, :] = _rmsnorm(y, gf_ref[...]) if final else y


def _ple(x2, p2d, gain_p, wpg, wpp, gain_f, *, final, tm=512):
    m, d = x2.shape
    pd = p2d.shape[1]
    return pl.pallas_call(
        functools.partial(_ple_kernel, final=final),
        grid=(m // tm,),
        in_specs=[
            pl.BlockSpec((tm, d), lambda i: (i, 0)),
            pl.BlockSpec((tm, pd), lambda i: (i, 0)),
            _resident(gain_p.shape), _resident(wpg.shape), _resident(wpp.shape),
            _resident(gain_f.shape),
        ],
        out_specs=pl.BlockSpec((tm, d), lambda i: (i, 0)),
        out_shape=jax.ShapeDtypeStruct((m, d), F32),
        compiler_params=_params("parallel"),
        name="ple",
    )(x2, p2d, gain_p, wpg, wpp, gain_f)


def kernel(x, p, rel_bias_table, attn_norm, w_in, sink_a, w_branch_a, w_branch_b, w_out,
           ffn_norm, w_ffn_gate, w_ffn_up, conv_w, conv_b, w_ffn_down,
           ple_norm, w_ple_gate, w_ple_proj, final_norm):
    b, s, d = x.shape
    depth = w_in.shape[0]
    assert d == D_MODEL and w_in.shape[2] == IN_PROJ_W and s % (Q_BLOCK * B_PATTERNS[-1][1]) == 0
    assert rel_bias_table.shape == (N_BUCKETS, N_BIAS_HEADS)
    m = b * s
    x2d = x.reshape(m, d)
    table_flat = rel_bias_table.reshape(-1)
    for i in range(depth):
        pb2d, pa2d = _inproj(x2d, attn_norm[i][None], w_in[i].astype(BF16))
        pb = pb2d.reshape(PB_W // HEAD_DIM, b, s, HEAD_DIM)
        ya, wa = _attn_a(pb, table_flat, sink_a[i], w_branch_a[i])
        yb, wo = _attn_b(pb, table_flat, w_out[i])
        x1, hf = _mix(ya.reshape(m, A_Q_W), yb.reshape(m, B_GROUP_W), pa2d, x2d, wa,
                      w_branch_b[i].astype(BF16), wo, ffn_norm[i][None])
        act, wd = _ffn_up(hf, w_ffn_gate[i], w_ffn_up[i], conv_w[i], conv_b[i][None], w_ffn_down[i],
                          seq=s)
        x2, wpg = _ffn_down(act, wd, x1, w_ple_gate[i])
        x2d = _ple(x2, p[i].reshape(m, -1), ple_norm[i][None], wpg,
                   w_ple_proj[i].astype(BF16), final_norm[None], final=i == depth - 1)
    return x2d.reshape(b, s, d)
```

```python
import functools
import math

import jax
import jax.numpy as jnp
import numpy as np
from jax import lax
from jax.experimental import pallas as pl
from jax.experimental.pallas import tpu as pltpu

D_MODEL = 2048
HEAD_DIM = 128
A_Q_HEADS = 8
A_KV_HEADS = 2
A_GROUP = A_Q_HEADS // A_KV_HEADS
A_HALF_WINDOW = 128
B_PATTERNS = ((128, 1), (512, 4), (2048, 16))
B_HEADS_PER_GROUP = 4
N_BUCKETS = 32
MAX_DISTANCE = 1024
N_BIAS_HEADS = A_Q_HEADS + len(B_PATTERNS) * B_HEADS_PER_GROUP
A_Q_W = A_Q_HEADS * HEAD_DIM
A_KV_W = A_KV_HEADS * HEAD_DIM
B_GROUP_W = B_HEADS_PER_GROUP * HEAD_DIM
B_W = len(B_PATTERNS) * B_GROUP_W
IN_PROJ_W = A_Q_W + 2 * A_KV_W + 3 * B_W + 2 * D_MODEL
RMS_EPS = 1e-6
NEG_INF = -1e30
LOG2_E = math.log2(math.e)
QK_SCALE_LOG2 = HEAD_DIM ** -0.5 * LOG2_E

N_STRIDED_GROUPS = sum(1 for _, d in B_PATTERNS if d > 1)
PB_W = 3 * N_STRIDED_GROUPS * B_GROUP_W
PA_COL_GA = 0
PA_COL_GB = D_MODEL
PA_COL_QA = 2 * D_MODEL
PA_COL_KA = PA_COL_QA + A_Q_W
PA_COL_VA = PA_COL_KA + A_KV_W
PA_COL_B0 = PA_COL_VA + A_KV_W
PA_W = PA_COL_B0 + 3 * B_GROUP_W

Q_BLOCK = 128
ATTN_UNROLL = 2
STAGE_STRIDE = 4
CHAIN_GROUP = (16, 1, 16)
A_BLOCKS_PER_GROUP = 2
MERGE_ROWS = 256
FFN_ROW_PARTS = 2
MIX_ROW_PARTS = 2
PLE_ROW_PARTS = 1

VMEM_LIMIT_BYTES = 56 * 1024 * 1024

BF16 = jnp.bfloat16
F32 = jnp.float32


def _params(*semantics):
    return pltpu.CompilerParams(dimension_semantics=semantics, vmem_limit_bytes=VMEM_LIMIT_BYTES)


def _rmsnorm(x, g):
    y = x * lax.rsqrt(jnp.mean(x * x, axis=-1, keepdims=True) + RMS_EPS)
    return y * g


def _cast_rider(w, n_steps, step_of):
    rows = w.shape[0] // n_steps
    assert rows * n_steps == w.shape[0] and rows % 16 == 0
    spec = pl.BlockSpec((rows, w.shape[1]), lambda *ids: (step_of(*ids), 0))
    return spec, spec, jax.ShapeDtypeStruct(w.shape, BF16)


def _inproj_kernel(x_ref, g_ref, w_ref, pb_ref, pa_ref, h_ref, *, row_chunk, n_b_tiles):
    j = pl.program_id(1)

    def store_slabs(rows, res):
        for s in range(pb_ref.shape[0]):
            pb_ref[s, rows, :] = res[:, s * HEAD_DIM:(s + 1) * HEAD_DIM]

    @pl.when(j == 0)
    def _():
        for r in range(0, x_ref.shape[0], row_chunk):
            rows = slice(r, r + row_chunk)
            h = _rmsnorm(x_ref[rows, :], g_ref[...]).astype(BF16)
            h_ref[rows, :] = h
            store_slabs(rows, jnp.dot(h, w_ref[...], preferred_element_type=F32))

    @pl.when((j > 0) & (j < n_b_tiles))
    def _():
        store_slabs(slice(None), jnp.dot(h_ref[...], w_ref[...], preferred_element_type=F32))

    @pl.when(j >= n_b_tiles)
    def _():
        pa_ref[...] = jnp.dot(h_ref[...], w_ref[...], preferred_element_type=F32).astype(BF16)


def _in_proj_weight(w):
    qa, ka, va, qb, kb, vb, gates = jnp.split(
        w, np.cumsum([A_Q_W, A_KV_W, A_KV_W, B_W, B_W, B_W]).tolist(), axis=1)
    strided = [gi for gi, (_, d) in enumerate(B_PATTERNS) if d > 1]
    plain = [gi for gi, (_, d) in enumerate(B_PATTERNS) if d == 1]
    assert plain == [0] and strided == list(range(1, len(B_PATTERNS)))

    def group(t, gi):
        return t[:, gi * B_GROUP_W:(gi + 1) * B_GROUP_W]

    cols = [group(t, gi) for t in (qb, kb, vb) for gi in strided]
    cols += [gates, qa * QK_SCALE_LOG2, ka, va, group(qb, 0) * QK_SCALE_LOG2, group(kb, 0), group(vb, 0)]
    return jnp.concatenate(cols, axis=1).astype(BF16)


def _inproj(x2d, gain, w, *, tm=1024, tn=1024):
    m, k = x2d.shape
    assert PB_W % tn == 0 and PA_W % tn == 0 and w.shape[1] == PB_W + PA_W
    nb_t = PB_W // tn
    n_steps = w.shape[1] // tn

    return pl.pallas_call(
        functools.partial(_inproj_kernel, row_chunk=256, n_b_tiles=nb_t),
        grid=(m // tm, n_steps),
        in_specs=[
            pl.BlockSpec((tm, k), lambda i, j: (i, 0)),
            pl.BlockSpec((1, k), lambda i, j: (0, 0)),
            pl.BlockSpec((k, tn), lambda i, j: (0, j)),
        ],
        out_specs=[
            pl.BlockSpec((tn // HEAD_DIM, tm, HEAD_DIM), lambda i, j: (jnp.minimum(j, nb_t - 1), i, 0)),
            pl.BlockSpec((tm, tn), lambda i, j: (i, jnp.maximum(j - nb_t, 0))),
        ],
        out_shape=[jax.ShapeDtypeStruct((PB_W // HEAD_DIM, m, HEAD_DIM), F32),
                   jax.ShapeDtypeStruct((m, PA_W), BF16)],
        scratch_shapes=[pltpu.VMEM((tm, k), BF16)],
        compiler_params=_params("arbitrary", "arbitrary"),
        name="inproj",
    )(x2d, gain, w)


def _t5_bucket_static(rel):
    half = N_BUCKETS // 2
    max_exact = half // 2
    n = np.abs(rel)
    side = np.where(rel > 0, half, 0)
    nf = np.maximum(n, 1).astype(np.float32)
    large = max_exact + (np.log(nf / max_exact) / math.log(MAX_DISTANCE / max_exact)
                         * (half - max_exact)).astype(np.int32)
    large = np.minimum(large, half - 1)
    return side + np.where(n < max_exact, n, large)


def _window_geometry(seq_len, half_w):
    nb = seq_len // Q_BLOCK
    if nb == 1:
        return seq_len, nb, (0,)
    kw = Q_BLOCK + 2 * half_w
    assert half_w <= Q_BLOCK and kw <= seq_len
    return kw, nb, (0, -half_w, Q_BLOCK - kw)


def _band_buckets(seq_len, half_w, dilation):
    kw, _, offsets = _window_geometry(seq_len, half_w)
    qi = np.arange(Q_BLOCK)[:, None]
    kj = np.arange(kw)[None, :]
    rel = np.stack([off + kj - qi for off in offsets])
    buckets = _t5_bucket_static(rel * dilation)
    return np.where(np.abs(rel) <= half_w, buckets, -1).astype(np.int32)


def _block_geometry(i, seq_len, half_w):
    kw, nb, _ = _window_geometry(seq_len, half_w)
    if isinstance(i, int):
        qs = i * Q_BLOCK
        return qs, min(max(qs - half_w, 0), seq_len - kw), 0 if i == 0 else (2 if i == nb - 1 else 1)
    qs = pl.multiple_of(i * Q_BLOCK, Q_BLOCK)
    ks = pl.multiple_of(jnp.clip(qs - half_w, 0, seq_len - kw), half_w)
    return qs, ks, jnp.where(i == 0, 0, jnp.where(i == nb - 1, 2, 1))


def _build_bias(bkt_ref, table_ref, bias_ref, slot, first_head, n_heads, buckets_used):
    for kind in range(bkt_ref.shape[0]):
        bkt = bkt_ref[kind]
        for h in range(n_heads):
            acc = jnp.full(bkt.shape, NEG_INF, F32)
            for bucket in buckets_used:
                acc = jnp.where(bkt == bucket,
                                table_ref[bucket * N_BIAS_HEADS + first_head + h] * LOG2_E, acc)
            bias_ref[slot, kind, h * Q_BLOCK:(h + 1) * Q_BLOCK, :] = acc


def _softmax_chains(items, *, load, store, bias, kw, sink=None, want_lse):
    ones = jnp.ones((kw, HEAD_DIM), BF16)
    scores = []
    for tag, qs, ks, kind in items:
        q = load(0, tag, qs, Q_BLOCK)
        if q.dtype != BF16:
            q = (q * QK_SCALE_LOG2).astype(BF16)
        s = lax.dot_general(q, load(1, tag, ks, kw).astype(BF16),
                            (((1,), (1,)), ((), ())), preferred_element_type=F32)
        scores.append(s + bias(tag, kind))
    for (tag, qs, ks, _), s in zip(items, scores):
        v_ext = jnp.concatenate([load(2, tag, ks, kw).astype(BF16), ones], axis=1)
        m = jnp.max(s, axis=-1, keepdims=True)
        if sink is not None:
            sk = sink(tag) * LOG2_E
            m = jnp.maximum(m, sk)
        p = jnp.exp2(s - m).astype(BF16)
        ov = jnp.dot(p, v_ext, preferred_element_type=F32)
        denom = ov[:, HEAD_DIM:]
        if sink is not None:
            denom = denom + jnp.exp2(sk - m)
        store(0, tag, qs, ov[:, :HEAD_DIM] / denom)
        if want_lse:
            store(1, tag, qs, m * (1.0 / LOG2_E) + jnp.log(denom))


def _attn_a_kernel(bkt_ref, table_ref, sink_ref, q_ref, k_ref, v_ref, wn_ref, o_ref, wn_out_ref,
                   bias_ref, *, seq_len, buckets_used):
    kv = pl.program_id(1)
    kw, nb, _ = _window_geometry(seq_len, A_HALF_WINDOW)
    wn_out_ref[...] = wn_ref[...].astype(BF16)

    @pl.when(pl.program_id(0) == 0)
    def _():
        _build_bias(bkt_ref, table_ref, bias_ref, kv, kv * A_GROUP, A_GROUP, buckets_used)

    def load(t, h, start, size):
        if t == 0:
            return q_ref[0, pl.ds(start, size), h * HEAD_DIM:(h + 1) * HEAD_DIM]
        return (k_ref, v_ref)[t - 1][0, pl.ds(start, size), :]

    def store(t, h, start, value):
        o_ref[0, pl.ds(start, Q_BLOCK), h * HEAD_DIM:(h + 1) * HEAD_DIM] = value.astype(o_ref.dtype)

    def body(it, carry):
        items = [(h,) + _block_geometry(it * A_BLOCKS_PER_GROUP + di, seq_len, A_HALF_WINDOW)
                 for di in range(A_BLOCKS_PER_GROUP) for h in range(A_GROUP)]
        _softmax_chains(
            items, load=load, store=store, kw=kw, want_lse=False,
            bias=lambda h, kind: bias_ref[kv, kind, h * Q_BLOCK:(h + 1) * Q_BLOCK, :],
            sink=lambda h: jnp.full((Q_BLOCK, 1), sink_ref[kv * A_GROUP + h], F32))
        return carry

    assert nb % A_BLOCKS_PER_GROUP == 0
    lax.fori_loop(0, nb // A_BLOCKS_PER_GROUP, body, 0, unroll=ATTN_UNROLL)


def _buckets_used(bkt):
    return tuple(int(v) for v in np.unique(bkt) if v >= 0)


def _smem_spec():
    return pl.BlockSpec(memory_space=pltpu.SMEM)


def _attn_a(pa, table_flat, sink, w_next):
    b, s, _ = pa.shape
    wn_in, wn_out, wn_shape = _cast_rider(w_next, b * A_KV_HEADS, lambda i, j: i * A_KV_HEADS + j)
    gw = A_GROUP * HEAD_DIM
    kw, _, offsets = _window_geometry(s, A_HALF_WINDOW)
    bkt = _band_buckets(s, A_HALF_WINDOW, 1)
    kernel = functools.partial(_attn_a_kernel, seq_len=s, buckets_used=_buckets_used(bkt))
    return pl.pallas_call(
        kernel,
        grid=(b, A_KV_HEADS),
        in_specs=[
            pl.BlockSpec(bkt.shape, lambda i, j: (0, 0, 0)),
            _smem_spec(),
            _smem_spec(),
            pl.BlockSpec((1, s, gw), lambda i, j: (i, 0, PA_COL_QA // gw + j)),
            pl.BlockSpec((1, s, HEAD_DIM), lambda i, j: (i, 0, PA_COL_KA // HEAD_DIM + j)),
            pl.BlockSpec((1, s, HEAD_DIM), lambda i, j: (i, 0, PA_COL_VA // HEAD_DIM + j)),
            wn_in,
        ],
        out_specs=[pl.BlockSpec((1, s, gw), lambda i, j: (i, 0, j)), wn_out],
        out_shape=[jax.ShapeDtypeStruct((b, s, A_Q_W), BF16), wn_shape],
        scratch_shapes=[pltpu.VMEM((A_KV_HEADS, len(offsets), A_GROUP * Q_BLOCK, kw), F32)],
        compiler_params=_params("arbitrary", "arbitrary"),
        name="attn_a",
    )(jnp.asarray(bkt), table_flat, sink, pa, pa, pa, w_next)


def _attn_b_kernel(*refs, seq, groups):
    ng = len(groups)
    it = iter(refs)
    bkt_refs = [next(it) for _ in range(ng)]
    table_ref = next(it)
    qkv_refs = [[next(it) for _ in range(3)] for _ in range(ng)]
    wn_ref, yb_ref, wn_out_ref = next(it), next(it), next(it)
    bias_refs = [next(it) for _ in range(ng)]
    o_acc, l_acc, stage_in, stage_out = next(it), next(it), next(it), next(it)
    head = pl.program_id(1)
    wn_out_ref[...] = wn_ref[...].astype(BF16)

    @pl.when(pl.program_id(0) == 0)
    def _():
        for g, grp in enumerate(groups):
            _build_bias(bkt_refs[g], table_ref, bias_refs[g], head, grp["head0"] + head, 1,
                        grp["buckets_used"])

    for g, grp in enumerate(groups):
        dil, sub, half, chain_group = grp["dil"], grp["sub"], grp["half"], grp["chain_group"]
        kw, nb, _ = _window_geometry(sub, half)
        stage = STAGE_STRIDE if dil > STAGE_STRIDE else 1
        hop2 = dil // stage
        accs = (o_acc, l_acc)

        def rows(residue, start, size, dil=dil, stage=stage, hop2=hop2):
            if dil == 1:
                return pl.ds(start, size)
            if stage > 1:
                return pl.ds(residue // stage + start * hop2, size, stride=hop2)
            return pl.ds(residue + start * dil, size, stride=dil)

        def load(t, residue, start, size, g=g, stage=stage, rows=rows):
            if stage > 1:
                return stage_in[t, residue % stage, rows(residue, start, size), :]
            ref = qkv_refs[g][t]
            return ref[(0,) * (ref.ndim - 2) + (rows(residue, start, size), slice(None))]

        def store(t, residue, start, value, g=g, stage=stage, rows=rows, accs=accs):
            if stage > 1:
                stage_out[t, residue % stage, rows(residue, start, Q_BLOCK), :] = value
            else:
                accs[t][g, rows(residue, start, Q_BLOCK), :] = value

        def bias(residue, kind, g=g):
            return bias_refs[g][head, kind]

        run = functools.partial(_softmax_chains, load=load, store=store, bias=bias, kw=kw,
                                want_lse=True)
        if dil == 1:
            per_iter = min(chain_group, nb)
            assert nb % per_iter == 0

            def body(it_, carry, run=run, per_iter=per_iter, sub=sub, half=half):
                run([(0,) + _block_geometry(it_ * per_iter + di, sub, half) for di in range(per_iter)])
                return carry

            lax.fori_loop(0, nb // per_iter, body, 0)
            continue
        if stage > 1:
            for t in range(3):
                for r1 in range(stage):
                    stage_in[t, r1] = qkv_refs[g][t][0, 0, pl.ds(r1, seq // stage, stride=stage), :]
        work = [(r,) + _block_geometry(i, sub, half) for r in range(dil) for i in range(nb)]
        for c in range(0, len(work), chain_group):
            run(work[c:c + chain_group])
        if stage > 1:
            for t in range(2):
                for r1 in range(stage):
                    accs[t][g, pl.ds(r1, seq // stage, stride=stage), :] = stage_out[t, r1]

    def merge(c, carry):
        rws = pl.ds(pl.multiple_of(c * MERGE_ROWS, MERGE_ROWS), MERGE_ROWS)
        ls = [l_acc[g, rws, :] for g in range(ng)]
        mx = functools.reduce(jnp.maximum, ls)
        es = [jnp.exp(l - mx) for l in ls]
        tot = functools.reduce(lambda a, e: a + e, es)
        y = functools.reduce(lambda a, e: a + e, [e * o_acc[g, rws, :] for g, e in enumerate(es)])
        yb_ref[0, rws, :] = (y / tot).astype(yb_ref.dtype)
        return carry

    lax.fori_loop(0, seq // MERGE_ROWS, merge, 0)


def _attn_b(pa, pb, table_flat, w_next):
    _, b, s, _ = pb.shape
    nh = B_HEADS_PER_GROUP
    wn_in, wn_out, wn_shape = _cast_rider(w_next, b * nh, lambda i, h: i * nh + h)
    groups, bkts, bias_scratch = [], [], []
    for gi, (window, dil) in enumerate(B_PATTERNS):
        sub, half = s // dil, window // (2 * dil)
        kw, _, offsets = _window_geometry(sub, half)
        bkt = _band_buckets(sub, half, dil)
        bkts.append(bkt)
        bias_scratch.append(pltpu.VMEM((nh, len(offsets), Q_BLOCK, kw), F32))
        groups.append(dict(dil=dil, sub=sub, half=half, chain_group=CHAIN_GROUP[gi],
                           head0=A_Q_HEADS + gi * nh, buckets_used=_buckets_used(bkt)))

    def qkv_spec(gi, part):
        if B_PATTERNS[gi][1] == 1:
            base = (PA_COL_B0 + part * B_GROUP_W) // HEAD_DIM
            return pl.BlockSpec((1, s, HEAD_DIM), lambda i, h: (i, 0, base + h))
        base = (part * N_STRIDED_GROUPS + gi - 1) * nh
        return pl.BlockSpec((1, 1, s, HEAD_DIM), lambda i, h: (base + h, i, 0, 0))

    ng = len(groups)
    return pl.pallas_call(
        functools.partial(_attn_b_kernel, seq=s, groups=groups),
        grid=(b, nh),
        in_specs=[pl.BlockSpec(bkt.shape, lambda i, h: (0, 0, 0)) for bkt in bkts] + [_smem_spec()]
        + [qkv_spec(gi, part) for gi in range(ng) for part in range(3)] + [wn_in],
        out_specs=[pl.BlockSpec((1, s, HEAD_DIM), lambda i, h: (i, 0, h)), wn_out],
        out_shape=[jax.ShapeDtypeStruct((b, s, B_GROUP_W), BF16), wn_shape],
        scratch_shapes=bias_scratch + [
            pltpu.VMEM((ng, s, HEAD_DIM), F32), pltpu.VMEM((ng, s, HEAD_DIM), F32),
            pltpu.VMEM((3, STAGE_STRIDE, s // STAGE_STRIDE, HEAD_DIM), F32),
            pltpu.VMEM((2, STAGE_STRIDE, s // STAGE_STRIDE, HEAD_DIM), F32)],
        compiler_params=_params("arbitrary", "arbitrary"),
        name="attn_b",
    )(*[jnp.asarray(bkt) for bkt in bkts], table_flat,
      *[pa if B_PATTERNS[gi][1] == 1 else pb for gi in range(ng) for _ in range(3)], w_next)


def _mix_kernel(ya_ref, yb_ref, ga_ref, gb_ref, x_ref, wa_ref, wb_ref, wo_ref, g_ref, x1_ref, hf_ref):
    rp = x_ref.shape[0] // MIX_ROW_PARTS
    for r in range(0, x_ref.shape[0], rp):
        rows = slice(r, r + rp)
        ta = jnp.dot(ya_ref[rows, :], wa_ref[...], preferred_element_type=F32)
        tb = jnp.dot(yb_ref[rows, :], wb_ref[...], preferred_element_type=F32)
        merged = (jax.nn.sigmoid(ga_ref[rows, :].astype(F32)) * ta
                  + jax.nn.sigmoid(gb_ref[rows, :].astype(F32)) * tb)
        x1 = x_ref[rows, :] + jnp.dot(merged.astype(BF16), wo_ref[...], preferred_element_type=F32)
        x1_ref[rows, :] = x1
        hf_ref[rows, :] = _rmsnorm(x1, g_ref[...]).astype(BF16)


def _resident(shape):
    return pl.BlockSpec(shape, lambda i: (0,) * len(shape), pipeline_mode=pl.Buffered(1))


def _mix(ya, yb, pa2d, x2d, wa, wb, wo, gain, *, tm=512):
    m, d = x2d.shape
    assert PA_COL_GA % d == 0 and PA_COL_GB % d == 0

    def rows(width, col_block=0):
        return pl.BlockSpec((tm, width), lambda i: (i, col_block))

    return pl.pallas_call(
        _mix_kernel,
        grid=(m // tm,),
        in_specs=[rows(A_Q_W), rows(B_GROUP_W),
                  rows(d, PA_COL_GA // d), rows(d, PA_COL_GB // d), rows(d),
                  _resident(wa.shape), _resident(wb.shape), _resident(wo.shape), _resident(gain.shape)],
        out_specs=[rows(d), rows(d)],
        out_shape=[jax.ShapeDtypeStruct((m, d), F32), jax.ShapeDtypeStruct((m, d), BF16)],
        compiler_params=_params("parallel"),
        name="mix",
    )(ya, yb, pa2d, pa2d, x2d, wa, wb, wo, gain)


def _ffn_up_kernel(hf_ref, wg_ref, wu_ref, cw_ref, cb_ref, wn_ref, act_ref, wn_out_ref):
    wn_out_ref[...] = wn_ref[...].astype(BF16)
    wg = wg_ref[...].astype(BF16)
    wu = wu_ref[...].astype(BF16)
    rp = hf_ref.shape[0] // FFN_ROW_PARTS
    parts = [(jnp.dot(hf_ref[r:r + rp, :], wg, preferred_element_type=F32),
              jnp.dot(hf_ref[r:r + rp, :], wu, preferred_element_type=F32))
             for r in range(0, hf_ref.shape[0], rp)]
    g = jnp.concatenate([p[0] for p in parts], axis=0)
    u = jnp.concatenate([p[1] for p in parts], axis=0)
    s = g.shape[0]
    row = lax.broadcasted_iota(jnp.int32, g.shape, 0)
    prev = jnp.where(row == 0, 0.0, pltpu.roll(g, 1, 0))
    nxt = jnp.where(row == s - 1, 0.0, pltpu.roll(g, s - 1, 0))
    cw = cw_ref[...]
    conv = prev * cw[0:1] + g * cw[1:2] + nxt * cw[2:3] + cb_ref[...]
    act_ref[...] = (jax.nn.gelu(conv) * u).astype(act_ref.dtype)


def _ffn_up(hf, wg, wu, cw, cb, w_next, *, seq, tf=512):
    m, d = hf.shape
    f = wg.shape[1]
    nj = f // tf
    wn_in, wn_out, wn_shape = _cast_rider(w_next, (m // seq) * nj, lambda i, j: i * nj + j)
    return pl.pallas_call(
        _ffn_up_kernel,
        grid=(m // seq, f // tf),
        in_specs=[
            pl.BlockSpec((seq, d), lambda i, j: (i, 0)),
            pl.BlockSpec((d, tf), lambda i, j: (0, j)),
            pl.BlockSpec((d, tf), lambda i, j: (0, j)),
            pl.BlockSpec((cw.shape[0], tf), lambda i, j: (0, j)),
            pl.BlockSpec((1, tf), lambda i, j: (0, j)),
            wn_in,
        ],
        out_specs=[pl.BlockSpec((seq, tf), lambda i, j: (i, j)), wn_out],
        out_shape=[jax.ShapeDtypeStruct((m, f), BF16), wn_shape],
        compiler_params=_params("parallel", "arbitrary"),
        name="ffn_up",
    )(hf, wg, wu, cw, cb, w_next)


def _ffn_down_kernel(act_ref, w_ref, x_ref, wn_ref, o_ref, wn_out_ref):
    wn_out_ref[...] = wn_ref[...].astype(BF16)
    o_ref[...] = x_ref[...] + jnp.dot(act_ref[...], w_ref[...], preferred_element_type=F32)


def _ffn_down(act, wd, x1, w_next, *, tm=512, tn=1024):
    m, f = act.shape
    n = wd.shape[1]
    ni = m // tm
    wn_in, wn_out, wn_shape = _cast_rider(w_next, (n // tn) * ni, lambda j, i: j * ni + i)
    return pl.pallas_call(
        _ffn_down_kernel,
        grid=(n // tn, m // tm),
        in_specs=[
            pl.BlockSpec((tm, f), lambda j, i: (i, 0)),
            pl.BlockSpec((f, tn), lambda j, i: (0, j)),
            pl.BlockSpec((tm, tn), lambda j, i: (i, j)),
            wn_in,
        ],
        out_specs=[pl.BlockSpec((tm, tn), lambda j, i: (i, j)), wn_out],
        out_shape=[jax.ShapeDtypeStruct((m, n), F32), wn_shape],
        compiler_params=_params("parallel", "parallel"),
        name="ffn_down",
    )(act, wd, x1, w_next)


def _ple_kernel(x_ref, p_ref, gp_ref, wg_ref, wp_ref, gf_ref, o_ref, *, final):
    rp = x_ref.shape[0] // PLE_ROW_PARTS
    for r in range(0, x_ref.shape[0], rp):
        rows = slice(r, r + rp)
        x = x_ref[rows, :]
        hp = _rmsnorm(x, gp_ref[...]).astype(BF16)
        gate = jax.nn.sigmoid(jnp.dot(hp, wg_ref[...], preferred_element_type=F32))
        emb = jnp.dot(p_ref[rows, :].astype(BF16), wp_ref[...], preferred_element_type=F32)
        y = x + gate * emb
        o_ref[rows, :] = _rmsnorm(y, gf_ref[...]) if final else y


def _ple(x2, p2d, gain_p, wpg, wpp, gain_f, *, final, tm=512):
    m, d = x2.shape
    pd = p2d.shape[1]
    return pl.pallas_call(
        functools.partial(_ple_kernel, final=final),
        grid=(m // tm,),
        in_specs=[
            pl.BlockSpec((tm, d), lambda i: (i, 0)),
            pl.BlockSpec((tm, pd), lambda i: (i, 0)),
            _resident(gain_p.shape), _resident(wpg.shape), _resident(wpp.shape),
            _resident(gain_f.shape),
        ],
        out_specs=pl.BlockSpec((tm, d), lambda i: (i, 0)),
        out_shape=jax.ShapeDtypeStruct((m, d), F32),
        compiler_params=_params("parallel"),
        name="ple",
    )(x2, p2d, gain_p, wpg, wpp, gain_f)


def kernel(x, p, rel_bias_table, attn_norm, w_in, sink_a, w_branch_a, w_branch_b, w_out,
           ffn_norm, w_ffn_gate, w_ffn_up, conv_w, conv_b, w_ffn_down,
           ple_norm, w_ple_gate, w_ple_proj, final_norm):
    b, s, d = x.shape
    depth = w_in.shape[0]
    assert d == D_MODEL and w_in.shape[2] == IN_PROJ_W and s % (Q_BLOCK * B_PATTERNS[-1][1]) == 0
    assert rel_bias_table.shape == (N_BUCKETS, N_BIAS_HEADS)
    m = b * s
    x2d = x.reshape(m, d)
    table_flat = rel_bias_table.reshape(-1)
    for i in range(depth):
        pb2d, pa2d = _inproj(x2d, attn_norm[i][None], _in_proj_weight(w_in[i]))
        pb = pb2d.reshape(PB_W // HEAD_DIM, b, s, HEAD_DIM)
        pa = pa2d.reshape(b, s, PA_W)
        ya, wa = _attn_a(pa, table_flat, sink_a[i], w_branch_a[i])
        yb, wo = _attn_b(pa, pb, table_flat, w_out[i])
        x1, hf = _mix(ya.reshape(m, A_Q_W), yb.reshape(m, B_GROUP_W), pa2d, x2d, wa,
                      w_branch_b[i].astype(BF16), wo, ffn_norm[i][None])
        act, wd = _ffn_up(hf, w_ffn_gate[i], w_ffn_up[i], conv_w[i], conv_b[i][None], w_ffn_down[i],
                          seq=s)
        x2, wpg = _ffn_down(act, wd, x1, w_ple_gate[i])
        x2d = _ple(x2, p[i].reshape(m, -1), ple_norm[i][None], wpg,
                   w_ple_proj[i].astype(BF16), final_norm[None], final=i == depth - 1)
    return x2d.reshape(b, s, d)
```

```python
import functools
import math

import jax
import jax.numpy as jnp
import numpy as np
from jax import lax
from jax.experimental import pallas as pl
from jax.experimental.pallas import tpu as pltpu

D_MODEL = 2048
HEAD_DIM = 128
A_Q_HEADS = 8
A_KV_HEADS = 2
A_GROUP = A_Q_HEADS // A_KV_HEADS
A_HALF_WINDOW = 128
B_PATTERNS = ((128, 1), (512, 4), (2048, 16))
B_HEADS_PER_GROUP = 4
N_BUCKETS = 32
MAX_DISTANCE = 1024
N_BIAS_HEADS = A_Q_HEADS + len(B_PATTERNS) * B_HEADS_PER_GROUP
A_Q_W = A_Q_HEADS * HEAD_DIM
A_KV_W = A_KV_HEADS * HEAD_DIM
B_GROUP_W = B_HEADS_PER_GROUP * HEAD_DIM
B_W = len(B_PATTERNS) * B_GROUP_W
IN_PROJ_W = A_Q_W + 2 * A_KV_W + 3 * B_W + 2 * D_MODEL
RMS_EPS = 1e-6
NEG_INF = -1e30
LOG2_E = math.log2(math.e)
QK_SCALE_LOG2 = HEAD_DIM ** -0.5 * LOG2_E

PB_W = A_Q_W + 2 * A_KV_W + 3 * B_W
PB_COL_KA = A_Q_W
PB_COL_VA = PB_COL_KA + A_KV_W
PB_COL_QB = PB_COL_VA + A_KV_W
PA_W = 2 * D_MODEL
PA_COL_GA = 0
PA_COL_GB = D_MODEL

Q_BLOCK = 128
ATTN_UNROLL = 2
STAGE_STRIDE = 4
CHAIN_GROUP = (16, 1, 16)
A_BLOCKS_PER_GROUP = 2
MERGE_ROWS = 256
FFN_ROW_PARTS = 2
MIX_ROW_PARTS = 2
PLE_ROW_PARTS = 1

VMEM_LIMIT_BYTES = 56 * 1024 * 1024

BF16 = jnp.bfloat16
F32 = jnp.float32


def _params(*semantics):
    return pltpu.CompilerParams(dimension_semantics=semantics, vmem_limit_bytes=VMEM_LIMIT_BYTES)


def _rmsnorm(x, g):
    y = x * lax.rsqrt(jnp.mean(x * x, axis=-1, keepdims=True) + RMS_EPS)
    return y * g


def _cast_rider(w, n_steps, step_of):
    rows = w.shape[0] // n_steps
    assert rows * n_steps == w.shape[0] and rows % 16 == 0
    spec = pl.BlockSpec((rows, w.shape[1]), lambda *ids: (step_of(*ids), 0))
    return spec, spec, jax.ShapeDtypeStruct(w.shape, BF16)


def _inproj_kernel(x_ref, g_ref, w_ref, pb_ref, pa_ref, h_ref, *, row_chunk, n_b_tiles):
    j = pl.program_id(1)

    def weight():
        return w_ref[...].astype(BF16)

    def store_slabs(rows, res):
        for s in range(pb_ref.shape[0]):
            pb_ref[s, rows, :] = res[:, s * HEAD_DIM:(s + 1) * HEAD_DIM]

    @pl.when(j == 0)
    def _():
        w = weight()
        for r in range(0, x_ref.shape[0], row_chunk):
            rows = slice(r, r + row_chunk)
            h = _rmsnorm(x_ref[rows, :], g_ref[...]).astype(BF16)
            h_ref[rows, :] = h
            store_slabs(rows, jnp.dot(h, w, preferred_element_type=F32))

    @pl.when((j > 0) & (j < n_b_tiles))
    def _():
        store_slabs(slice(None), jnp.dot(h_ref[...], weight(), preferred_element_type=F32))

    @pl.when(j >= n_b_tiles)
    def _():
        pa_ref[...] = jnp.dot(h_ref[...], weight(), preferred_element_type=F32).astype(BF16)


def _inproj(x2d, gain, w, *, tm=1024, tn=1024):
    m, k = x2d.shape
    assert PB_W % tn == 0 and PA_W % tn == 0 and w.shape[1] == PB_W + PA_W
    nb_t = PB_W // tn
    n_steps = w.shape[1] // tn

    return pl.pallas_call(
        functools.partial(_inproj_kernel, row_chunk=256, n_b_tiles=nb_t),
        grid=(m // tm, n_steps),
        in_specs=[
            pl.BlockSpec((tm, k), lambda i, j: (i, 0)),
            pl.BlockSpec((1, k), lambda i, j: (0, 0)),
            pl.BlockSpec((k, tn), lambda i, j: (0, j)),
        ],
        out_specs=[
            pl.BlockSpec((tn // HEAD_DIM, tm, HEAD_DIM), lambda i, j: (jnp.minimum(j, nb_t - 1), i, 0)),
            pl.BlockSpec((tm, tn), lambda i, j: (i, jnp.maximum(j - nb_t, 0))),
        ],
        out_shape=[jax.ShapeDtypeStruct((PB_W // HEAD_DIM, m, HEAD_DIM), F32),
                   jax.ShapeDtypeStruct((m, PA_W), BF16)],
        scratch_shapes=[pltpu.VMEM((tm, k), BF16)],
        compiler_params=_params("arbitrary", "arbitrary"),
        name="inproj",
    )(x2d, gain, w)


def _t5_bucket_static(rel):
    half = N_BUCKETS // 2
    max_exact = half // 2
    n = np.abs(rel)
    side = np.where(rel > 0, half, 0)
    nf = np.maximum(n, 1).astype(np.float32)
    large = max_exact + (np.log(nf / max_exact) / math.log(MAX_DISTANCE / max_exact)
                         * (half - max_exact)).astype(np.int32)
    large = np.minimum(large, half - 1)
    return side + np.where(n < max_exact, n, large)


def _window_geometry(seq_len, half_w):
    nb = seq_len // Q_BLOCK
    if nb == 1:
        return seq_len, nb, (0,)
    kw = Q_BLOCK + 2 * half_w
    assert half_w <= Q_BLOCK and kw <= seq_len
    return kw, nb, (0, -half_w, Q_BLOCK - kw)


def _band_buckets(seq_len, half_w, dilation):
    kw, _, offsets = _window_geometry(seq_len, half_w)
    qi = np.arange(Q_BLOCK)[:, None]
    kj = np.arange(kw)[None, :]
    rel = np.stack([off + kj - qi for off in offsets])
    buckets = _t5_bucket_static(rel * dilation)
    return np.where(np.abs(rel) <= half_w, buckets, -1).astype(np.int32)


def _block_geometry(i, seq_len, half_w):
    kw, nb, _ = _window_geometry(seq_len, half_w)
    if isinstance(i, int):
        qs = i * Q_BLOCK
        return qs, min(max(qs - half_w, 0), seq_len - kw), 0 if i == 0 else (2 if i == nb - 1 else 1)
    qs = pl.multiple_of(i * Q_BLOCK, Q_BLOCK)
    ks = pl.multiple_of(jnp.clip(qs - half_w, 0, seq_len - kw), half_w)
    return qs, ks, jnp.where(i == 0, 0, jnp.where(i == nb - 1, 2, 1))


def _build_bias(bkt_ref, table_ref, bias_ref, slot, first_head, n_heads, buckets_used):
    for kind in range(bkt_ref.shape[0]):
        bkt = bkt_ref[kind]
        for h in range(n_heads):
            acc = jnp.full(bkt.shape, NEG_INF, F32)
            for bucket in buckets_used:
                acc = jnp.where(bkt == bucket,
                                table_ref[bucket * N_BIAS_HEADS + first_head + h] * LOG2_E, acc)
            bias_ref[slot, kind, h * Q_BLOCK:(h + 1) * Q_BLOCK, :] = acc


def _softmax_chains(items, *, load, store, bias, kw, sink=None, want_lse):
    ones = jnp.ones((kw, HEAD_DIM), BF16)
    scores = []
    for tag, qs, ks, kind in items:
        q = (load(0, tag, qs, Q_BLOCK) * QK_SCALE_LOG2).astype(BF16)
        s = lax.dot_general(q, load(1, tag, ks, kw).astype(BF16),
                            (((1,), (1,)), ((), ())), preferred_element_type=F32)
        scores.append(s + bias(tag, kind))
    for (tag, qs, ks, _), s in zip(items, scores):
        v_ext = jnp.concatenate([load(2, tag, ks, kw).astype(BF16), ones], axis=1)
        m = jnp.max(s, axis=-1, keepdims=True)
        if sink is not None:
            sk = sink(tag) * LOG2_E
            m = jnp.maximum(m, sk)
        p = jnp.exp2(s - m).astype(BF16)
        ov = jnp.dot(p, v_ext, preferred_element_type=F32)
        denom = ov[:, HEAD_DIM:]
        if sink is not None:
            denom = denom + jnp.exp2(sk - m)
        store(0, tag, qs, ov[:, :HEAD_DIM] / denom)
        if want_lse:
            store(1, tag, qs, m * (1.0 / LOG2_E) + jnp.log(denom))


def _attn_a_kernel(bkt_ref, table_ref, sink_ref, q_ref, k_ref, v_ref, wn_ref, o_ref, wn_out_ref,
                   bias_ref, *, seq_len, buckets_used):
    kv = pl.program_id(1)
    kw, nb, _ = _window_geometry(seq_len, A_HALF_WINDOW)
    wn_out_ref[...] = wn_ref[...].astype(BF16)

    @pl.when(pl.program_id(0) == 0)
    def _():
        _build_bias(bkt_ref, table_ref, bias_ref, kv, kv * A_GROUP, A_GROUP, buckets_used)

    def load(t, h, start, size):
        if t == 0:
            return q_ref[h, 0, pl.ds(start, size), :]
        return (k_ref, v_ref)[t - 1][0, 0, pl.ds(start, size), :]

    def store(t, h, start, value):
        o_ref[0, pl.ds(start, Q_BLOCK), h * HEAD_DIM:(h + 1) * HEAD_DIM] = value.astype(o_ref.dtype)

    def body(it, carry):
        items = [(h,) + _block_geometry(it * A_BLOCKS_PER_GROUP + di, seq_len, A_HALF_WINDOW)
                 for di in range(A_BLOCKS_PER_GROUP) for h in range(A_GROUP)]
        _softmax_chains(
            items, load=load, store=store, kw=kw, want_lse=False,
            bias=lambda h, kind: bias_ref[kv, kind, h * Q_BLOCK:(h + 1) * Q_BLOCK, :],
            sink=lambda h: jnp.full((Q_BLOCK, 1), sink_ref[kv * A_GROUP + h], F32))
        return carry

    assert nb % A_BLOCKS_PER_GROUP == 0
    lax.fori_loop(0, nb // A_BLOCKS_PER_GROUP, body, 0, unroll=ATTN_UNROLL)


def _buckets_used(bkt):
    return tuple(int(v) for v in np.unique(bkt) if v >= 0)


def _smem_spec():
    return pl.BlockSpec(memory_space=pltpu.SMEM)


def _attn_a(pb, table_flat, sink, w_next):
    _, b, s, _ = pb.shape
    wn_in, wn_out, wn_shape = _cast_rider(w_next, b * A_KV_HEADS, lambda i, j: i * A_KV_HEADS + j)
    gw = A_GROUP * HEAD_DIM
    kw, _, offsets = _window_geometry(s, A_HALF_WINDOW)
    bkt = _band_buckets(s, A_HALF_WINDOW, 1)
    kernel = functools.partial(_attn_a_kernel, seq_len=s, buckets_used=_buckets_used(bkt))
    return pl.pallas_call(
        kernel,
        grid=(b, A_KV_HEADS),
        in_specs=[
            pl.BlockSpec(bkt.shape, lambda i, j: (0, 0, 0)),
            _smem_spec(),
            _smem_spec(),
            pl.BlockSpec((A_GROUP, 1, s, HEAD_DIM), lambda i, j: (j, i, 0, 0)),
            pl.BlockSpec((1, 1, s, HEAD_DIM), lambda i, j: (PB_COL_KA // HEAD_DIM + j, i, 0, 0)),
            pl.BlockSpec((1, 1, s, HEAD_DIM), lambda i, j: (PB_COL_VA // HEAD_DIM + j, i, 0, 0)),
            wn_in,
        ],
        out_specs=[pl.BlockSpec((1, s, gw), lambda i, j: (i, 0, j)), wn_out],
        out_shape=[jax.ShapeDtypeStruct((b, s, A_Q_W), BF16), wn_shape],
        scratch_shapes=[pltpu.VMEM((A_KV_HEADS, len(offsets), A_GROUP * Q_BLOCK, kw), F32)],
        compiler_params=_params("arbitrary", "arbitrary"),
        name="attn_a",
    )(jnp.asarray(bkt), table_flat, sink, pb, pb, pb, w_next)


def _attn_b_kernel(*refs, seq, groups):
    ng = len(groups)
    it = iter(refs)
    bkt_refs = [next(it) for _ in range(ng)]
    table_ref = next(it)
    qkv_refs = [[next(it) for _ in range(3)] for _ in range(ng)]
    wn_ref, yb_ref, wn_out_ref = next(it), next(it), next(it)
    bias_refs = [next(it) for _ in range(ng)]
    o_acc, l_acc, stage_in, stage_out = next(it), next(it), next(it), next(it)
    head = pl.program_id(1)
    wn_out_ref[...] = wn_ref[...].astype(BF16)

    @pl.when(pl.program_id(0) == 0)
    def _():
        for g, grp in enumerate(groups):
            _build_bias(bkt_refs[g], table_ref, bias_refs[g], head, grp["head0"] + head, 1,
                        grp["buckets_used"])

    for g, grp in enumerate(groups):
        dil, sub, half, chain_group = grp["dil"], grp["sub"], grp["half"], grp["chain_group"]
        kw, nb, _ = _window_geometry(sub, half)
        stage = STAGE_STRIDE if dil > STAGE_STRIDE else 1
        hop2 = dil // stage
        accs = (o_acc, l_acc)

        def rows(residue, start, size, dil=dil, stage=stage, hop2=hop2):
            if dil == 1:
                return pl.ds(start, size)
            if stage > 1:
                return pl.ds(residue // stage + start * hop2, size, stride=hop2)
            return pl.ds(residue + start * dil, size, stride=dil)

        def load(t, residue, start, size, g=g, stage=stage, rows=rows):
            if stage > 1:
                return stage_in[t, residue % stage, rows(residue, start, size), :]
            return qkv_refs[g][t][0, 0, rows(residue, start, size), :]

        def store(t, residue, start, value, g=g, stage=stage, rows=rows, accs=accs):
            if stage > 1:
                stage_out[t, residue % stage, rows(residue, start, Q_BLOCK), :] = value
            else:
                accs[t][g, rows(residue, start, Q_BLOCK), :] = value

        def bias(residue, kind, g=g):
            return bias_refs[g][head, kind]

        run = functools.partial(_softmax_chains, load=load, store=store, bias=bias, kw=kw,
                                want_lse=True)
        if dil == 1:
            per_iter = min(chain_group, nb)
            assert nb % per_iter == 0

            def body(it_, carry, run=run, per_iter=per_iter, sub=sub, half=half):
                run([(0,) + _block_geometry(it_ * per_iter + di, sub, half) for di in range(per_iter)])
                return carry

            lax.fori_loop(0, nb // per_iter, body, 0)
            continue
        if stage > 1:
            for t in range(3):
                for r1 in range(stage):
                    stage_in[t, r1] = qkv_refs[g][t][0, 0, pl.ds(r1, seq // stage, stride=stage), :]
        work = [(r,) + _block_geometry(i, sub, half) for r in range(dil) for i in range(nb)]
        for c in range(0, len(work), chain_group):
            run(work[c:c + chain_group])
        if stage > 1:
            for t in range(2):
                for r1 in range(stage):
                    accs[t][g, pl.ds(r1, seq // stage, stride=stage), :] = stage_out[t, r1]

    def merge(c, carry):
        rws = pl.ds(pl.multiple_of(c * MERGE_ROWS, MERGE_ROWS), MERGE_ROWS)
        ls = [l_acc[g, rws, :] for g in range(ng)]
        mx = functools.reduce(jnp.maximum, ls)
        es = [jnp.exp(l - mx) for l in ls]
        tot = functools.reduce(lambda a, e: a + e, es)
        y = functools.reduce(lambda a, e: a + e, [e * o_acc[g, rws, :] for g, e in enumerate(es)])
        yb_ref[0, rws, :] = (y / tot).astype(yb_ref.dtype)
        return carry

    lax.fori_loop(0, seq // MERGE_ROWS, merge, 0)


def _attn_b(pb, table_flat, w_next):
    _, b, s, _ = pb.shape
    nh = B_HEADS_PER_GROUP
    wn_in, wn_out, wn_shape = _cast_rider(w_next, b * nh, lambda i, h: i * nh + h)
    groups, bkts, bias_scratch = [], [], []
    for gi, (window, dil) in enumerate(B_PATTERNS):
        sub, half = s // dil, window // (2 * dil)
        kw, _, offsets = _window_geometry(sub, half)
        bkt = _band_buckets(sub, half, dil)
        bkts.append(bkt)
        bias_scratch.append(pltpu.VMEM((nh, len(offsets), Q_BLOCK, kw), F32))
        groups.append(dict(dil=dil, sub=sub, half=half, chain_group=CHAIN_GROUP[gi],
                           head0=A_Q_HEADS + gi * nh, buckets_used=_buckets_used(bkt)))

    def qkv_spec(gi, part):
        base = (PB_COL_QB + part * B_W + gi * B_GROUP_W) // HEAD_DIM
        return pl.BlockSpec((1, 1, s, HEAD_DIM), lambda i, h: (base + h, i, 0, 0))

    ng = len(groups)
    return pl.pallas_call(
        functools.partial(_attn_b_kernel, seq=s, groups=groups),
        grid=(b, nh),
        in_specs=[pl.BlockSpec(bkt.shape, lambda i, h: (0, 0, 0)) for bkt in bkts] + [_smem_spec()]
        + [qkv_spec(gi, part) for gi in range(ng) for part in range(3)] + [wn_in],
        out_specs=[pl.BlockSpec((1, s, HEAD_DIM), lambda i, h: (i, 0, h)), wn_out],
        out_shape=[jax.ShapeDtypeStruct((b, s, B_GROUP_W), BF16), wn_shape],
        scratch_shapes=bias_scratch + [
            pltpu.VMEM((ng, s, HEAD_DIM), F32), pltpu.VMEM((ng, s, HEAD_DIM), F32),
            pltpu.VMEM((3, STAGE_STRIDE, s // STAGE_STRIDE, HEAD_DIM), F32),
            pltpu.VMEM((2, STAGE_STRIDE, s // STAGE_STRIDE, HEAD_DIM), F32)],
        compiler_params=_params("arbitrary", "arbitrary"),
        name="attn_b",
    )(*[jnp.asarray(bkt) for bkt in bkts], table_flat, *([pb] * (3 * ng)), w_next)


def _mix_kernel(ya_ref, yb_ref, ga_ref, gb_ref, x_ref, wa_ref, wb_ref, wo_ref, g_ref, x1_ref, hf_ref):
    rp = x_ref.shape[0] // MIX_ROW_PARTS
    for r in range(0, x_ref.shape[0], rp):
        rows = slice(r, r + rp)
        ta = jnp.dot(ya_ref[rows, :], wa_ref[...], preferred_element_type=F32)
        tb = jnp.dot(yb_ref[rows, :], wb_ref[...], preferred_element_type=F32)
        merged = (jax.nn.sigmoid(ga_ref[rows, :].astype(F32)) * ta
                  + jax.nn.sigmoid(gb_ref[rows, :].astype(F32)) * tb)
        x1 = x_ref[rows, :] + jnp.dot(merged.astype(BF16), wo_ref[...], preferred_element_type=F32)
        x1_ref[rows, :] = x1
        hf_ref[rows, :] = _rmsnorm(x1, g_ref[...]).astype(BF16)


def _resident(shape):
    return pl.BlockSpec(shape, lambda i: (0,) * len(shape), pipeline_mode=pl.Buffered(1))


def _mix(ya, yb, pa2d, x2d, wa, wb, wo, gain, *, tm=512):
    m, d = x2d.shape
    assert PA_COL_GA % d == 0 and PA_COL_GB % d == 0

    def rows(width, col_block=0):
        return pl.BlockSpec((tm, width), lambda i: (i, col_block))

    return pl.pallas_call(
        _mix_kernel,
        grid=(m // tm,),
        in_specs=[rows(A_Q_W), rows(B_GROUP_W),
                  rows(d, PA_COL_GA // d), rows(d, PA_COL_GB // d), rows(d),
                  _resident(wa.shape), _resident(wb.shape), _resident(wo.shape), _resident(gain.shape)],
        out_specs=[rows(d), rows(d)],
        out_shape=[jax.ShapeDtypeStruct((m, d), F32), jax.ShapeDtypeStruct((m, d), BF16)],
        compiler_params=_params("parallel"),
        name="mix",
    )(ya, yb, pa2d, pa2d, x2d, wa, wb, wo, gain)


def _ffn_up_kernel(hf_ref, wg_ref, wu_ref, cw_ref, cb_ref, wn_ref, act_ref, wn_out_ref):
    wn_out_ref[...] = wn_ref[...].astype(BF16)
    wg = wg_ref[...].astype(BF16)
    wu = wu_ref[...].astype(BF16)
    rp = hf_ref.shape[0] // FFN_ROW_PARTS
    parts = [(jnp.dot(hf_ref[r:r + rp, :], wg, preferred_element_type=F32),
              jnp.dot(hf_ref[r:r + rp, :], wu, preferred_element_type=F32))
             for r in range(0, hf_ref.shape[0], rp)]
    g = jnp.concatenate([p[0] for p in parts], axis=0)
    u = jnp.concatenate([p[1] for p in parts], axis=0)
    s = g.shape[0]
    row = lax.broadcasted_iota(jnp.int32, g.shape, 0)
    prev = jnp.where(row == 0, 0.0, pltpu.roll(g, 1, 0))
    nxt = jnp.where(row == s - 1, 0.0, pltpu.roll(g, s - 1, 0))
    cw = cw_ref[...]
    conv = prev * cw[0:1] + g * cw[1:2] + nxt * cw[2:3] + cb_ref[...]
    act_ref[...] = (jax.nn.gelu(conv) * u).astype(act_ref.dtype)


def _ffn_up(hf, wg, wu, cw, cb, w_next, *, seq, tf=512):
    m, d = hf.shape
    f = wg.shape[1]
    nj = f // tf
    wn_in, wn_out, wn_shape = _cast_rider(w_next, (m // seq) * nj, lambda i, j: i * nj + j)
    return pl.pallas_call(
        _ffn_up_kernel,
        grid=(m // seq, f // tf),
        in_specs=[
            pl.BlockSpec((seq, d), lambda i, j: (i, 0)),
            pl.BlockSpec((d, tf), lambda i, j: (0, j)),
            pl.BlockSpec((d, tf), lambda i, j: (0, j)),
            pl.BlockSpec((cw.shape[0], tf), lambda i, j: (0, j)),
            pl.BlockSpec((1, tf), lambda i, j: (0, j)),
            wn_in,
        ],
        out_specs=[pl.BlockSpec((seq, tf), lambda i, j: (i, j)), wn_out],
        out_shape=[jax.ShapeDtypeStruct((m, f), BF16), wn_shape],
        compiler_params=_params("parallel", "arbitrary"),
        name="ffn_up",
    )(hf, wg, wu, cw, cb, w_next)


def _ffn_down_kernel(act_ref, w_ref, x_ref, wn_ref, o_ref, wn_out_ref):
    wn_out_ref[...] = wn_ref[...].astype(BF16)
    o_ref[...] = x_ref[...] + jnp.dot(act_ref[...], w_ref[...], preferred_element_type=F32)


def _ffn_down(act, wd, x1, w_next, *, tm=512, tn=1024):
    m, f = act.shape
    n = wd.shape[1]
    ni = m // tm
    wn_in, wn_out, wn_shape = _cast_rider(w_next, (n // tn) * ni, lambda j, i: j * ni + i)
    return pl.pallas_call(
        _ffn_down_kernel,
        grid=(n // tn, m // tm),
        in_specs=[
            pl.BlockSpec((tm, f), lambda j, i: (i, 0)),
            pl.BlockSpec((f, tn), lambda j, i: (0, j)),
            pl.BlockSpec((tm, tn), lambda j, i: (i, j)),
            wn_in,
        ],
        out_specs=[pl.BlockSpec((tm, tn), lambda j, i: (i, j)), wn_out],
        out_shape=[jax.ShapeDtypeStruct((m, n), F32), wn_shape],
        compiler_params=_params("parallel", "parallel"),
        name="ffn_down",
    )(act, wd, x1, w_next)


def _ple_kernel(x_ref, p_ref, gp_ref, wg_ref, wp_ref, gf_ref, o_ref, *, final):
    rp = x_ref.shape[0] // PLE_ROW_PARTS
    for r in range(0, x_ref.shape[0], rp):
        rows = slice(r, r + rp)
        x = x_ref[rows, :]
        hp = _rmsnorm(x, gp_ref[...]).astype(BF16)
        gate = jax.nn.sigmoid(jnp.dot(hp, wg_ref[...], preferred_element_type=F32))
        emb = jnp.dot(p_ref[rows, :].astype(BF16), wp_ref[...], preferred_element_type=F32)
        y = x + gate * emb
        o_ref[rows, :] = _rmsnorm(y, gf_ref[...]) if final else y


def _ple(x2, p2d, gain_p, wpg, wpp, gain_f, *, final, tm=512):
    m, d = x2.shape
    pd = p2d.shape[1]
    return pl.pallas_call(
        functools.partial(_ple_kernel, final=final),
        grid=(m // tm,),
        in_specs=[
            pl.BlockSpec((tm, d), lambda i: (i, 0)),
            pl.BlockSpec((tm, pd), lambda i: (i, 0)),
            _resident(gain_p.shape), _resident(wpg.shape), _resident(wpp.shape),
            _resident(gain_f.shape),
        ],
        out_specs=pl.BlockSpec((tm, d), lambda i: (i, 0)),
        out_shape=jax.ShapeDtypeStruct((m, d), F32),
        compiler_params=_params("parallel"),
        name="ple",
    )(x2, p2d, gain_p, wpg, wpp, gain_f)


def kernel(x, p, rel_bias_table, attn_norm, w_in, sink_a, w_branch_a, w_branch_b, w_out,
           ffn_norm, w_ffn_gate, w_ffn_up, conv_w, conv_b, w_ffn_down,
           ple_norm, w_ple_gate, w_ple_proj, final_norm):
    b, s, d = x.shape
    depth = w_in.shape[0]
    assert d == D_MODEL and w_in.shape[2] == IN_PROJ_W and s % (Q_BLOCK * B_PATTERNS[-1][1]) == 0
    assert rel_bias_table.shape == (N_BUCKETS, N_BIAS_HEADS)
    m = b * s
    x2d = x.reshape(m, d)
    table_flat = rel_bias_table.reshape(-1)
    for i in range(depth):
        pb2d, pa2d = _inproj(x2d, attn_norm[i][None], w_in[i])
        pb = pb2d.reshape(PB_W // HEAD_DIM, b, s, HEAD_DIM)
        ya, wa = _attn_a(pb, table_flat, sink_a[i], w_branch_a[i])
        yb, wo = _attn_b(pb, table_flat, w_out[i])
        x1, hf = _mix(ya.reshape(m, A_Q_W), yb.reshape(m, B_GROUP_W), pa2d, x2d, wa,
                      w_branch_b[i].astype(BF16), wo, ffn_norm[i][None])
        act, wd = _ffn_up(hf, w_ffn_gate[i], w_ffn_up[i], conv_w[i], conv_b[i][None], w_ffn_down[i],
                          seq=s)
        x2, wpg = _ffn_down(act, wd, x1, w_ple_gate[i])
        x2d = _ple(x2, p[i].reshape(m, -1), ple_norm[i][None], wpg,
                   w_ple_proj[i].astype(BF16), final_norm[None], final=i == depth - 1)
    return x2d.reshape(b, s, d)
```

```python
import functools
import math

import jax
import jax.numpy as jnp
import numpy as np
from jax import lax
from jax.experimental import pallas as pl
from jax.experimental.pallas import tpu as pltpu

D_MODEL = 2048
HEAD_DIM = 128
A_Q_HEADS = 8
A_KV_HEADS = 2
A_GROUP = A_Q_HEADS // A_KV_HEADS
A_HALF_WINDOW = 128
B_PATTERNS = ((128, 1), (512, 4), (2048, 16))
B_HEADS_PER_GROUP = 4
N_BUCKETS = 32
MAX_DISTANCE = 1024
N_BIAS_HEADS = A_Q_HEADS + len(B_PATTERNS) * B_HEADS_PER_GROUP
A_Q_W = A_Q_HEADS * HEAD_DIM
A_KV_W = A_KV_HEADS * HEAD_DIM
B_GROUP_W = B_HEADS_PER_GROUP * HEAD_DIM
B_W = len(B_PATTERNS) * B_GROUP_W
IN_PROJ_W = A_Q_W + 2 * A_KV_W + 3 * B_W + 2 * D_MODEL
RMS_EPS = 1e-6
NEG_INF = -1e30
LOG2_E = math.log2(math.e)
QK_SCALE_LOG2 = HEAD_DIM ** -0.5 * LOG2_E

PB_W = A_Q_W + 2 * A_KV_W + 3 * B_W
PB_COL_KA = A_Q_W
PB_COL_VA = PB_COL_KA + A_KV_W
PB_COL_QB = PB_COL_VA + A_KV_W
PA_W = 2 * D_MODEL
PA_COL_GA = 0
PA_COL_GB = D_MODEL

Q_BLOCK = 128
ATTN_UNROLL = 2
STAGE_STRIDE = 4
CHAIN_GROUP = (16, 1, 16)
A_BLOCKS_PER_GROUP = 2
MERGE_ROWS = 256
FFN_ROW_PARTS = 2
MIX_ROW_PARTS = 2
PLE_ROW_PARTS = 1

VMEM_LIMIT_BYTES = 56 * 1024 * 1024

BF16 = jnp.bfloat16
F32 = jnp.float32


def _params(*semantics):
    return pltpu.CompilerParams(dimension_semantics=semantics, vmem_limit_bytes=VMEM_LIMIT_BYTES)


def _rmsnorm(x, g):
    y = x * lax.rsqrt(jnp.mean(x * x, axis=-1, keepdims=True) + RMS_EPS)
    return y * g


def _cast_rider(w, n_steps, step_of):
    rows = w.shape[0] // n_steps
    assert rows * n_steps == w.shape[0] and rows % 16 == 0
    spec = pl.BlockSpec((rows, w.shape[1]), lambda *ids: (step_of(*ids), 0))
    return spec, spec, jax.ShapeDtypeStruct(w.shape, BF16)


def _inproj_kernel(*refs, row_chunk, n_b_tiles, casts_weight):
    if casts_weight:
        x_ref, g_ref, w_ref, pb_ref, pa_ref, wbf_ref, h_ref = refs
    else:
        x_ref, g_ref, w_ref, _, _, pb_ref, pa_ref, h_ref = refs
    j = pl.program_id(1)

    def weight():
        if not casts_weight:
            return w_ref[...]
        w = w_ref[...].astype(BF16)
        wbf_ref[...] = w
        return w

    def store_slabs(rows, res):
        for s in range(pb_ref.shape[0]):
            pb_ref[s, rows, :] = res[:, s * HEAD_DIM:(s + 1) * HEAD_DIM]

    @pl.when(j == 0)
    def _():
        w = weight()
        for r in range(0, x_ref.shape[0], row_chunk):
            rows = slice(r, r + row_chunk)
            h = _rmsnorm(x_ref[rows, :], g_ref[...]).astype(BF16)
            h_ref[rows, :] = h
            store_slabs(rows, jnp.dot(h, w, preferred_element_type=F32))

    @pl.when((j > 0) & (j < n_b_tiles))
    def _():
        store_slabs(slice(None), jnp.dot(h_ref[...], weight(), preferred_element_type=F32))

    @pl.when(j >= n_b_tiles)
    def _():
        pa_ref[...] = jnp.dot(h_ref[...], weight(), preferred_element_type=F32).astype(BF16)


def _inproj(x2d, gain, w, *, tm=1024, tn=1024):
    m, k = x2d.shape
    assert PB_W % tn == 0 and PA_W % tn == 0 and w.shape[1] == PB_W + PA_W
    nb_t = PB_W // tn
    n_steps = w.shape[1] // tn
    w_spec = pl.BlockSpec((k, tn), lambda i, j: (0, j))

    def call(row0, n_rows, casts_weight, *operands):
        in_specs = [pl.BlockSpec((tm, k), lambda i, j: (i + row0, 0)),
                    pl.BlockSpec((1, k), lambda i, j: (0, 0)),
                    w_spec]
        out_specs = [
            pl.BlockSpec((tn // HEAD_DIM, tm, HEAD_DIM),
                         lambda i, j: (jnp.minimum(j, nb_t - 1), i + row0, 0)),
            pl.BlockSpec((tm, tn), lambda i, j: (i + row0, jnp.maximum(j - nb_t, 0))),
        ]
        out_shape = [jax.ShapeDtypeStruct((PB_W // HEAD_DIM, m, HEAD_DIM), F32),
                     jax.ShapeDtypeStruct((m, PA_W), BF16)]
        if casts_weight:
            out_specs.append(w_spec)
            out_shape.append(jax.ShapeDtypeStruct(w.shape, BF16))
            aliases = {}
        else:
            in_specs += [pl.BlockSpec(memory_space=pl.ANY)] * 2
            aliases = {3: 0, 4: 1}
        return pl.pallas_call(
            functools.partial(_inproj_kernel, row_chunk=256, n_b_tiles=nb_t,
                              casts_weight=casts_weight),
            grid=(n_rows, n_steps),
            in_specs=in_specs,
            out_specs=out_specs,
            out_shape=out_shape,
            input_output_aliases=aliases,
            scratch_shapes=[pltpu.VMEM((tm, k), BF16)],
            compiler_params=_params("arbitrary", "arbitrary"),
            name="inproj_first" if casts_weight else "inproj",
        )(*operands)

    pb, pa, w_bf16 = call(0, 1, True, x2d, gain, w)
    return call(1, m // tm - 1, False, x2d, gain, w_bf16, pb, pa)


def _t5_bucket_static(rel):
    half = N_BUCKETS // 2
    max_exact = half // 2
    n = np.abs(rel)
    side = np.where(rel > 0, half, 0)
    nf = np.maximum(n, 1).astype(np.float32)
    large = max_exact + (np.log(nf / max_exact) / math.log(MAX_DISTANCE / max_exact)
                         * (half - max_exact)).astype(np.int32)
    large = np.minimum(large, half - 1)
    return side + np.where(n < max_exact, n, large)


def _window_geometry(seq_len, half_w):
    nb = seq_len // Q_BLOCK
    if nb == 1:
        return seq_len, nb, (0,)
    kw = Q_BLOCK + 2 * half_w
    assert half_w <= Q_BLOCK and kw <= seq_len
    return kw, nb, (0, -half_w, Q_BLOCK - kw)


def _band_buckets(seq_len, half_w, dilation):
    kw, _, offsets = _window_geometry(seq_len, half_w)
    qi = np.arange(Q_BLOCK)[:, None]
    kj = np.arange(kw)[None, :]
    rel = np.stack([off + kj - qi for off in offsets])
    buckets = _t5_bucket_static(rel * dilation)
    return np.where(np.abs(rel) <= half_w, buckets, -1).astype(np.int32)


def _block_geometry(i, seq_len, half_w):
    kw, nb, _ = _window_geometry(seq_len, half_w)
    if isinstance(i, int):
        qs = i * Q_BLOCK
        return qs, min(max(qs - half_w, 0), seq_len - kw), 0 if i == 0 else (2 if i == nb - 1 else 1)
    qs = pl.multiple_of(i * Q_BLOCK, Q_BLOCK)
    ks = pl.multiple_of(jnp.clip(qs - half_w, 0, seq_len - kw), half_w)
    return qs, ks, jnp.where(i == 0, 0, jnp.where(i == nb - 1, 2, 1))


def _build_bias(bkt_ref, table_ref, bias_ref, slot, first_head, n_heads, buckets_used):
    for kind in range(bkt_ref.shape[0]):
        bkt = bkt_ref[kind]
        for h in range(n_heads):
            acc = jnp.full(bkt.shape, NEG_INF, F32)
            for bucket in buckets_used:
                acc = jnp.where(bkt == bucket,
                                table_ref[bucket * N_BIAS_HEADS + first_head + h] * LOG2_E, acc)
            bias_ref[slot, kind, h * Q_BLOCK:(h + 1) * Q_BLOCK, :] = acc


def _softmax_chains(items, *, load, store, bias, kw, sink=None, want_lse):
    ones = jnp.ones((kw, HEAD_DIM), BF16)
    scores = []
    for tag, qs, ks, kind in items:
        q = (load(0, tag, qs, Q_BLOCK) * QK_SCALE_LOG2).astype(BF16)
        s = lax.dot_general(q, load(1, tag, ks, kw).astype(BF16),
                            (((1,), (1,)), ((), ())), preferred_element_type=F32)
        scores.append(s + bias(tag, kind))
    for (tag, qs, ks, _), s in zip(items, scores):
        v_ext = jnp.concatenate([load(2, tag, ks, kw).astype(BF16), ones], axis=1)
        m = jnp.max(s, axis=-1, keepdims=True)
        if sink is not None:
            sk = sink(tag) * LOG2_E
            m = jnp.maximum(m, sk)
        p = jnp.exp2(s - m).astype(BF16)
        ov = jnp.dot(p, v_ext, preferred_element_type=F32)
        denom = ov[:, HEAD_DIM:]
        if sink is not None:
            denom = denom + jnp.exp2(sk - m)
        store(0, tag, qs, ov[:, :HEAD_DIM] / denom)
        if want_lse:
            store(1, tag, qs, m * (1.0 / LOG2_E) + jnp.log(denom))


def _attn_a_kernel(bkt_ref, table_ref, sink_ref, q_ref, k_ref, v_ref, wn_ref, o_ref, wn_out_ref,
                   bias_ref, *, seq_len, buckets_used):
    kv = pl.program_id(1)
    kw, nb, _ = _window_geometry(seq_len, A_HALF_WINDOW)
    wn_out_ref[...] = wn_ref[...].astype(BF16)

    @pl.when(pl.program_id(0) == 0)
    def _():
        _build_bias(bkt_ref, table_ref, bias_ref, kv, kv * A_GROUP, A_GROUP, buckets_used)

    def load(t, h, start, size):
        if t == 0:
            return q_ref[h, 0, pl.ds(start, size), :]
        return (k_ref, v_ref)[t - 1][0, 0, pl.ds(start, size), :]

    def store(t, h, start, value):
        o_ref[0, pl.ds(start, Q_BLOCK), h * HEAD_DIM:(h + 1) * HEAD_DIM] = value.astype(o_ref.dtype)

    def body(it, carry):
        items = [(h,) + _block_geometry(it * A_BLOCKS_PER_GROUP + di, seq_len, A_HALF_WINDOW)
                 for di in range(A_BLOCKS_PER_GROUP) for h in range(A_GROUP)]
        _softmax_chains(
            items, load=load, store=store, kw=kw, want_lse=False,
            bias=lambda h, kind: bias_ref[kv, kind, h * Q_BLOCK:(h + 1) * Q_BLOCK, :],
            sink=lambda h: jnp.full((Q_BLOCK, 1), sink_ref[kv * A_GROUP + h], F32))
        return carry

    assert nb % A_BLOCKS_PER_GROUP == 0
    lax.fori_loop(0, nb // A_BLOCKS_PER_GROUP, body, 0, unroll=ATTN_UNROLL)


def _buckets_used(bkt):
    return tuple(int(v) for v in np.unique(bkt) if v >= 0)


def _smem_spec():
    return pl.BlockSpec(memory_space=pltpu.SMEM)


def _attn_a(pb, table_flat, sink, w_next):
    _, b, s, _ = pb.shape
    wn_in, wn_out, wn_shape = _cast_rider(w_next, b * A_KV_HEADS, lambda i, j: i * A_KV_HEADS + j)
    gw = A_GROUP * HEAD_DIM
    kw, _, offsets = _window_geometry(s, A_HALF_WINDOW)
    bkt = _band_buckets(s, A_HALF_WINDOW, 1)
    kernel = functools.partial(_attn_a_kernel, seq_len=s, buckets_used=_buckets_used(bkt))
    return pl.pallas_call(
        kernel,
        grid=(b, A_KV_HEADS),
        in_specs=[
            pl.BlockSpec(bkt.shape, lambda i, j: (0, 0, 0)),
            _smem_spec(),
            _smem_spec(),
            pl.BlockSpec((A_GROUP, 1, s, HEAD_DIM), lambda i, j: (j, i, 0, 0)),
            pl.BlockSpec((1, 1, s, HEAD_DIM), lambda i, j: (PB_COL_KA // HEAD_DIM + j, i, 0, 0)),
            pl.BlockSpec((1, 1, s, HEAD_DIM), lambda i, j: (PB_COL_VA // HEAD_DIM + j, i, 0, 0)),
            wn_in,
        ],
        out_specs=[pl.BlockSpec((1, s, gw), lambda i, j: (i, 0, j)), wn_out],
        out_shape=[jax.ShapeDtypeStruct((b, s, A_Q_W), BF16), wn_shape],
        scratch_shapes=[pltpu.VMEM((A_KV_HEADS, len(offsets), A_GROUP * Q_BLOCK, kw), F32)],
        compiler_params=_params("arbitrary", "arbitrary"),
        name="attn_a",
    )(jnp.asarray(bkt), table_flat, sink, pb, pb, pb, w_next)


def _attn_b_kernel(*refs, seq, groups):
    ng = len(groups)
    it = iter(refs)
    bkt_refs = [next(it) for _ in range(ng)]
    table_ref = next(it)
    qkv_refs = [[next(it) for _ in range(3)] for _ in range(ng)]
    wn_ref, yb_ref, wn_out_ref = next(it), next(it), next(it)
    bias_refs = [next(it) for _ in range(ng)]
    o_acc, l_acc, stage_in, stage_out = next(it), next(it), next(it), next(it)
    head = pl.program_id(1)
    wn_out_ref[...] = wn_ref[...].astype(BF16)

    @pl.when(pl.program_id(0) == 0)
    def _():
        for g, grp in enumerate(groups):
            _build_bias(bkt_refs[g], table_ref, bias_refs[g], head, grp["head0"] + head, 1,
                        grp["buckets_used"])

    for g, grp in enumerate(groups):
        dil, sub, half, chain_group = grp["dil"], grp["sub"], grp["half"], grp["chain_group"]
        kw, nb, _ = _window_geometry(sub, half)
        stage = STAGE_STRIDE if dil > STAGE_STRIDE else 1
        hop2 = dil // stage
        accs = (o_acc, l_acc)

        def rows(residue, start, size, dil=dil, stage=stage, hop2=hop2):
            if dil == 1:
                return pl.ds(start, size)
            if stage > 1:
                return pl.ds(residue // stage + start * hop2, size, stride=hop2)
            return pl.ds(residue + start * dil, size, stride=dil)

        def load(t, residue, start, size, g=g, stage=stage, rows=rows):
            if stage > 1:
                return stage_in[t, residue % stage, rows(residue, start, size), :]
            return qkv_refs[g][t][0, 0, rows(residue, start, size), :]

        def store(t, residue, start, value, g=g, stage=stage, rows=rows, accs=accs):
            if stage > 1:
                stage_out[t, residue % stage, rows(residue, start, Q_BLOCK), :] = value
            else:
                accs[t][g, rows(residue, start, Q_BLOCK), :] = value

        def bias(residue, kind, g=g):
            return bias_refs[g][head, kind]

        run = functools.partial(_softmax_chains, load=load, store=store, bias=bias, kw=kw,
                                want_lse=True)
        if dil == 1:
            per_iter = min(chain_group, nb)
            assert nb % per_iter == 0

            def body(it_, carry, run=run, per_iter=per_iter, sub=sub, half=half):
                run([(0,) + _block_geometry(it_ * per_iter + di, sub, half) for di in range(per_iter)])
                return carry

            lax.fori_loop(0, nb // per_iter, body, 0)
            continue
        if stage > 1:
            for t in range(3):
                for r1 in range(stage):
                    stage_in[t, r1] = qkv_refs[g][t][0, 0, pl.ds(r1, seq // stage, stride=stage), :]
        work = [(r,) + _block_geometry(i, sub, half) for r in range(dil) for i in range(nb)]
        for c in range(0, len(work), chain_group):
            run(work[c:c + chain_group])
        if stage > 1:
            for t in range(2):
                for r1 in range(stage):
                    accs[t][g, pl.ds(r1, seq // stage, stride=stage), :] = stage_out[t, r1]

    def merge(c, carry):
        rws = pl.ds(pl.multiple_of(c * MERGE_ROWS, MERGE_ROWS), MERGE_ROWS)
        ls = [l_acc[g, rws, :] for g in range(ng)]
        mx = functools.reduce(jnp.maximum, ls)
        es = [jnp.exp(l - mx) for l in ls]
        tot = functools.reduce(lambda a, e: a + e, es)
        y = functools.reduce(lambda a, e: a + e, [e * o_acc[g, rws, :] for g, e in enumerate(es)])
        yb_ref[0, rws, :] = (y / tot).astype(yb_ref.dtype)
        return carry

    lax.fori_loop(0, seq // MERGE_ROWS, merge, 0)


def _attn_b(pb, table_flat, w_next):
    _, b, s, _ = pb.shape
    nh = B_HEADS_PER_GROUP
    wn_in, wn_out, wn_shape = _cast_rider(w_next, b * nh, lambda i, h: i * nh + h)
    groups, bkts, bias_scratch = [], [], []
    for gi, (window, dil) in enumerate(B_PATTERNS):
        sub, half = s // dil, window // (2 * dil)
        kw, _, offsets = _window_geometry(sub, half)
        bkt = _band_buckets(sub, half, dil)
        bkts.append(bkt)
        bias_scratch.append(pltpu.VMEM((nh, len(offsets), Q_BLOCK, kw), F32))
        groups.append(dict(dil=dil, sub=sub, half=half, chain_group=CHAIN_GROUP[gi],
                           head0=A_Q_HEADS + gi * nh, buckets_used=_buckets_used(bkt)))

    def qkv_spec(gi, part):
        base = (PB_COL_QB + part * B_W + gi * B_GROUP_W) // HEAD_DIM
        return pl.BlockSpec((1, 1, s, HEAD_DIM), lambda i, h: (base + h, i, 0, 0))

    ng = len(groups)
    return pl.pallas_call(
        functools.partial(_attn_b_kernel, seq=s, groups=groups),
        grid=(b, nh),
        in_specs=[pl.BlockSpec(bkt.shape, lambda i, h: (0, 0, 0)) for bkt in bkts] + [_smem_spec()]
        + [qkv_spec(gi, part) for gi in range(ng) for part in range(3)] + [wn_in],
        out_specs=[pl.BlockSpec((1, s, HEAD_DIM), lambda i, h: (i, 0, h)), wn_out],
        out_shape=[jax.ShapeDtypeStruct((b, s, B_GROUP_W), BF16), wn_shape],
        scratch_shapes=bias_scratch + [
            pltpu.VMEM((ng, s, HEAD_DIM), F32), pltpu.VMEM((ng, s, HEAD_DIM), F32),
            pltpu.VMEM((3, STAGE_STRIDE, s // STAGE_STRIDE, HEAD_DIM), F32),
            pltpu.VMEM((2, STAGE_STRIDE, s // STAGE_STRIDE, HEAD_DIM), F32)],
        compiler_params=_params("arbitrary", "arbitrary"),
        name="attn_b",
    )(*[jnp.asarray(bkt) for bkt in bkts], table_flat, *([pb] * (3 * ng)), w_next)


def _mix_kernel(ya_ref, yb_ref, ga_ref, gb_ref, x_ref, wa_ref, wb_ref, wo_ref, g_ref, x1_ref, hf_ref):
    rp = x_ref.shape[0] // MIX_ROW_PARTS
    for r in range(0, x_ref.shape[0], rp):
        rows = slice(r, r + rp)
        ta = jnp.dot(ya_ref[rows, :], wa_ref[...], preferred_element_type=F32)
        tb = jnp.dot(yb_ref[rows, :], wb_ref[...], preferred_element_type=F32)
        merged = (jax.nn.sigmoid(ga_ref[rows, :].astype(F32)) * ta
                  + jax.nn.sigmoid(gb_ref[rows, :].astype(F32)) * tb)
        x1 = x_ref[rows, :] + jnp.dot(merged.astype(BF16), wo_ref[...], preferred_element_type=F32)
        x1_ref[rows, :] = x1
        hf_ref[rows, :] = _rmsnorm(x1, g_ref[...]).astype(BF16)


def _resident(shape):
    return pl.BlockSpec(shape, lambda i: (0,) * len(shape), pipeline_mode=pl.Buffered(1))


def _mix(ya, yb, pa2d, x2d, wa, wb, wo, gain, *, tm=512):
    m, d = x2d.shape
    assert PA_COL_GA % d == 0 and PA_COL_GB % d == 0

    def rows(width, col_block=0):
        return pl.BlockSpec((tm, width), lambda i: (i, col_block))

    return pl.pallas_call(
        _mix_kernel,
        grid=(m // tm,),
        in_specs=[rows(A_Q_W), rows(B_GROUP_W),
                  rows(d, PA_COL_GA // d), rows(d, PA_COL_GB // d), rows(d),
                  _resident(wa.shape), _resident(wb.shape), _resident(wo.shape), _resident(gain.shape)],
        out_specs=[rows(d), rows(d)],
        out_shape=[jax.ShapeDtypeStruct((m, d), F32), jax.ShapeDtypeStruct((m, d), BF16)],
        compiler_params=_params("parallel"),
        name="mix",
    )(ya, yb, pa2d, pa2d, x2d, wa, wb, wo, gain)


def _ffn_up_kernel(hf_ref, wg_ref, wu_ref, cw_ref, cb_ref, wn_ref, act_ref, wn_out_ref):
    wn_out_ref[...] = wn_ref[...].astype(BF16)
    wg = wg_ref[...].astype(BF16)
    wu = wu_ref[...].astype(BF16)
    rp = hf_ref.shape[0] // FFN_ROW_PARTS
    parts = [(jnp.dot(hf_ref[r:r + rp, :], wg, preferred_element_type=F32),
              jnp.dot(hf_ref[r:r + rp, :], wu, preferred_element_type=F32))
             for r in range(0, hf_ref.shape[0], rp)]
    g = jnp.concatenate([p[0] for p in parts], axis=0)
    u = jnp.concatenate([p[1] for p in parts], axis=0)
    s = g.shape[0]
    row = lax.broadcasted_iota(jnp.int32, g.shape, 0)
    prev = jnp.where(row == 0, 0.0, pltpu.roll(g, 1, 0))
    nxt = jnp.where(row == s - 1, 0.0, pltpu.roll(g, s - 1, 0))
    cw = cw_ref[...]
    conv = prev * cw[0:1] + g * cw[1:2] + nxt * cw[2:3] + cb_ref[...]
    act_ref[...] = (jax.nn.gelu(conv) * u).astype(act_ref.dtype)


def _ffn_up(hf, wg, wu, cw, cb, w_next, *, seq, tf=512):
    m, d = hf.shape
    f = wg.shape[1]
    nj = f // tf
    wn_in, wn_out, wn_shape = _cast_rider(w_next, (m // seq) * nj, lambda i, j: i * nj + j)
    return pl.pallas_call(
        _ffn_up_kernel,
        grid=(m // seq, f // tf),
        in_specs=[
            pl.BlockSpec((seq, d), lambda i, j: (i, 0)),
            pl.BlockSpec((d, tf), lambda i, j: (0, j)),
            pl.BlockSpec((d, tf), lambda i, j: (0, j)),
            pl.BlockSpec((cw.shape[0], tf), lambda i, j: (0, j)),
            pl.BlockSpec((1, tf), lambda i, j: (0, j)),
            wn_in,
        ],
        out_specs=[pl.BlockSpec((seq, tf), lambda i, j: (i, j)), wn_out],
        out_shape=[jax.ShapeDtypeStruct((m, f), BF16), wn_shape],
        compiler_params=_params("parallel", "arbitrary"),
        name="ffn_up",
    )(hf, wg, wu, cw, cb, w_next)


def _ffn_down_kernel(act_ref, w_ref, x_ref, wn_ref, o_ref, wn_out_ref):
    wn_out_ref[...] = wn_ref[...].astype(BF16)
    o_ref[...] = x_ref[...] + jnp.dot(act_ref[...], w_ref[...], preferred_element_type=F32)


def _ffn_down(act, wd, x1, w_next, *, tm=512, tn=1024):
    m, f = act.shape
    n = wd.shape[1]
    ni = m // tm
    wn_in, wn_out, wn_shape = _cast_rider(w_next, (n // tn) * ni, lambda j, i: j * ni + i)
    return pl.pallas_call(
        _ffn_down_kernel,
        grid=(n // tn, m // tm),
        in_specs=[
            pl.BlockSpec((tm, f), lambda j, i: (i, 0)),
            pl.BlockSpec((f, tn), lambda j, i: (0, j)),
            pl.BlockSpec((tm, tn), lambda j, i: (i, j)),
            wn_in,
        ],
        out_specs=[pl.BlockSpec((tm, tn), lambda j, i: (i, j)), wn_out],
        out_shape=[jax.ShapeDtypeStruct((m, n), F32), wn_shape],
        compiler_params=_params("parallel", "parallel"),
        name="ffn_down",
    )(act, wd, x1, w_next)


def _ple_kernel(x_ref, p_ref, gp_ref, wg_ref, wp_ref, gf_ref, o_ref, *, final):
    rp = x_ref.shape[0] // PLE_ROW_PARTS
    for r in range(0, x_ref.shape[0], rp):
        rows = slice(r, r + rp)
        x = x_ref[rows, :]
        hp = _rmsnorm(x, gp_ref[...]).astype(BF16)
        gate = jax.nn.sigmoid(jnp.dot(hp, wg_ref[...], preferred_element_type=F32))
        emb = jnp.dot(p_ref[rows, :].astype(BF16), wp_ref[...], preferred_element_type=F32)
        y = x + gate * emb
        o_ref[rows, :] = _rmsnorm(y, gf_ref[...]) if final else y


def _ple(x2, p2d, gain_p, wpg, wpp, gain_f, *, final, tm=512):
    m, d = x2.shape
    pd = p2d.shape[1]
    return pl.pallas_call(
        functools.partial(_ple_kernel, final=final),
        grid=(m // tm,),
        in_specs=[
            pl.BlockSpec((tm, d), lambda i: (i, 0)),
            pl.BlockSpec((tm, pd), lambda i: (i, 0)),
            _resident(gain_p.shape), _resident(wpg.shape), _resident(wpp.shape),
            _resident(gain_f.shape),
        ],
        out_specs=pl.BlockSpec((tm, d), lambda i: (i, 0)),
        out_shape=jax.ShapeDtypeStruct((m, d), F32),
        compiler_params=_params("parallel"),
        name="ple",
    )(x2, p2d, gain_p, wpg, wpp, gain_f)


def kernel(x, p, rel_bias_table, attn_norm, w_in, sink_a, w_branch_a, w_branch_b, w_out,
           ffn_norm, w_ffn_gate, w_ffn_up, conv_w, conv_b, w_ffn_down,
           ple_norm, w_ple_gate, w_ple_proj, final_norm):
    b, s, d = x.shape
    depth = w_in.shape[0]
    assert d == D_MODEL and w_in.shape[2] == IN_PROJ_W and s % (Q_BLOCK * B_PATTERNS[-1][1]) == 0
    assert rel_bias_table.shape == (N_BUCKETS, N_BIAS_HEADS)
    m = b * s
    x2d = x.reshape(m, d)
    table_flat = rel_bias_table.reshape(-1)
    for i in range(depth):
        pb2d, pa2d = _inproj(x2d, attn_norm[i][None], w_in[i])
        pb = pb2d.reshape(PB_W // HEAD_DIM, b, s, HEAD_DIM)
        ya, wa = _attn_a(pb, table_flat, sink_a[i], w_branch_a[i])
        yb, wo = _attn_b(pb, table_flat, w_out[i])
        x1, hf = _mix(ya.reshape(m, A_Q_W), yb.reshape(m, B_GROUP_W), pa2d, x2d, wa,
                      w_branch_b[i].astype(BF16), wo, ffn_norm[i][None])
        act, wd = _ffn_up(hf, w_ffn_gate[i], w_ffn_up[i], conv_w[i], conv_b[i][None], w_ffn_down[i],
                          seq=s)
        x2, wpg = _ffn_down(act, wd, x1, w_ple_gate[i])
        x2d = _ple(x2, p[i].reshape(m, -1), ple_norm[i][None], wpg,
                   w_ple_proj[i].astype(BF16), final_norm[None], final=i == depth - 1)
    return x2d.reshape(b, s, d)
```

```python
import functools
import math

import jax
import jax.numpy as jnp
import numpy as np
from jax import lax
from jax.experimental import pallas as pl
from jax.experimental.pallas import tpu as pltpu

D_MODEL = 2048
HEAD_DIM = 128
A_Q_HEADS = 8
A_KV_HEADS = 2
A_GROUP = A_Q_HEADS // A_KV_HEADS
A_HALF_WINDOW = 128
B_PATTERNS = ((128, 1), (512, 4), (2048, 16))
B_HEADS_PER_GROUP = 4
N_BUCKETS = 32
MAX_DISTANCE = 1024
N_BIAS_HEADS = A_Q_HEADS + len(B_PATTERNS) * B_HEADS_PER_GROUP
A_Q_W = A_Q_HEADS * HEAD_DIM
A_KV_W = A_KV_HEADS * HEAD_DIM
B_GROUP_W = B_HEADS_PER_GROUP * HEAD_DIM
B_W = len(B_PATTERNS) * B_GROUP_W
IN_PROJ_W = A_Q_W + 2 * A_KV_W + 3 * B_W + 2 * D_MODEL
RMS_EPS = 1e-6
NEG_INF = -1e30
LOG2_E = math.log2(math.e)
QK_SCALE_LOG2 = HEAD_DIM ** -0.5 * LOG2_E

PB_W = A_Q_W + 2 * A_KV_W + 3 * B_W
PB_COL_KA = A_Q_W
PB_COL_VA = PB_COL_KA + A_KV_W
PB_COL_QB = PB_COL_VA + A_KV_W
PA_W = 2 * D_MODEL
PA_COL_GA = 0
PA_COL_GB = D_MODEL

Q_BLOCK = 128
ATTN_UNROLL = 2
STAGE_STRIDE = 4
CHAIN_GROUP = (16, 1, 16)
A_BLOCKS_PER_GROUP = 2
MERGE_ROWS = 256
FFN_ROW_PARTS = 2
MIX_ROW_PARTS = 2
PLE_ROW_PARTS = 1

VMEM_LIMIT_BYTES = 56 * 1024 * 1024

BF16 = jnp.bfloat16
F32 = jnp.float32


def _params(*semantics):
    return pltpu.CompilerParams(dimension_semantics=semantics, vmem_limit_bytes=VMEM_LIMIT_BYTES)


def _rmsnorm(x, g):
    y = x * lax.rsqrt(jnp.mean(x * x, axis=-1, keepdims=True) + RMS_EPS)
    return y * g


def _cast_rider(w, n_steps, step_of):
    rows = w.shape[0] // n_steps
    assert rows * n_steps == w.shape[0] and rows % 16 == 0
    spec = pl.BlockSpec((rows, w.shape[1]), lambda *ids: (step_of(*ids), 0))
    return spec, spec, jax.ShapeDtypeStruct(w.shape, BF16)


def _inproj_kernel(*refs, row_chunk, n_b_tiles, casts_weight):
    if casts_weight:
        x_ref, g_ref, w_ref, pb_ref, pa_ref, wbf_ref, h_ref = refs
    else:
        x_ref, g_ref, w_ref, _, _, pb_ref, pa_ref, h_ref = refs
    j = pl.program_id(1)

    def weight():
        if not casts_weight:
            return w_ref[...]
        w = w_ref[...].astype(BF16)
        wbf_ref[...] = w
        return w

    def store_slabs(rows, res):
        for s in range(pb_ref.shape[0]):
            pb_ref[s, rows, :] = res[:, s * HEAD_DIM:(s + 1) * HEAD_DIM]

    @pl.when(j == 0)
    def _():
        w = weight()
        for r in range(0, x_ref.shape[0], row_chunk):
            rows = slice(r, r + row_chunk)
            h = _rmsnorm(x_ref[rows, :], g_ref[...]).astype(BF16)
            h_ref[rows, :] = h
            store_slabs(rows, jnp.dot(h, w, preferred_element_type=F32))

    @pl.when((j > 0) & (j < n_b_tiles))
    def _():
        store_slabs(slice(None), jnp.dot(h_ref[...], weight(), preferred_element_type=F32))

    @pl.when(j >= n_b_tiles)
    def _():
        pa_ref[...] = jnp.dot(h_ref[...], weight(), preferred_element_type=F32).astype(BF16)


def _inproj(x2d, gain, w, *, tm=1024, tn=1024):
    m, k = x2d.shape
    assert PB_W % tn == 0 and PA_W % tn == 0 and w.shape[1] == PB_W + PA_W
    nb_t = PB_W // tn
    n_steps = w.shape[1] // tn
    w_spec = pl.BlockSpec((k, tn), lambda i, j: (0, j))

    def call(row0, n_rows, casts_weight, *operands):
        in_specs = [pl.BlockSpec((tm, k), lambda i, j: (i + row0, 0)),
                    pl.BlockSpec((1, k), lambda i, j: (0, 0)),
                    w_spec]
        out_specs = [
            pl.BlockSpec((tn // HEAD_DIM, tm, HEAD_DIM),
                         lambda i, j: (jnp.minimum(j, nb_t - 1), i + row0, 0)),
            pl.BlockSpec((tm, tn), lambda i, j: (i + row0, jnp.maximum(j - nb_t, 0))),
        ]
        out_shape = [jax.ShapeDtypeStruct((PB_W // HEAD_DIM, m, HEAD_DIM), F32),
                     jax.ShapeDtypeStruct((m, PA_W), BF16)]
        if casts_weight:
            out_specs.append(w_spec)
            out_shape.append(jax.ShapeDtypeStruct(w.shape, BF16))
            aliases = {}
        else:
            in_specs += [pl.BlockSpec(memory_space=pl.ANY)] * 2
            aliases = {3: 0, 4: 1}
        return pl.pallas_call(
            functools.partial(_inproj_kernel, row_chunk=256, n_b_tiles=nb_t,
                              casts_weight=casts_weight),
            grid=(n_rows, n_steps),
            in_specs=in_specs,
            out_specs=out_specs,
            out_shape=out_shape,
            input_output_aliases=aliases,
            scratch_shapes=[pltpu.VMEM((tm, k), BF16)],
            compiler_params=_params("arbitrary", "arbitrary"),
            name="inproj_first" if casts_weight else "inproj",
        )(*operands)

    pb, pa, w_bf16 = call(0, 1, True, x2d, gain, w)
    return call(1, m // tm - 1, False, x2d, gain, w_bf16, pb, pa)


def _t5_bucket_static(rel):
    half = N_BUCKETS // 2
    max_exact = half // 2
    n = np.abs(rel)
    side = np.where(rel > 0, half, 0)
    nf = np.maximum(n, 1).astype(np.float32)
    large = max_exact + (np.log(nf / max_exact) / math.log(MAX_DISTANCE / max_exact)
                         * (half - max_exact)).astype(np.int32)
    large = np.minimum(large, half - 1)
    return side + np.where(n < max_exact, n, large)


def _window_geometry(seq_len, half_w):
    nb = seq_len // Q_BLOCK
    if nb == 1:
        return seq_len, nb, (0,)
    kw = Q_BLOCK + 2 * half_w
    assert half_w <= Q_BLOCK and kw <= seq_len
    return kw, nb, (0, -half_w, Q_BLOCK - kw)


def _band_buckets(seq_len, half_w, dilation):
    kw, _, offsets = _window_geometry(seq_len, half_w)
    qi = np.arange(Q_BLOCK)[:, None]
    kj = np.arange(kw)[None, :]
    rel = np.stack([off + kj - qi for off in offsets])
    buckets = _t5_bucket_static(rel * dilation)
    return np.where(np.abs(rel) <= half_w, buckets, -1).astype(np.int32)


def _block_geometry(i, seq_len, half_w):
    kw, nb, _ = _window_geometry(seq_len, half_w)
    if isinstance(i, int):
        qs = i * Q_BLOCK
        return qs, min(max(qs - half_w, 0), seq_len - kw), 0 if i == 0 else (2 if i == nb - 1 else 1)
    qs = pl.multiple_of(i * Q_BLOCK, Q_BLOCK)
    ks = pl.multiple_of(jnp.clip(qs - half_w, 0, seq_len - kw), half_w)
    return qs, ks, jnp.where(i == 0, 0, jnp.where(i == nb - 1, 2, 1))


def _build_bias(bkt_ref, table_ref, bias_ref, slot, first_head, n_heads, buckets_used):
    for kind in range(bkt_ref.shape[0]):
        bkt = bkt_ref[kind]
        for h in range(n_heads):
            acc = jnp.full(bkt.shape, NEG_INF, F32)
            for bucket in buckets_used:
                acc = jnp.where(bkt == bucket,
                                table_ref[bucket * N_BIAS_HEADS + first_head + h] * LOG2_E, acc)
            bias_ref[slot, kind, h * Q_BLOCK:(h + 1) * Q_BLOCK, :] = acc


def _softmax_chains(items, *, load, store, bias, kw, sink=None, want_lse):
    ones = jnp.ones((kw, HEAD_DIM), BF16)
    scores = []
    for tag, qs, ks, kind in items:
        q = (load(0, tag, qs, Q_BLOCK) * QK_SCALE_LOG2).astype(BF16)
        s = lax.dot_general(q, load(1, tag, ks, kw).astype(BF16),
                            (((1,), (1,)), ((), ())), preferred_element_type=F32)
        scores.append(s + bias(tag, kind))
    for (tag, qs, ks, _), s in zip(items, scores):
        v_ext = jnp.concatenate([load(2, tag, ks, kw).astype(BF16), ones], axis=1)
        m = jnp.max(s, axis=-1, keepdims=True)
        if sink is not None:
            sk = sink(tag) * LOG2_E
            m = jnp.maximum(m, sk)
        p = jnp.exp2(s - m).astype(BF16)
        ov = jnp.dot(p, v_ext, preferred_element_type=F32)
        denom = ov[:, HEAD_DIM:]
        if sink is not None:
            denom = denom + jnp.exp2(sk - m)
        store(0, tag, qs, ov[:, :HEAD_DIM] / denom)
        if want_lse:
            store(1, tag, qs, m * (1.0 / LOG2_E) + jnp.log(denom))


def _attn_a_kernel(bkt_ref, table_ref, sink_ref, q_ref, k_ref, v_ref, wn_ref, o_ref, wn_out_ref,
                   bias_ref, *, seq_len, buckets_used):
    kv = pl.program_id(1)
    kw, nb, _ = _window_geometry(seq_len, A_HALF_WINDOW)
    wn_out_ref[...] = wn_ref[...].astype(BF16)

    @pl.when(pl.program_id(0) == 0)
    def _():
        _build_bias(bkt_ref, table_ref, bias_ref, kv, kv * A_GROUP, A_GROUP, buckets_used)

    def load(t, h, start, size):
        if t == 0:
            return q_ref[h, 0, pl.ds(start, size), :]
        return (k_ref, v_ref)[t - 1][0, 0, pl.ds(start, size), :]

    def store(t, h, start, value):
        o_ref[0, pl.ds(start, Q_BLOCK), h * HEAD_DIM:(h + 1) * HEAD_DIM] = value.astype(o_ref.dtype)

    def body(it, carry):
        items = [(h,) + _block_geometry(it * A_BLOCKS_PER_GROUP + di, seq_len, A_HALF_WINDOW)
                 for di in range(A_BLOCKS_PER_GROUP) for h in range(A_GROUP)]
        _softmax_chains(
            items, load=load, store=store, kw=kw, want_lse=False,
            bias=lambda h, kind: bias_ref[kv, kind, h * Q_BLOCK:(h + 1) * Q_BLOCK, :],
            sink=lambda h: jnp.full((Q_BLOCK, 1), sink_ref[kv * A_GROUP + h], F32))
        return carry

    assert nb % A_BLOCKS_PER_GROUP == 0
    lax.fori_loop(0, nb // A_BLOCKS_PER_GROUP, body, 0, unroll=ATTN_UNROLL)


def _buckets_used(bkt):
    return tuple(int(v) for v in np.unique(bkt) if v >= 0)


def _smem_spec():
    return pl.BlockSpec(memory_space=pltpu.SMEM)


def _attn_a(pb, table_flat, sink, w_next):
    _, b, s, _ = pb.shape
    wn_in, wn_out, wn_shape = _cast_rider(w_next, b * A_KV_HEADS, lambda i, j: i * A_KV_HEADS + j)
    gw = A_GROUP * HEAD_DIM
    kw, _, offsets = _window_geometry(s, A_HALF_WINDOW)
    bkt = _band_buckets(s, A_HALF_WINDOW, 1)
    kernel = functools.partial(_attn_a_kernel, seq_len=s, buckets_used=_buckets_used(bkt))
    return pl.pallas_call(
        kernel,
        grid=(b, A_KV_HEADS),
        in_specs=[
            pl.BlockSpec(bkt.shape, lambda i, j: (0, 0, 0)),
            _smem_spec(),
            _smem_spec(),
            pl.BlockSpec((A_GROUP, 1, s, HEAD_DIM), lambda i, j: (j, i, 0, 0)),
            pl.BlockSpec((1, 1, s, HEAD_DIM), lambda i, j: (PB_COL_KA // HEAD_DIM + j, i, 0, 0)),
            pl.BlockSpec((1, 1, s, HEAD_DIM), lambda i, j: (PB_COL_VA // HEAD_DIM + j, i, 0, 0)),
            wn_in,
        ],
        out_specs=[pl.BlockSpec((1, s, gw), lambda i, j: (i, 0, j)), wn_out],
        out_shape=[jax.ShapeDtypeStruct((b, s, A_Q_W), BF16), wn_shape],
        scratch_shapes=[pltpu.VMEM((A_KV_HEADS, len(offsets), A_GROUP * Q_BLOCK, kw), F32)],
        compiler_params=_params("arbitrary", "arbitrary"),
        name="attn_a",
    )(jnp.asarray(bkt), table_flat, sink, pb, pb, pb, w_next)


def _attn_b_kernel(*refs, seq, groups):
    ng = len(groups)
    it = iter(refs)
    bkt_refs = [next(it) for _ in range(ng)]
    table_ref = next(it)
    qkv_refs = [[next(it) for _ in range(3)] for _ in range(ng)]
    wn_ref, yb_ref, wn_out_ref = next(it), next(it), next(it)
    bias_refs = [next(it) for _ in range(ng)]
    o_acc, l_acc, stage_in, stage_out = next(it), next(it), next(it), next(it)
    head = pl.program_id(1)
    wn_out_ref[...] = wn_ref[...].astype(BF16)

    @pl.when(pl.program_id(0) == 0)
    def _():
        for g, grp in enumerate(groups):
            _build_bias(bkt_refs[g], table_ref, bias_refs[g], head, grp["head0"] + head, 1,
                        grp["buckets_used"])

    for g, grp in enumerate(groups):
        dil, sub, half, chain_group = grp["dil"], grp["sub"], grp["half"], grp["chain_group"]
        kw, nb, _ = _window_geometry(sub, half)
        stage = STAGE_STRIDE if dil > STAGE_STRIDE else 1
        hop2 = dil // stage
        accs = (o_acc, l_acc)

        def rows(residue, start, size, dil=dil, stage=stage, hop2=hop2):
            if dil == 1:
                return pl.ds(start, size)
            if stage > 1:
                return pl.ds(residue // stage + start * hop2, size, stride=hop2)
            return pl.ds(residue + start * dil, size, stride=dil)

        def load(t, residue, start, size, g=g, stage=stage, rows=rows):
            if stage > 1:
                return stage_in[t, residue % stage, rows(residue, start, size), :]
            return qkv_refs[g][t][0, 0, rows(residue, start, size), :]

        def store(t, residue, start, value, g=g, stage=stage, rows=rows, accs=accs):
            if stage > 1:
                stage_out[t, residue % stage, rows(residue, start, Q_BLOCK), :] = value
            else:
                accs[t][g, rows(residue, start, Q_BLOCK), :] = value

        def bias(residue, kind, g=g):
            return bias_refs[g][head, kind]

        run = functools.partial(_softmax_chains, load=load, store=store, bias=bias, kw=kw,
                                want_lse=True)
        if dil == 1:
            per_iter = min(chain_group, nb)
            assert nb % per_iter == 0

            def body(it_, carry, run=run, per_iter=per_iter, sub=sub, half=half):
                run([(0,) + _block_geometry(it_ * per_iter + di, sub, half) for di in range(per_iter)])
                return carry

            lax.fori_loop(0, nb // per_iter, body, 0)
            continue
        if stage > 1:
            for t in range(3):
                for r1 in range(stage):
                    stage_in[t, r1] = qkv_refs[g][t][0, 0, pl.ds(r1, seq // stage, stride=stage), :]
        work = [(r,) + _block_geometry(i, sub, half) for r in range(dil) for i in range(nb)]
        for c in range(0, len(work), chain_group):
            run(work[c:c + chain_group])
        if stage > 1:
            for t in range(2):
                for r1 in range(stage):
                    accs[t][g, pl.ds(r1, seq // stage, stride=stage), :] = stage_out[t, r1]

    def merge(c, carry):
        rws = pl.ds(pl.multiple_of(c * MERGE_ROWS, MERGE_ROWS), MERGE_ROWS)
        ls = [l_acc[g, rws, :] for g in range(ng)]
        mx = functools.reduce(jnp.maximum, ls)
        es = [jnp.exp(l - mx) for l in ls]
        tot = functools.reduce(lambda a, e: a + e, es)
        y = functools.reduce(lambda a, e: a + e, [e * o_acc[g, rws, :] for g, e in enumerate(es)])
        yb_ref[0, rws, :] = (y / tot).astype(yb_ref.dtype)
        return carry

    lax.fori_loop(0, seq // MERGE_ROWS, merge, 0)


def _attn_b(pb, table_flat, w_next):
    _, b, s, _ = pb.shape
    nh = B_HEADS_PER_GROUP
    wn_in, wn_out, wn_shape = _cast_rider(w_next, b * nh, lambda i, h: i * nh + h)
    groups, bkts, bias_scratch = [], [], []
    for gi, (window, dil) in enumerate(B_PATTERNS):
        sub, half = s // dil, window // (2 * dil)
        kw, _, offsets = _window_geometry(sub, half)
        bkt = _band_buckets(sub, half, dil)
        bkts.append(bkt)
        bias_scratch.append(pltpu.VMEM((nh, len(offsets), Q_BLOCK, kw), F32))
        groups.append(dict(dil=dil, sub=sub, half=half, chain_group=CHAIN_GROUP[gi],
                           head0=A_Q_HEADS + gi * nh, buckets_used=_buckets_used(bkt)))

    def qkv_spec(gi, part):
        base = (PB_COL_QB + part * B_W + gi * B_GROUP_W) // HEAD_DIM
        return pl.BlockSpec((1, 1, s, HEAD_DIM), lambda i, h: (base + h, i, 0, 0))

    ng = len(groups)
    return pl.pallas_call(
        functools.partial(_attn_b_kernel, seq=s, groups=groups),
        grid=(b, nh),
        in_specs=[pl.BlockSpec(bkt.shape, lambda i, h: (0, 0, 0)) for bkt in bkts] + [_smem_spec()]
        + [qkv_spec(gi, part) for gi in range(ng) for part in range(3)] + [wn_in],
        out_specs=[pl.BlockSpec((1, s, HEAD_DIM), lambda i, h: (i, 0, h)), wn_out],
        out_shape=[jax.ShapeDtypeStruct((b, s, B_GROUP_W), BF16), wn_shape],
        scratch_shapes=bias_scratch + [
            pltpu.VMEM((ng, s, HEAD_DIM), F32), pltpu.VMEM((ng, s, HEAD_DIM), F32),
            pltpu.VMEM((3, STAGE_STRIDE, s // STAGE_STRIDE, HEAD_DIM), F32),
            pltpu.VMEM((2, STAGE_STRIDE, s // STAGE_STRIDE, HEAD_DIM), F32)],
        compiler_params=_params("arbitrary", "arbitrary"),
        name="attn_b",
    )(*[jnp.asarray(bkt) for bkt in bkts], table_flat, *([pb] * (3 * ng)), w_next)


def _mix_kernel(ya_ref, yb_ref, ga_ref, gb_ref, x_ref, wa_ref, wb_ref, wo_ref, g_ref, x1_ref, hf_ref):
    rp = x_ref.shape[0] // MIX_ROW_PARTS
    for r in range(0, x_ref.shape[0], rp):
        rows = slice(r, r + rp)
        ta = jnp.dot(ya_ref[rows, :], wa_ref[...], preferred_element_type=F32)
        tb = jnp.dot(yb_ref[rows, :], wb_ref[...], preferred_element_type=F32)
        merged = (jax.nn.sigmoid(ga_ref[rows, :].astype(F32)) * ta
                  + jax.nn.sigmoid(gb_ref[rows, :].astype(F32)) * tb)
        x1 = x_ref[rows, :] + jnp.dot(merged.astype(BF16), wo_ref[...], preferred_element_type=F32)
        x1_ref[rows, :] = x1
        hf_ref[rows, :] = _rmsnorm(x1, g_ref[...]).astype(BF16)


def _resident(shape):
    return pl.BlockSpec(shape, lambda i: (0,) * len(shape), pipeline_mode=pl.Buffered(1))


def _mix(ya, yb, pa2d, x2d, wa, wb, wo, gain, *, tm=512):
    m, d = x2d.shape
    assert PA_COL_GA % d == 0 and PA_COL_GB % d == 0

    def rows(width, col_block=0):
        return pl.BlockSpec((tm, width), lambda i: (i, col_block))

    return pl.pallas_call(
        _mix_kernel,
        grid=(m // tm,),
        in_specs=[rows(A_Q_W), rows(B_GROUP_W),
                  rows(d, PA_COL_GA // d), rows(d, PA_COL_GB // d), rows(d),
                  _resident(wa.shape), _resident(wb.shape), _resident(wo.shape), _resident(gain.shape)],
        out_specs=[rows(d), rows(d)],
        out_shape=[jax.ShapeDtypeStruct((m, d), F32), jax.ShapeDtypeStruct((m, d), BF16)],
        compiler_params=_params("parallel"),
        name="mix",
    )(ya, yb, pa2d, pa2d, x2d, wa, wb, wo, gain)


def _ffn_up_kernel(hf_ref, wg_ref, wu_ref, cw_ref, cb_ref, wn_ref, act_ref, wn_out_ref):
    wn_out_ref[...] = wn_ref[...].astype(BF16)
    wg = wg_ref[...].astype(BF16)
    wu = wu_ref[...].astype(BF16)
    rp = hf_ref.shape[0] // FFN_ROW_PARTS
    parts = [(jnp.dot(hf_ref[r:r + rp, :], wg, preferred_element_type=F32),
              jnp.dot(hf_ref[r:r + rp, :], wu, preferred_element_type=F32))
             for r in range(0, hf_ref.shape[0], rp)]
    g = jnp.concatenate([p[0] for p in parts], axis=0)
    u = jnp.concatenate([p[1] for p in parts], axis=0)
    s = g.shape[0]
    row = lax.broadcasted_iota(jnp.int32, g.shape, 0)
    prev = jnp.where(row == 0, 0.0, pltpu.roll(g, 1, 0))
    nxt = jnp.where(row == s - 1, 0.0, pltpu.roll(g, s - 1, 0))
    cw = cw_ref[...]
    conv = prev * cw[0:1] + g * cw[1:2] + nxt * cw[2:3] + cb_ref[...]
    act_ref[...] = (jax.nn.gelu(conv) * u).astype(act_ref.dtype)


def _ffn_up(hf, wg, wu, cw, cb, w_next, *, seq, tf=512):
    m, d = hf.shape
    f = wg.shape[1]
    nj = f // tf
    wn_in, wn_out, wn_shape = _cast_rider(w_next, (m // seq) * nj, lambda i, j: i * nj + j)
    return pl.pallas_call(
        _ffn_up_kernel,
        grid=(m // seq, f // tf),
        in_specs=[
            pl.BlockSpec((seq, d), lambda i, j: (i, 0)),
            pl.BlockSpec((d, tf), lambda i, j: (0, j)),
            pl.BlockSpec((d, tf), lambda i, j: (0, j)),
            pl.BlockSpec((cw.shape[0], tf), lambda i, j: (0, j)),
            pl.BlockSpec((1, tf), lambda i, j: (0, j)),
            wn_in,
        ],
        out_specs=[pl.BlockSpec((seq, tf), lambda i, j: (i, j)), wn_out],
        out_shape=[jax.ShapeDtypeStruct((m, f), BF16), wn_shape],
        compiler_params=_params("parallel", "arbitrary"),
        name="ffn_up",
    )(hf, wg, wu, cw, cb, w_next)


def _ffn_down_kernel(act_ref, w_ref, x_ref, wn_ref, o_ref, wn_out_ref):
    wn_out_ref[...] = wn_ref[...].astype(BF16)
    o_ref[...] = x_ref[...] + jnp.dot(act_ref[...], w_ref[...], preferred_element_type=F32)


def _ffn_down(act, wd, x1, w_next, *, tm=1024, tn=1024):
    m, f = act.shape
    n = wd.shape[1]
    ni = m // tm
    wn_in, wn_out, wn_shape = _cast_rider(w_next, (n // tn) * ni, lambda j, i: j * ni + i)
    return pl.pallas_call(
        _ffn_down_kernel,
        grid=(n // tn, m // tm),
        in_specs=[
            pl.BlockSpec((tm, f), lambda j, i: (i, 0)),
            pl.BlockSpec((f, tn), lambda j, i: (0, j), pipeline_mode=pl.Buffered(1)),
            pl.BlockSpec((tm, tn), lambda j, i: (i, j)),
            wn_in,
        ],
        out_specs=[pl.BlockSpec((tm, tn), lambda j, i: (i, j)), wn_out],
        out_shape=[jax.ShapeDtypeStruct((m, n), F32), wn_shape],
        compiler_params=_params("parallel", "parallel"),
        name="ffn_down",
    )(act, wd, x1, w_next)


def _ple_kernel(x_ref, p_ref, gp_ref, wg_ref, wp_ref, gf_ref, o_ref, *, final):
    rp = x_ref.shape[0] // PLE_ROW_PARTS
    for r in range(0, x_ref.shape[0], rp):
        rows = slice(r, r + rp)
        x = x_ref[rows, :]
        hp = _rmsnorm(x, gp_ref[...]).astype(BF16)
        gate = jax.nn.sigmoid(jnp.dot(hp, wg_ref[...], preferred_element_type=F32))
        emb = jnp.dot(p_ref[rows, :].astype(BF16), wp_ref[...], preferred_element_type=F32)
        y = x + gate * emb
        o_ref[rows, :] = _rmsnorm(y, gf_ref[...]) if final else y


def _ple(x2, p2d, gain_p, wpg, wpp, gain_f, *, final, tm=512):
    m, d = x2.shape
    pd = p2d.shape[1]
    return pl.pallas_call(
        functools.partial(_ple_kernel, final=final),
        grid=(m // tm,),
        in_specs=[
            pl.BlockSpec((tm, d), lambda i: (i, 0)),
            pl.BlockSpec((tm, pd), lambda i: (i, 0)),
            _resident(gain_p.shape), _resident(wpg.shape), _resident(wpp.shape),
            _resident(gain_f.shape),
        ],
        out_specs=pl.BlockSpec((tm, d), lambda i: (i, 0)),
        out_shape=jax.ShapeDtypeStruct((m, d), F32),
        compiler_params=_params("parallel"),
        name="ple",
    )(x2, p2d, gain_p, wpg, wpp, gain_f)


def kernel(x, p, rel_bias_table, attn_norm, w_in, sink_a, w_branch_a, w_branch_b, w_out,
           ffn_norm, w_ffn_gate, w_ffn_up, conv_w, conv_b, w_ffn_down,
           ple_norm, w_ple_gate, w_ple_proj, final_norm):
    b, s, d = x.shape
    depth = w_in.shape[0]
    assert d == D_MODEL and w_in.shape[2] == IN_PROJ_W and s % (Q_BLOCK * B_PATTERNS[-1][1]) == 0
    assert rel_bias_table.shape == (N_BUCKETS, N_BIAS_HEADS)
    m = b * s
    x2d = x.reshape(m, d)
    table_flat = rel_bias_table.reshape(-1)
    for i in range(depth):
        pb2d, pa2d = _inproj(x2d, attn_norm[i][None], w_in[i])
        pb = pb2d.reshape(PB_W // HEAD_DIM, b, s, HEAD_DIM)
        ya, wa = _attn_a(pb, table_flat, sink_a[i], w_branch_a[i])
        yb, wo = _attn_b(pb, table_flat, w_out[i])
        x1, hf = _mix(ya.reshape(m, A_Q_W), yb.reshape(m, B_GROUP_W), pa2d, x2d, wa,
                      w_branch_b[i].astype(BF16), wo, ffn_norm[i][None])
        act, wd = _ffn_up(hf, w_ffn_gate[i], w_ffn_up[i], conv_w[i], conv_b[i][None], w_ffn_down[i],
                          seq=s)
        x2, wpg = _ffn_down(act, wd, x1, w_ple_gate[i])
        x2d = _ple(x2, p[i].reshape(m, -1), ple_norm[i][None], wpg,
                   w_ple_proj[i].astype(BF16), final_norm[None], final=i == depth - 1)
    return x2d.reshape(b, s, d)
```

```python
import functools
import math

import jax
import jax.numpy as jnp
import numpy as np
from jax import lax
from jax.experimental import pallas as pl
from jax.experimental.pallas import tpu as pltpu

D_MODEL = 2048
HEAD_DIM = 128
A_Q_HEADS = 8
A_KV_HEADS = 2
A_GROUP = A_Q_HEADS // A_KV_HEADS
A_HALF_WINDOW = 128
B_PATTERNS = ((128, 1), (512, 4), (2048, 16))
B_HEADS_PER_GROUP = 4
N_BUCKETS = 32
MAX_DISTANCE = 1024
N_BIAS_HEADS = A_Q_HEADS + len(B_PATTERNS) * B_HEADS_PER_GROUP
A_Q_W = A_Q_HEADS * HEAD_DIM
A_KV_W = A_KV_HEADS * HEAD_DIM
B_GROUP_W = B_HEADS_PER_GROUP * HEAD_DIM
B_W = len(B_PATTERNS) * B_GROUP_W
IN_PROJ_W = A_Q_W + 2 * A_KV_W + 3 * B_W + 2 * D_MODEL
RMS_EPS = 1e-6
NEG_INF = -1e30
LOG2_E = math.log2(math.e)
QK_SCALE_LOG2 = HEAD_DIM ** -0.5 * LOG2_E

PB_W = A_Q_W + 2 * A_KV_W + 3 * B_W
PB_COL_KA = A_Q_W
PB_COL_VA = PB_COL_KA + A_KV_W
PB_COL_QB = PB_COL_VA + A_KV_W
PA_W = 2 * D_MODEL
PA_COL_GA = 0
PA_COL_GB = D_MODEL

Q_BLOCK = 128
ATTN_UNROLL = 2
STAGE_STRIDE = 4
CHAIN_GROUP = (16, 1, 16)
A_BLOCKS_PER_GROUP = 2
MERGE_ROWS = 256
FFN_ROW_PARTS = 2
MIX_ROW_PARTS = 2
PLE_ROW_PARTS = 1

VMEM_LIMIT_BYTES = 56 * 1024 * 1024

BF16 = jnp.bfloat16
F32 = jnp.float32


def _params(*semantics, vmem_limit_bytes=VMEM_LIMIT_BYTES):
    return pltpu.CompilerParams(dimension_semantics=semantics, vmem_limit_bytes=vmem_limit_bytes)


def _rmsnorm(x, g):
    y = x * lax.rsqrt(jnp.mean(x * x, axis=-1, keepdims=True) + RMS_EPS)
    return y * g


def _cast_rider(w, n_steps, step_of):
    rows = w.shape[0] // n_steps
    assert rows * n_steps == w.shape[0] and rows % 16 == 0
    spec = pl.BlockSpec((rows, w.shape[1]), lambda *ids: (step_of(*ids), 0))
    return spec, spec, jax.ShapeDtypeStruct(w.shape, BF16)


def _inproj_kernel(x_ref, g_ref, w_ref, pb_ref, pa_ref, wbf_ref, h_ref, wbuf, rsem, wsem,
                   *, row_chunk, n_b_tiles):
    i, j = pl.program_id(0), pl.program_id(1)
    n_rows, n_steps = pl.num_programs(0), pl.num_programs(1)
    slot = j % 2
    other = 1 - slot
    first = i == 0

    def write_tile(t, s):
        return pltpu.make_async_copy(wbuf.at[s], wbf_ref.at[t], wsem.at[s])

    def read_tile(t, s):
        return pltpu.make_async_copy(wbf_ref.at[t], wbuf.at[s], rsem.at[s])

    @pl.when(first & (j >= 2))
    def _():
        write_tile(j - 2, slot).wait()

    @pl.when(first)
    def _():
        wbuf[slot] = w_ref[...].astype(BF16)
        write_tile(j, slot).start()

    @pl.when(jnp.logical_not(first))
    def _():
        read_tile(j, slot).wait()

    @pl.when(first & (j == n_steps - 1))
    def _():
        write_tile(n_steps - 2, other).wait()
        read_tile(0, other).start()

    @pl.when((i == 1) & (j == 0))
    def _():
        write_tile(n_steps - 1, other).wait()

    @pl.when(jnp.logical_not(first) & jnp.logical_not((i == n_rows - 1) & (j == n_steps - 1)))
    def _():
        read_tile((j + 1) % n_steps, other).start()

    def store_slabs(rows, res):
        for s in range(pb_ref.shape[0]):
            pb_ref[s, rows, :] = res[:, s * HEAD_DIM:(s + 1) * HEAD_DIM]

    @pl.when(j == 0)
    def _():
        for r in range(0, x_ref.shape[0], row_chunk):
            rows = slice(r, r + row_chunk)
            h = _rmsnorm(x_ref[rows, :], g_ref[...]).astype(BF16)
            h_ref[rows, :] = h
            store_slabs(rows, jnp.dot(h, wbuf[slot], preferred_element_type=F32))

    @pl.when((j > 0) & (j < n_b_tiles))
    def _():
        store_slabs(slice(None), jnp.dot(h_ref[...], wbuf[slot], preferred_element_type=F32))

    @pl.when(j >= n_b_tiles)
    def _():
        pa_ref[...] = jnp.dot(h_ref[...], wbuf[slot], preferred_element_type=F32).astype(BF16)


def _inproj(x2d, gain, w, *, tm=1024, tn=1024):
    m, k = x2d.shape
    assert PB_W % tn == 0 and PA_W % tn == 0 and w.shape[1] == PB_W + PA_W
    nb_t = PB_W // tn
    n_steps = w.shape[1] // tn
    assert n_steps % 2 == 0 and n_steps >= 4 and m // tm >= 2

    pb, pa, _ = pl.pallas_call(
        functools.partial(_inproj_kernel, row_chunk=256, n_b_tiles=nb_t),
        grid=(m // tm, n_steps),
        in_specs=[
            pl.BlockSpec((tm, k), lambda i, j: (i, 0)),
            pl.BlockSpec((1, k), lambda i, j: (0, 0)),
            pl.BlockSpec((k, tn), lambda i, j: (0, jnp.where(i == 0, j, n_steps - 1))),
        ],
        out_specs=[
            pl.BlockSpec((tn // HEAD_DIM, tm, HEAD_DIM), lambda i, j: (jnp.minimum(j, nb_t - 1), i, 0)),
            pl.BlockSpec((tm, tn), lambda i, j: (i, jnp.maximum(j - nb_t, 0))),
            pl.BlockSpec(memory_space=pl.ANY),
        ],
        out_shape=[jax.ShapeDtypeStruct((PB_W // HEAD_DIM, m, HEAD_DIM), F32),
                   jax.ShapeDtypeStruct((m, PA_W), BF16),
                   jax.ShapeDtypeStruct((n_steps, k, tn), BF16)],
        scratch_shapes=[pltpu.VMEM((tm, k), BF16), pltpu.VMEM((2, k, tn), BF16),
                        pltpu.SemaphoreType.DMA((2,)), pltpu.SemaphoreType.DMA((2,))],
        compiler_params=_params("arbitrary", "arbitrary",
                                vmem_limit_bytes=2 * (tm * k * 4 + k * tn * 4 + tm * tn * 4
                                                      + tm * tn * 2 + k * tn * 2)
                                + tm * k * 2 + tm * tn * 4 + 1024 * 1024),
        name="inproj",
    )(x2d, gain, w)
    return pb, pa


def _t5_bucket_static(rel):
    half = N_BUCKETS // 2
    max_exact = half // 2
    n = np.abs(rel)
    side = np.where(rel > 0, half, 0)
    nf = np.maximum(n, 1).astype(np.float32)
    large = max_exact + (np.log(nf / max_exact) / math.log(MAX_DISTANCE / max_exact)
                         * (half - max_exact)).astype(np.int32)
    large = np.minimum(large, half - 1)
    return side + np.where(n < max_exact, n, large)


def _window_geometry(seq_len, half_w):
    nb = seq_len // Q_BLOCK
    if nb == 1:
        return seq_len, nb, (0,)
    kw = Q_BLOCK + 2 * half_w
    assert half_w <= Q_BLOCK and kw <= seq_len
    return kw, nb, (0, -half_w, Q_BLOCK - kw)


def _band_buckets(seq_len, half_w, dilation):
    kw, _, offsets = _window_geometry(seq_len, half_w)
    qi = np.arange(Q_BLOCK)[:, None]
    kj = np.arange(kw)[None, :]
    rel = np.stack([off + kj - qi for off in offsets])
    buckets = _t5_bucket_static(rel * dilation)
    return np.where(np.abs(rel) <= half_w, buckets, -1).astype(np.int32)


def _block_geometry(i, seq_len, half_w):
    kw, nb, _ = _window_geometry(seq_len, half_w)
    if isinstance(i, int):
        qs = i * Q_BLOCK
        return qs, min(max(qs - half_w, 0), seq_len - kw), 0 if i == 0 else (2 if i == nb - 1 else 1)
    qs = pl.multiple_of(i * Q_BLOCK, Q_BLOCK)
    ks = pl.multiple_of(jnp.clip(qs - half_w, 0, seq_len - kw), half_w)
    return qs, ks, jnp.where(i == 0, 0, jnp.where(i == nb - 1, 2, 1))


def _build_bias(bkt_ref, table_ref, bias_ref, slot, first_head, n_heads, buckets_used):
    for kind in range(bkt_ref.shape[0]):
        bkt = bkt_ref[kind]
        for h in range(n_heads):
            acc = jnp.full(bkt.shape, NEG_INF, F32)
            for bucket in buckets_used:
                acc = jnp.where(bkt == bucket,
                                table_ref[bucket * N_BIAS_HEADS + first_head + h] * LOG2_E, acc)
            bias_ref[slot, kind, h * Q_BLOCK:(h + 1) * Q_BLOCK, :] = acc


def _softmax_chains(items, *, load, store, bias, kw, sink=None, want_lse):
    ones = jnp.ones((kw, HEAD_DIM), BF16)
    scores = []
    for tag, qs, ks, kind in items:
        q = (load(0, tag, qs, Q_BLOCK) * QK_SCALE_LOG2).astype(BF16)
        s = lax.dot_general(q, load(1, tag, ks, kw).astype(BF16),
                            (((1,), (1,)), ((), ())), preferred_element_type=F32)
        scores.append(s + bias(tag, kind))
    for (tag, qs, ks, _), s in zip(items, scores):
        v_ext = jnp.concatenate([load(2, tag, ks, kw).astype(BF16), ones], axis=1)
        m = jnp.max(s, axis=-1, keepdims=True)
        if sink is not None:
            sk = sink(tag) * LOG2_E
            m = jnp.maximum(m, sk)
        p = jnp.exp2(s - m).astype(BF16)
        ov = jnp.dot(p, v_ext, preferred_element_type=F32)
        denom = ov[:, HEAD_DIM:]
        if sink is not None:
            denom = denom + jnp.exp2(sk - m)
        store(0, tag, qs, ov[:, :HEAD_DIM] / denom)
        if want_lse:
            store(1, tag, qs, m * (1.0 / LOG2_E) + jnp.log(denom))


def _attn_a_kernel(bkt_ref, table_ref, sink_ref, q_ref, k_ref, v_ref, wn_ref, o_ref, wn_out_ref,
                   bias_ref, *, seq_len, buckets_used):
    kv = pl.program_id(1)
    kw, nb, _ = _window_geometry(seq_len, A_HALF_WINDOW)
    wn_out_ref[...] = wn_ref[...].astype(BF16)

    @pl.when(pl.program_id(0) == 0)
    def _():
        _build_bias(bkt_ref, table_ref, bias_ref, kv, kv * A_GROUP, A_GROUP, buckets_used)

    def load(t, h, start, size):
        if t == 0:
            return q_ref[h, 0, pl.ds(start, size), :]
        return (k_ref, v_ref)[t - 1][0, 0, pl.ds(start, size), :]

    def store(t, h, start, value):
        o_ref[0, pl.ds(start, Q_BLOCK), h * HEAD_DIM:(h + 1) * HEAD_DIM] = value.astype(o_ref.dtype)

    def body(it, carry):
        items = [(h,) + _block_geometry(it * A_BLOCKS_PER_GROUP + di, seq_len, A_HALF_WINDOW)
                 for di in range(A_BLOCKS_PER_GROUP) for h in range(A_GROUP)]
        _softmax_chains(
            items, load=load, store=store, kw=kw, want_lse=False,
            bias=lambda h, kind: bias_ref[kv, kind, h * Q_BLOCK:(h + 1) * Q_BLOCK, :],
            sink=lambda h: jnp.full((Q_BLOCK, 1), sink_ref[kv * A_GROUP + h], F32))
        return carry

    assert nb % A_BLOCKS_PER_GROUP == 0
    lax.fori_loop(0, nb // A_BLOCKS_PER_GROUP, body, 0, unroll=ATTN_UNROLL)


def _buckets_used(bkt):
    return tuple(int(v) for v in np.unique(bkt) if v >= 0)


def _smem_spec():
    return pl.BlockSpec(memory_space=pltpu.SMEM)


def _attn_a(pb, table_flat, sink, w_next):
    _, b, s, _ = pb.shape
    wn_in, wn_out, wn_shape = _cast_rider(w_next, b * A_KV_HEADS, lambda i, j: i * A_KV_HEADS + j)
    gw = A_GROUP * HEAD_DIM
    kw, _, offsets = _window_geometry(s, A_HALF_WINDOW)
    bkt = _band_buckets(s, A_HALF_WINDOW, 1)
    kernel = functools.partial(_attn_a_kernel, seq_len=s, buckets_used=_buckets_used(bkt))
    return pl.pallas_call(
        kernel,
        grid=(b, A_KV_HEADS),
        in_specs=[
            pl.BlockSpec(bkt.shape, lambda i, j: (0, 0, 0)),
            _smem_spec(),
            _smem_spec(),
            pl.BlockSpec((A_GROUP, 1, s, HEAD_DIM), lambda i, j: (j, i, 0, 0)),
            pl.BlockSpec((1, 1, s, HEAD_DIM), lambda i, j: (PB_COL_KA // HEAD_DIM + j, i, 0, 0)),
            pl.BlockSpec((1, 1, s, HEAD_DIM), lambda i, j: (PB_COL_VA // HEAD_DIM + j, i, 0, 0)),
            wn_in,
        ],
        out_specs=[pl.BlockSpec((1, s, gw), lambda i, j: (i, 0, j)), wn_out],
        out_shape=[jax.ShapeDtypeStruct((b, s, A_Q_W), BF16), wn_shape],
        scratch_shapes=[pltpu.VMEM((A_KV_HEADS, len(offsets), A_GROUP * Q_BLOCK, kw), F32)],
        compiler_params=_params("arbitrary", "arbitrary"),
        name="attn_a",
    )(jnp.asarray(bkt), table_flat, sink, pb, pb, pb, w_next)


def _attn_b_kernel(*refs, seq, groups):
    ng = len(groups)
    it = iter(refs)
    bkt_refs = [next(it) for _ in range(ng)]
    table_ref = next(it)
    qkv_refs = [[next(it) for _ in range(3)] for _ in range(ng)]
    wn_ref, yb_ref, wn_out_ref = next(it), next(it), next(it)
    bias_refs = [next(it) for _ in range(ng)]
    o_acc, l_acc, stage_in, stage_out = next(it), next(it), next(it), next(it)
    head = pl.program_id(1)
    wn_out_ref[...] = wn_ref[...].astype(BF16)

    @pl.when(pl.program_id(0) == 0)
    def _():
        for g, grp in enumerate(groups):
            _build_bias(bkt_refs[g], table_ref, bias_refs[g], head, grp["head0"] + head, 1,
                        grp["buckets_used"])

    for g, grp in enumerate(groups):
        dil, sub, half, chain_group = grp["dil"], grp["sub"], grp["half"], grp["chain_group"]
        kw, nb, _ = _window_geometry(sub, half)
        stage = STAGE_STRIDE if dil > STAGE_STRIDE else 1
        hop2 = dil // stage
        accs = (o_acc, l_acc)

        def rows(residue, start, size, dil=dil, stage=stage, hop2=hop2):
            if dil == 1:
                return pl.ds(start, size)
            if stage > 1:
                return pl.ds(residue // stage + start * hop2, size, stride=hop2)
            return pl.ds(residue + start * dil, size, stride=dil)

        def load(t, residue, start, size, g=g, stage=stage, rows=rows):
            if stage > 1:
                return stage_in[t, residue % stage, rows(residue, start, size), :]
            return qkv_refs[g][t][0, 0, rows(residue, start, size), :]

        def store(t, residue, start, value, g=g, stage=stage, rows=rows, accs=accs):
            if stage > 1:
                stage_out[t, residue % stage, rows(residue, start, Q_BLOCK), :] = value
            else:
                accs[t][g, rows(residue, start, Q_BLOCK), :] = value

        def bias(residue, kind, g=g):
            return bias_refs[g][head, kind]

        run = functools.partial(_softmax_chains, load=load, store=store, bias=bias, kw=kw,
                                want_lse=True)
        if dil == 1:
            per_iter = min(chain_group, nb)
            assert nb % per_iter == 0

            def body(it_, carry, run=run, per_iter=per_iter, sub=sub, half=half):
                run([(0,) + _block_geometry(it_ * per_iter + di, sub, half) for di in range(per_iter)])
                return carry

            lax.fori_loop(0, nb // per_iter, body, 0)
            continue
        if stage > 1:
            for t in range(3):
                for r1 in range(stage):
                    stage_in[t, r1] = qkv_refs[g][t][0, 0, pl.ds(r1, seq // stage, stride=stage), :]
        work = [(r,) + _block_geometry(i, sub, half) for r in range(dil) for i in range(nb)]
        for c in range(0, len(work), chain_group):
            run(work[c:c + chain_group])
        if stage > 1:
            for t in range(2):
                for r1 in range(stage):
                    accs[t][g, pl.ds(r1, seq // stage, stride=stage), :] = stage_out[t, r1]

    def merge(c, carry):
        rws = pl.ds(pl.multiple_of(c * MERGE_ROWS, MERGE_ROWS), MERGE_ROWS)
        ls = [l_acc[g, rws, :] for g in range(ng)]
        mx = functools.reduce(jnp.maximum, ls)
        es = [jnp.exp(l - mx) for l in ls]
        tot = functools.reduce(lambda a, e: a + e, es)
        y = functools.reduce(lambda a, e: a + e, [e * o_acc[g, rws, :] for g, e in enumerate(es)])
        yb_ref[0, rws, :] = (y / tot).astype(yb_ref.dtype)
        return carry

    lax.fori_loop(0, seq // MERGE_ROWS, merge, 0)


def _attn_b(pb, table_flat, w_next):
    _, b, s, _ = pb.shape
    nh = B_HEADS_PER_GROUP
    wn_in, wn_out, wn_shape = _cast_rider(w_next, b * nh, lambda i, h: i * nh + h)
    groups, bkts, bias_scratch = [], [], []
    for gi, (window, dil) in enumerate(B_PATTERNS):
        sub, half = s // dil, window // (2 * dil)
        kw, _, offsets = _window_geometry(sub, half)
        bkt = _band_buckets(sub, half, dil)
        bkts.append(bkt)
        bias_scratch.append(pltpu.VMEM((nh, len(offsets), Q_BLOCK, kw), F32))
        groups.append(dict(dil=dil, sub=sub, half=half, chain_group=CHAIN_GROUP[gi],
                           head0=A_Q_HEADS + gi * nh, buckets_used=_buckets_used(bkt)))

    def qkv_spec(gi, part):
        base = (PB_COL_QB + part * B_W + gi * B_GROUP_W) // HEAD_DIM
        return pl.BlockSpec((1, 1, s, HEAD_DIM), lambda i, h: (base + h, i, 0, 0))

    ng = len(groups)
    return pl.pallas_call(
        functools.partial(_attn_b_kernel, seq=s, groups=groups),
        grid=(b, nh),
        in_specs=[pl.BlockSpec(bkt.shape, lambda i, h: (0, 0, 0)) for bkt in bkts] + [_smem_spec()]
        + [qkv_spec(gi, part) for gi in range(ng) for part in range(3)] + [wn_in],
        out_specs=[pl.BlockSpec((1, s, HEAD_DIM), lambda i, h: (i, 0, h)), wn_out],
        out_shape=[jax.ShapeDtypeStruct((b, s, B_GROUP_W), BF16), wn_shape],
        scratch_shapes=bias_scratch + [
            pltpu.VMEM((ng, s, HEAD_DIM), F32), pltpu.VMEM((ng, s, HEAD_DIM), F32),
            pltpu.VMEM((3, STAGE_STRIDE, s // STAGE_STRIDE, HEAD_DIM), F32),
            pltpu.VMEM((2, STAGE_STRIDE, s // STAGE_STRIDE, HEAD_DIM), F32)],
        compiler_params=_params("arbitrary", "arbitrary"),
        name="attn_b",
    )(*[jnp.asarray(bkt) for bkt in bkts], table_flat, *([pb] * (3 * ng)), w_next)


def _mix_kernel(ya_ref, yb_ref, ga_ref, gb_ref, x_ref, wa_ref, wb_ref, wo_ref, g_ref, x1_ref, hf_ref):
    rp = x_ref.shape[0] // MIX_ROW_PARTS
    for r in range(0, x_ref.shape[0], rp):
        rows = slice(r, r + rp)
        ta = jnp.dot(ya_ref[rows, :], wa_ref[...], preferred_element_type=F32)
        tb = jnp.dot(yb_ref[rows, :], wb_ref[...], preferred_element_type=F32)
        merged = (jax.nn.sigmoid(ga_ref[rows, :].astype(F32)) * ta
                  + jax.nn.sigmoid(gb_ref[rows, :].astype(F32)) * tb)
        x1 = x_ref[rows, :] + jnp.dot(merged.astype(BF16), wo_ref[...], preferred_element_type=F32)
        x1_ref[rows, :] = x1
        hf_ref[rows, :] = _rmsnorm(x1, g_ref[...]).astype(BF16)


def _resident(shape):
    return pl.BlockSpec(shape, lambda i: (0,) * len(shape), pipeline_mode=pl.Buffered(1))


def _mix(ya, yb, pa2d, x2d, wa, wb, wo, gain, *, tm=512):
    m, d = x2d.shape
    assert PA_COL_GA % d == 0 and PA_COL_GB % d == 0

    def rows(width, col_block=0):
        return pl.BlockSpec((tm, width), lambda i: (i, col_block))

    return pl.pallas_call(
        _mix_kernel,
        grid=(m // tm,),
        in_specs=[rows(A_Q_W), rows(B_GROUP_W),
                  rows(d, PA_COL_GA // d), rows(d, PA_COL_GB // d), rows(d),
                  _resident(wa.shape), _resident(wb.shape), _resident(wo.shape), _resident(gain.shape)],
        out_specs=[rows(d), rows(d)],
        out_shape=[jax.ShapeDtypeStruct((m, d), F32), jax.ShapeDtypeStruct((m, d), BF16)],
        compiler_params=_params("parallel"),
        name="mix",
    )(ya, yb, pa2d, pa2d, x2d, wa, wb, wo, gain)


def _ffn_up_kernel(hf_ref, wg_ref, wu_ref, cw_ref, cb_ref, wn_ref, act_ref, wn_out_ref):
    wn_out_ref[...] = wn_ref[...].astype(BF16)
    wg = wg_ref[...].astype(BF16)
    wu = wu_ref[...].astype(BF16)
    rp = hf_ref.shape[0] // FFN_ROW_PARTS
    parts = [(jnp.dot(hf_ref[r:r + rp, :], wg, preferred_element_type=F32),
              jnp.dot(hf_ref[r:r + rp, :], wu, preferred_element_type=F32))
             for r in range(0, hf_ref.shape[0], rp)]
    g = jnp.concatenate([p[0] for p in parts], axis=0)
    u = jnp.concatenate([p[1] for p in parts], axis=0)
    s = g.shape[0]
    row = lax.broadcasted_iota(jnp.int32, g.shape, 0)
    prev = jnp.where(row == 0, 0.0, pltpu.roll(g, 1, 0))
    nxt = jnp.where(row == s - 1, 0.0, pltpu.roll(g, s - 1, 0))
    cw = cw_ref[...]
    conv = prev * cw[0:1] + g * cw[1:2] + nxt * cw[2:3] + cb_ref[...]
    act_ref[...] = (jax.nn.gelu(conv) * u).astype(act_ref.dtype)


def _ffn_up(hf, wg, wu, cw, cb, w_next, *, seq, tf=512):
    m, d = hf.shape
    f = wg.shape[1]
    nj = f // tf
    wn_in, wn_out, wn_shape = _cast_rider(w_next, (m // seq) * nj, lambda i, j: i * nj + j)
    return pl.pallas_call(
        _ffn_up_kernel,
        grid=(m // seq, f // tf),
        in_specs=[
            pl.BlockSpec((seq, d), lambda i, j: (i, 0)),
            pl.BlockSpec((d, tf), lambda i, j: (0, j)),
            pl.BlockSpec((d, tf), lambda i, j: (0, j)),
            pl.BlockSpec((cw.shape[0], tf), lambda i, j: (0, j)),
            pl.BlockSpec((1, tf), lambda i, j: (0, j)),
            wn_in,
        ],
        out_specs=[pl.BlockSpec((seq, tf), lambda i, j: (i, j)), wn_out],
        out_shape=[jax.ShapeDtypeStruct((m, f), BF16), wn_shape],
        compiler_params=_params("parallel", "arbitrary"),
        name="ffn_up",
    )(hf, wg, wu, cw, cb, w_next)


def _ffn_down_kernel(act_ref, w_ref, x_ref, wn_ref, o_ref, wn_out_ref):
    wn_out_ref[...] = wn_ref[...].astype(BF16)
    o_ref[...] = x_ref[...] + jnp.dot(act_ref[...], w_ref[...], preferred_element_type=F32)


def _ffn_down(act, wd, x1, w_next, *, tm=512, tn=1024):
    m, f = act.shape
    n = wd.shape[1]
    ni = m // tm
    wn_in, wn_out, wn_shape = _cast_rider(w_next, (n // tn) * ni, lambda j, i: j * ni + i)
    return pl.pallas_call(
        _ffn_down_kernel,
        grid=(n // tn, m // tm),
        in_specs=[
            pl.BlockSpec((tm, f), lambda j, i: (i, 0)),
            pl.BlockSpec((f, tn), lambda j, i: (0, j)),
            pl.BlockSpec((tm, tn), lambda j, i: (i, j)),
            wn_in,
        ],
        out_specs=[pl.BlockSpec((tm, tn), lambda j, i: (i, j)), wn_out],
        out_shape=[jax.ShapeDtypeStruct((m, n), F32), wn_shape],
        compiler_params=_params("parallel", "parallel"),
        name="ffn_down",
    )(act, wd, x1, w_next)


def _ple_kernel(x_ref, p_ref, gp_ref, wg_ref, wp_ref, gf_ref, o_ref, *, final):
    rp = x_ref.shape[0] // PLE_ROW_PARTS
    for r in range(0, x_ref.shape[0], rp):
        rows = slice(r, r + rp)
        x = x_ref[rows, :]
        hp = _rmsnorm(x, gp_ref[...]).astype(BF16)
        gate = jax.nn.sigmoid(jnp.dot(hp, wg_ref[...], preferred_element_type=F32))
        emb = jnp.dot(p_ref[rows, :].astype(BF16), wp_ref[...], preferred_element_type=F32)
        y = x + gate * emb
        o_ref[rows, :] = _rmsnorm(y, gf_ref[...]) if final else y


def _ple(x2, p2d, gain_p, wpg, wpp, gain_f, *, final, tm=512):
    m, d = x2.shape
    pd = p2d.shape[1]

    def whole(shape):
        return pl.BlockSpec(shape, lambda i: (0,) * len(shape))

    def outer(*hbm_refs):
        pltpu.emit_pipeline(
            functools.partial(_ple_kernel, final=final),
            grid=(m // tm,),
            in_specs=[
                pl.BlockSpec((tm, d), lambda i: (i, 0)),
                pl.BlockSpec((tm, pd), lambda i: (i, 0)),
                whole(gain_p.shape), whole(wpg.shape), whole(wpp.shape), whole(gain_f.shape),
            ],
            out_specs=[pl.BlockSpec((tm, d), lambda i: (i, 0))],
        )(*hbm_refs)

    return pl.pallas_call(
        outer,
        in_specs=[pl.BlockSpec(memory_space=pl.ANY)] * 6,
        out_specs=pl.BlockSpec(memory_space=pl.ANY),
        out_shape=jax.ShapeDtypeStruct((m, d), F32),
        compiler_params=pltpu.CompilerParams(vmem_limit_bytes=VMEM_LIMIT_BYTES),
        name="ple",
    )(x2, p2d, gain_p, wpg, wpp, gain_f)


def kernel(x, p, rel_bias_table, attn_norm, w_in, sink_a, w_branch_a, w_branch_b, w_out,
           ffn_norm, w_ffn_gate, w_ffn_up, conv_w, conv_b, w_ffn_down,
           ple_norm, w_ple_gate, w_ple_proj, final_norm):
    b, s, d = x.shape
    depth = w_in.shape[0]
    assert d == D_MODEL and w_in.shape[2] == IN_PROJ_W and s % (Q_BLOCK * B_PATTERNS[-1][1]) == 0
    assert rel_bias_table.shape == (N_BUCKETS, N_BIAS_HEADS)
    m = b * s
    x2d = x.reshape(m, d)
    table_flat = rel_bias_table.reshape(-1)
    for i in range(depth):
        pb2d, pa2d = _inproj(x2d, attn_norm[i][None], w_in[i])
        pb = pb2d.reshape(PB_W // HEAD_DIM, b, s, HEAD_DIM)
        ya, wa = _attn_a(pb, table_flat, sink_a[i], w_branch_a[i])
        yb, wo = _attn_b(pb, table_flat, w_out[i])
        x1, hf = _mix(ya.reshape(m, A_Q_W), yb.reshape(m, B_GROUP_W), pa2d, x2d, wa,
                      w_branch_b[i].astype(BF16), wo, ffn_norm[i][None])
        act, wd = _ffn_up(hf, w_ffn_gate[i], w_ffn_up[i], conv_w[i], conv_b[i][None], w_ffn_down[i],
                          seq=s)
        x2, wpg = _ffn_down(act, wd, x1, w_ple_gate[i])
        x2d = _ple(x2, p[i].reshape(m, -1), ple_norm[i][None], wpg,
                   w_ple_proj[i].astype(BF16), final_norm[None], final=i == depth - 1)
    return x2d.reshape(b, s, d)
```
